```python
import math
import jax
import jax.numpy as jnp
from jax import lax
import numpy as np

D_MODEL = 1024
BATCH = 8
SEQ = 4096
DEPTH = 2

CHUNK = 64
Q_BLOCK = 128

N_MIXERS = 4
HEAD_DIM = 64
GROUP_WIDTH = D_MODEL // N_MIXERS
N_HEADS_GROUP = GROUP_WIDTH // HEAD_DIM
D_MIX = N_MIXERS * GROUP_WIDTH

MLA_Q_RANK = D_MODEL // 4
MLA_KV_RANK = D_MODEL // 8
MLA_NOPE_DIM = HEAD_DIM
MLA_ROPE_DIM = HEAD_DIM // 2
MLA_V_DIM = HEAD_DIM

RET_DECAY_OFFSET = 5.0

D_FF = 4 * D_MODEL

ROPE_BASE = 10000.0
EPS = 1e-6
FORGET_BIAS = 4.0

D_IN_PROJ = 10 * GROUP_WIDTH + N_HEADS_GROUP + MLA_Q_RANK + MLA_KV_RANK + MLA_ROPE_DIM

kernel_name = 'hybrid_fox_mla_retention_stickbreaking_trunk'

F32 = jnp.float32


def rms_norm(x, gain):
    xf = x.astype(F32)
    y = xf * lax.rsqrt(jnp.mean(xf * xf, axis=-1, keepdims=True) + EPS)
    return (y * gain.astype(F32)).astype(x.dtype)


def head_group_norm(o, gain):
    mu = jnp.mean(o, axis=-1, keepdims=True)
    var = jnp.mean(jnp.square(o - mu), axis=-1, keepdims=True)
    y = (o - mu) * lax.rsqrt(var + EPS)
    return y * gain.astype(F32).reshape(o.shape[2], o.shape[3])


def apply_rope(x, positions):
    half = x.shape[-1] // 2
    inv_freq = ROPE_BASE ** (-jnp.arange(half, dtype=F32) / half)
    ang = positions.astype(F32)[:, :, None, None] * inv_freq
    cos, sin = jnp.cos(ang), jnp.sin(ang)
    xf = x.astype(F32)
    x1, x2 = xf[..., :half], xf[..., half:]
    return jnp.concatenate([x1 * cos - x2 * sin, x1 * sin + x2 * cos], axis=-1).astype(x.dtype)


def to_blocks(t):
    b, s = t.shape[:2]
    return jnp.moveaxis(t.reshape((b, s // Q_BLOCK, Q_BLOCK) + t.shape[2:]), 1, 0)


def from_blocks(t):
    nb, b, qb = t.shape[:3]
    return jnp.moveaxis(t, 0, 1).reshape((b, nb * qb) + t.shape[3:])


def forgetting_attention(q, k, v, f_logit):
    seq = q.shape[1]
    scale = q.shape[-1] ** -0.5
    cum = jnp.cumsum(jax.nn.log_sigmoid(f_logit.astype(F32)), axis=1)
    cum_k = jnp.swapaxes(cum, 1, 2)
    k_pos = jnp.arange(seq)
    q_pos = k_pos.reshape(-1, Q_BLOCK)

    def block(xs):
        qb, cqb, qp = xs
        s = jnp.einsum('bqhd,bkhd->bhqk', qb, k, preferred_element_type=F32) * scale
        s = s + jnp.swapaxes(cqb, 1, 2)[..., None] - cum_k[:, :, None, :]
        s = jnp.where(k_pos[None, :] <= qp[:, None], s, -jnp.inf)
        p = jax.nn.softmax(s, axis=-1)
        return jnp.einsum('bhqk,bkhd->bqhd', p.astype(v.dtype), v)

    return from_blocks(lax.map(block, (to_blocks(q), to_blocks(cum), q_pos)))


def chunk_causal_softmax_attention(q, k, v):
    seq = q.shape[1]
    scale = q.shape[-1] ** -0.5
    k_chunk = jnp.arange(seq) // CHUNK
    q_pos = jnp.arange(seq).reshape(-1, Q_BLOCK)

    def block(xs):
        qb, qp = xs
        s = jnp.einsum('bqhd,bkhd->bhqk', qb, k, preferred_element_type=F32) * scale
        s = jnp.where(k_chunk[None, :] <= (qp // CHUNK)[:, None], s, -jnp.inf)
        p = jax.nn.softmax(s, axis=-1)
        return jnp.einsum('bhqk,bkhd->bqhd', p.astype(v.dtype), v)

    return from_blocks(lax.map(block, (to_blocks(q), q_pos)))


def stick_breaking_attention(q, k, v):
    seq = q.shape[1]
    scale = q.shape[-1] ** -0.5
    k_pos = jnp.arange(seq)
    q_pos = k_pos.reshape(-1, Q_BLOCK)

    def block(xs):
        qb, qp = xs
        z = jnp.einsum('bqhd,bkhd->bhqk', qb, k, preferred_element_type=F32) * scale
        visible = k_pos[None, :] < qp[:, None]
        log_stay = jnp.where(visible, jax.nn.log_sigmoid(-z), 0.0)
        later = lax.cumsum(log_stay, axis=3, reverse=True) - log_stay
        w = jnp.where(visible, jnp.exp(jax.nn.log_sigmoid(z) + later), 0.0)
        return jnp.einsum('bhqk,bkhd->bqhd', w.astype(v.dtype), v)

    return from_blocks(lax.map(block, (to_blocks(q), q_pos)))


def chunkwise_retention(q, k, v, positions):
    b, seq, h, d = q.shape
    n = seq // CHUNK
    qf = apply_rope(q, positions).astype(F32)
    kf = apply_rope(k, positions).astype(F32) * (d ** -0.5)
    vf = v.astype(F32)
    log_gamma = jnp.log1p(-jnp.power(2.0, -RET_DECAY_OFFSET - jnp.arange(h, dtype=F32)))
    idx = jnp.arange(CHUNK, dtype=F32)
    qc = qf.reshape(b, n, CHUNK, h, d)
    kc = kf.reshape(b, n, CHUNK, h, d)
    vc = vf.reshape(b, n, CHUNK, h, d)
    intra_decay = jnp.exp(log_gamma[:, None, None] * jnp.abs(idx[:, None] - idx[None, :]))
    scores = jnp.einsum('bncht,bnmht->bnhcm', qc, kc) * intra_decay
    intra = jnp.einsum('bnhcm,bnmhe->bnche', scores, vc)
    k_tail = kc * jnp.exp(log_gamma[None, :] * (CHUNK - 1 - idx)[:, None])[None, None, :, :, None]
    chunk_kv = jnp.einsum('bnmht,bnmhe->nbhte', k_tail, vc)
    chunk_decay = jnp.exp(log_gamma * CHUNK)[None, :, None, None]

    def step(state, kv):
        return state * chunk_decay + kv, state

    _, prev_state = lax.scan(step, jnp.zeros((b, h, d, d), F32), chunk_kv)
    q_head = qc * jnp.exp(log_gamma[None, :] * (idx + 1.0)[:, None])[None, None, :, :, None]
    inter = jnp.einsum('bncht,nbhte->bnche', q_head, prev_state)
    return (intra + inter).reshape(b, seq, h, d)


def split_columns(proj):
    sizes = [GROUP_WIDTH, GROUP_WIDTH, GROUP_WIDTH, N_HEADS_GROUP,
             MLA_Q_RANK, MLA_KV_RANK, MLA_ROPE_DIM,
             GROUP_WIDTH, GROUP_WIDTH, GROUP_WIDTH, GROUP_WIDTH,
             GROUP_WIDTH, GROUP_WIDTH, GROUP_WIDTH]
    offsets = [int(o) for o in np.cumsum(sizes)[:-1]]
    return jnp.split(proj, offsets, axis=-1)


def hybrid_mixer(h, positions, w_in, b_forget, g_q_lora, w_q_up, g_kv_lora, w_kv_up, g_mix_out, w_out):
    b, seq, _ = h.shape
    heads = lambda t: t.reshape(b, seq, N_HEADS_GROUP, -1)
    proj = jnp.einsum('bsd,dn->bsn', h, w_in)
    (fq, fk, fv, ff, cq, ckv, kr, rq, rk, rv, rg, sq, sk, sv) = split_columns(proj)

    out_a = forgetting_attention(heads(fq), heads(fk), heads(fv), ff + b_forget)
    out_a = rms_norm(out_a.reshape(b, seq, GROUP_WIDTH), g_mix_out[0:GROUP_WIDTH])

    q = jnp.einsum('bsr,rn->bsn', rms_norm(cq, g_q_lora), w_q_up).reshape(b, seq, N_HEADS_GROUP, MLA_NOPE_DIM + MLA_ROPE_DIM)
    q = jnp.concatenate([q[..., :MLA_NOPE_DIM], apply_rope(q[..., MLA_NOPE_DIM:], positions)], axis=-1)
    kv = jnp.einsum('bsr,rn->bsn', rms_norm(ckv, g_kv_lora), w_kv_up).reshape(b, seq, N_HEADS_GROUP, MLA_NOPE_DIM + MLA_V_DIM)
    k_rope = apply_rope(kr[:, :, None, :], positions)
    k = jnp.concatenate([kv[..., :MLA_NOPE_DIM], jnp.broadcast_to(k_rope, (b, seq, N_HEADS_GROUP, MLA_ROPE_DIM))], axis=-1)
    out_b = chunk_causal_softmax_attention(q, k, kv[..., MLA_NOPE_DIM:])
    out_b = rms_norm(out_b.reshape(b, seq, GROUP_WIDTH), g_mix_out[GROUP_WIDTH:2 * GROUP_WIDTH])

    ret = chunkwise_retention(heads(rq), heads(rk), heads(rv), positions)
    ret = head_group_norm(ret, g_mix_out[2 * GROUP_WIDTH:3 * GROUP_WIDTH]).reshape(b, seq, GROUP_WIDTH)
    out_c = (ret * jax.nn.silu(rg.astype(F32))).astype(h.dtype)

    out_d = stick_breaking_attention(heads(sq), heads(sk), heads(sv))
    out_d = rms_norm(out_d.reshape(b, seq, GROUP_WIDTH), g_mix_out[3 * GROUP_WIDTH:])

    mixed = jnp.concatenate([out_a, out_b, out_c, out_d], axis=-1)
    return jnp.einsum('bsn,nd->bsd', mixed, w_out)


def squared_relu_mlp(h, w_up, w_down):
    u = jnp.square(jax.nn.relu(jnp.einsum('bsd,df->bsf', h, w_up)))
    return jnp.einsum('bsf,fd->bsd', u, w_down)


def setup_inputs(seed: int = 0) -> dict:
    key = jax.random.key(seed)
    ks = jax.random.split(key, 20)
    nrm = lambda k, shape, fan_in: jax.random.normal(k, shape, F32) * (fan_in ** -0.5)
    gain = lambda k, shape: 1.0 + 0.05 * jax.random.normal(k, shape, F32)
    x = jax.random.normal(ks[0], (BATCH, SEQ, D_MODEL), F32)
    start = jax.random.randint(ks[1], (BATCH, 1), 0, 1024, dtype=jnp.int32)
    positions = start + jnp.arange(SEQ, dtype=jnp.int32)[None, :]
    return {
        'x': x,
        'positions': positions,
        'g_mix_pre': gain(ks[2], (DEPTH, D_MODEL)),
        'w_in': nrm(ks[3], (DEPTH, D_MODEL, D_IN_PROJ), D_MODEL),
        'b_forget': FORGET_BIAS + 0.5 * jax.random.normal(ks[4], (DEPTH, N_HEADS_GROUP), F32),
        'g_q_lora': gain(ks[5], (DEPTH, MLA_Q_RANK)),
        'w_q_up': nrm(ks[6], (DEPTH, MLA_Q_RANK, N_HEADS_GROUP * (MLA_NOPE_DIM + MLA_ROPE_DIM)), MLA_Q_RANK),
        'g_kv_lora': gain(ks[7], (DEPTH, MLA_KV_RANK)),
        'w_kv_up': nrm(ks[8], (DEPTH, MLA_KV_RANK, N_HEADS_GROUP * (MLA_NOPE_DIM + MLA_V_DIM)), MLA_KV_RANK),
        'g_mix_out': gain(ks[9], (DEPTH, D_MIX)),
        'w_out': nrm(ks[10], (DEPTH, D_MIX, D_MODEL), D_MIX),
        'g_mix_post': gain(ks[11], (DEPTH, D_MODEL)),
        'g_ffn_pre': gain(ks[12], (DEPTH, D_MODEL)),
        'w_ffn_up': nrm(ks[13], (DEPTH, D_MODEL, D_FF), D_MODEL),
        'w_ffn_down': nrm(ks[14], (DEPTH, D_FF, D_MODEL), D_FF),
        'g_ffn_post': gain(ks[15], (DEPTH, D_MODEL)),
    }


def reference(x, positions, g_mix_pre, w_in, b_forget, g_q_lora, w_q_up, g_kv_lora, w_kv_up,
              g_mix_out, w_out, g_mix_post, g_ffn_pre, w_ffn_up, w_ffn_down, g_ffn_post):
    for layer in range(DEPTH):
        h = rms_norm(x, g_mix_pre[layer])
        mix = hybrid_mixer(h, positions, w_in[layer], b_forget[layer], g_q_lora[layer], w_q_up[layer],
                           g_kv_lora[layer], w_kv_up[layer], g_mix_out[layer], w_out[layer])
        x = x + rms_norm(mix, g_mix_post[layer])
        h = rms_norm(x, g_ffn_pre[layer])
        x = x + rms_norm(squared_relu_mlp(h, w_ffn_up[layer], w_ffn_down[layer]), g_ffn_post[layer])
    return x
```

```python
import functools

import numpy as np
import jax
import jax.numpy as jnp
from jax import lax
from jax.experimental import pallas as pl
from jax.experimental.pallas import tpu as pltpu

F32 = jnp.float32
BF16 = jnp.bfloat16

N_HEADS = 4
HEAD_DIM = 64
GROUP = N_HEADS * HEAD_DIM
CHUNK = 64
CHUNK_SHIFT = 6
MLA_ROPE = 32
MLA_QK = HEAD_DIM + MLA_ROPE
MLA_QW = GROUP + N_HEADS * MLA_ROPE
ROPE_BASE = 10000.0
EPS = 1e-6
RET_DECAY_OFFSET = 5.0
NEG_BIG = -1e30

LANES = 128
VMEM_LIMIT_BYTES = 56 * 1024 * 1024

TM_PROJ = 512
TM_POST = 512
TQ_ATTN = 512
TK_ATTN = 512
T_RET = 256
TM_TAB = 1024

C_FOX = 0
C_FF = 768
C_CQ = 896
C_CKV = 1152
C_KR = 1280
C_RET = 1536
C_RVG = 2560
C_SB = 3072
C_END = 3840


def _rms(x, g):
    return x * lax.rsqrt(jnp.mean(x * x, axis=-1, keepdims=True) + EPS) * g


def _log_sigmoid(x):
    return jnp.minimum(x, 0.0) - jnp.log1p(jnp.exp(-jnp.abs(x)))


def _dot(a, b):
    return jnp.dot(a, b, preferred_element_type=F32)


def _dot_nt(a, b):
    return lax.dot_general(a, b, (((1,), (1,)), ((), ())), preferred_element_type=F32)


def _dot_f32(a, b):
    return jnp.dot(a, b, preferred_element_type=F32, precision=lax.Precision.HIGHEST)


def _head_select(lane_head, per_head):
    out = per_head[N_HEADS - 1]
    for h in range(N_HEADS - 2, -1, -1):
        out = jnp.where(lane_head == h, per_head[h], out)
    return out


def _table_kernel(pos_ref, invf_ref, sign_ref, tab_ref):
    ang = pos_ref[...].astype(F32) * invf_ref[...]
    c = jnp.cos(ang)
    s = jnp.sin(ang) * sign_ref[...]
    tab_ref[...] = jnp.concatenate(
        [c[:, :LANES], s[:, :LANES], c[:, LANES:], s[:, LANES:]], axis=-1)


def _rope_tables(positions):
    b, s = positions.shape
    half_r, half_m = HEAD_DIM // 2, MLA_ROPE // 2
    invf_r = ROPE_BASE ** (-jnp.arange(half_r, dtype=F32) / half_r)
    invf_m = ROPE_BASE ** (-jnp.arange(half_m, dtype=F32) / half_m)
    invf = jnp.concatenate([jnp.tile(invf_r, LANES // half_r),
                            jnp.tile(invf_m, LANES // half_m)])[None, :]
    lane = np.arange(LANES)
    sign = np.concatenate([np.where(lane % HEAD_DIM < half_r, -1.0, 1.0),
                           np.where(lane % MLA_ROPE < half_m, -1.0, 1.0)])
    sign = jnp.asarray(sign, F32)[None, :]
    tm = min(TM_TAB, s)
    return pl.pallas_call(
        _table_kernel,
        out_shape=jax.ShapeDtypeStruct((b, s, 4 * LANES), F32),
        grid=(b, s // tm),
        in_specs=[pl.BlockSpec((None, tm, 1), lambda i, j: (i, j, 0)),
                  pl.BlockSpec((1, 2 * LANES), lambda i, j: (0, 0)),
                  pl.BlockSpec((1, 2 * LANES), lambda i, j: (0, 0))],
        out_specs=pl.BlockSpec((None, tm, 4 * LANES), lambda i, j: (i, j, 0)),
        compiler_params=pltpu.CompilerParams(
            dimension_semantics=("parallel", "parallel")),
        name="rope_tables",
    )(positions[:, :, None], invf, sign)


def _proj_kernel(x_ref, tab_ref, g_ref, w_ref, bf_ref, gq_ref, wq_ref, gkv_ref, wkv_ref,
                 fq_o, fk_o, fv_o, cum_o, mq_o, mk_o, mv_o,
                 rq_o, rk_o, rv_o, rg_o, sq_o, sk_o, sv_o, carry_ref, *, mla_scale):
    tm = x_ref.shape[0]
    h = _rms(x_ref[...], g_ref[...]).astype(BF16)

    def proj(a, b):
        return _dot(h, w_ref[:, a:b])

    tab = tab_ref[...]
    cos_r, sin_r = tab[:, 0:LANES], tab[:, LANES:2 * LANES]
    cos_m, sin_m = tab[:, 2 * LANES:3 * LANES], tab[:, 3 * LANES:]
    cos_r2 = jnp.concatenate([cos_r, cos_r], axis=-1)
    sin_r2 = jnp.concatenate([sin_r, sin_r], axis=-1)

    p = proj(C_FOX, C_FF)
    fq_o[...] = p[:, 0:GROUP].astype(BF16)
    fk_o[...] = p[:, GROUP:2 * GROUP].astype(BF16)
    fv_o[...] = p[:, 2 * GROUP:].astype(BF16)

    @pl.when(pl.program_id(1) == 0)
    def _():
        carry_ref[...] = jnp.zeros_like(carry_ref)

    ls = _log_sigmoid(proj(C_FF, C_CQ) + bf_ref[...])
    row = lax.broadcasted_iota(jnp.int32, (tm, tm), 0)
    col = lax.broadcasted_iota(jnp.int32, (tm, tm), 1)
    tri = jnp.where(col <= row, 1.0, 0.0).astype(F32)
    cum = _dot_f32(tri, ls) + carry_ref[...]
    cum_o[...] = cum
    carry_ref[...] = cum[tm - 1:tm, :]

    hq = _rms(proj(C_CQ, C_CKV), gq_ref[...]).astype(BF16)
    qm = _dot(hq, wq_ref[...])
    q_rope = qm[:, GROUP:GROUP + LANES] * cos_m + qm[:, GROUP + LANES:] * sin_m
    mq_o[...] = (jnp.concatenate([qm[:, :GROUP], q_rope], axis=-1) * mla_scale).astype(BF16)
    hkv = _rms(proj(C_CKV, C_KR), gkv_ref[...]).astype(BF16)
    kv = _dot(hkv, wkv_ref[...])
    pk = proj(C_KR, C_RET)
    k_rope = pk[:, :LANES] * cos_m + pk[:, LANES:] * sin_m
    mk_o[...] = jnp.concatenate([kv[:, :GROUP], k_rope], axis=-1).astype(BF16)
    mv_o[...] = kv[:, GROUP:].astype(BF16)

    pr = proj(C_RET, C_RVG)
    rq_o[...] = pr[:, 0:GROUP] * cos_r2 + pr[:, GROUP:2 * GROUP] * sin_r2
    rk_o[...] = pr[:, 2 * GROUP:3 * GROUP] * cos_r2 + pr[:, 3 * GROUP:] * sin_r2
    pv = proj(C_RVG, C_SB)
    rv_o[...] = pv[:, :GROUP].astype(BF16)
    rg_o[...] = pv[:, GROUP:]

    ps = proj(C_SB, C_END)
    sq_o[...] = ps[:, 0:GROUP].astype(BF16)
    sk_o[...] = ps[:, GROUP:2 * GROUP].astype(BF16)
    sv_o[...] = ps[:, 2 * GROUP:].astype(BF16)


def _projection(x, tab, g_pre, w_cat, bf_pad, g_q, w_q, g_kv, w_kv):
    b, s, d = x.shape
    tm = min(TM_PROJ, s)
    row = lambda w: pl.BlockSpec((None, tm, w), lambda i, j: (i, j, 0))
    const = lambda shape: pl.BlockSpec(shape, lambda i, j: (0,) * len(shape))
    bf = lambda w: jax.ShapeDtypeStruct((b, s, w), BF16)
    f32 = lambda w: jax.ShapeDtypeStruct((b, s, w), F32)
    out_shape = [bf(GROUP), bf(GROUP), bf(GROUP), f32(LANES),
                 bf(MLA_QW), bf(MLA_QW), bf(GROUP),
                 f32(GROUP), f32(GROUP), bf(GROUP), f32(GROUP),
                 bf(GROUP), bf(GROUP), bf(GROUP)]
    out_specs = [row(o.shape[-1]) for o in out_shape]
    return pl.pallas_call(
        functools.partial(_proj_kernel, mla_scale=MLA_QK ** -0.5),
        out_shape=out_shape,
        grid=(b, s // tm),
        in_specs=[row(d), row(4 * LANES), const((1, d)), const(w_cat.shape),
                  const((1, LANES)), const(g_q.shape), const(w_q.shape),
                  const(g_kv.shape), const(w_kv.shape)],
        out_specs=out_specs,
        scratch_shapes=[pltpu.VMEM((1, LANES), F32)],
        compiler_params=pltpu.CompilerParams(
            dimension_semantics=("parallel", "arbitrary"),
            vmem_limit_bytes=VMEM_LIMIT_BYTES),
        name="in_projection",
    )(x, tab, g_pre, w_cat, bf_pad, g_q, w_q, g_kv, w_kv)


def _softmax_attn_kernel(*refs, fox, tq, tk):
    if fox:
        q_ref, k_ref, v_ref, cq_ref, ck_ref, gain_ref, o_ref, acc_ref, m_ref, l_ref = refs
    else:
        q_ref, k_ref, v_ref, gain_ref, o_ref, acc_ref, m_ref, l_ref = refs
    i = pl.program_id(1)
    qw = q_ref.shape[-1]
    q = q_ref[...]
    lane_q = lax.broadcasted_iota(jnp.int32, (1, qw), 1)
    if fox:
        head_q = lane_q >> 6
    else:
        head_q = jnp.where(lane_q < GROUP, lane_q >> 6, (lane_q - GROUP) >> 5)
    head_v = lax.broadcasted_iota(jnp.int32, (1, GROUP), 1) >> 6
    zq = jnp.zeros_like(q)
    qh = [jnp.where(head_q == h, q, zq) for h in range(N_HEADS)]
    if fox:
        cqb = cq_ref[...]
        cq = [cqb[:, h:h + 1] for h in range(N_HEADS)]

    acc_ref[...] = jnp.zeros_like(acc_ref)
    m_ref[...] = jnp.full_like(m_ref, NEG_BIG)
    l_ref[...] = jnp.zeros_like(l_ref)

    def block(j, masked):
        start = pl.multiple_of(j * tk, tk)
        kb = k_ref[pl.ds(start, tk), :]
        vb = v_ref[pl.ds(start, tk), :]
        zv = jnp.zeros_like(vb)
        if masked:
            qpos = i * tq + lax.broadcasted_iota(jnp.int32, (tq, tk), 0)
            kpos = j * tk + lax.broadcasted_iota(jnp.int32, (tq, tk), 1)
            if fox:
                visible = kpos <= qpos
            else:
                visible = (kpos >> CHUNK_SHIFT) <= (qpos >> CHUNK_SHIFT)
        if fox:
            ckb = ck_ref[j]
        alphas = []
        pv = None
        for h in range(N_HEADS):
            s = _dot_nt(qh[h], kb)
            if fox:
                s = s - ckb[h:h + 1, :]
            if masked:
                s = jnp.where(visible, s, NEG_BIG)
            smax = jnp.max(s, axis=-1, keepdims=True)
            if fox:
                smax = smax + cq[h]
            m_prev = m_ref[h]
            m_new = jnp.maximum(m_prev, smax)
            alpha = jnp.exp(m_prev - m_new)
            shift = (cq[h] - m_new) if fox else (-m_new)
            p = jnp.exp(s + shift)
            l_ref[h] = alpha * l_ref[h] + jnp.sum(p, axis=-1, keepdims=True)
            m_ref[h] = m_new
            alphas.append(alpha)
            d = _dot(p.astype(BF16), jnp.where(head_v == h, vb, zv))
            pv = d if pv is None else pv + d
        acc_ref[...] = acc_ref[...] * _head_select(head_v, alphas) + pv

    n_full = (i * tq) // tk

    def body(j, carry):
        block(j, False)
        return carry

    lax.fori_loop(0, n_full, body, 0)
    block(n_full, True)

    inv_l = _head_select(head_v, [1.0 / l_ref[h] for h in range(N_HEADS)])
    o = acc_ref[...] * inv_l
    o_ref[...] = _rms(o, gain_ref[...]).astype(o_ref.dtype)


def _softmax_attention(q, k, v, gain, cum=None):
    b, s, qw = q.shape
    tq, tk = min(TQ_ATTN, s), min(TK_ATTN, s)
    fox = cum is not None
    in_specs = [pl.BlockSpec((None, tq, qw), lambda i, j: (i, j, 0)),
                pl.BlockSpec((None, s, qw), lambda i, j: (i, 0, 0)),
                pl.BlockSpec((None, s, GROUP), lambda i, j: (i, 0, 0))]
    args = [q, k, v]
    if fox:
        nk = s // tk
        ck = cum[:, :, :8].reshape(b, nk, tk, 8).transpose(0, 1, 3, 2)
        in_specs += [pl.BlockSpec((None, tq, LANES), lambda i, j: (i, j, 0)),
                     pl.BlockSpec((None, nk, 8, tk), lambda i, j: (i, 0, 0, 0))]
        args += [cum, ck]
    in_specs.append(pl.BlockSpec((1, GROUP), lambda i, j: (0, 0)))
    args.append(gain)
    return pl.pallas_call(
        functools.partial(_softmax_attn_kernel, fox=fox, tq=tq, tk=tk),
        out_shape=jax.ShapeDtypeStruct((b, s, GROUP), BF16),
        grid=(b, s // tq),
        in_specs=in_specs,
        out_specs=pl.BlockSpec((None, tq, GROUP), lambda i, j: (i, j, 0)),
        scratch_shapes=[pltpu.VMEM((tq, GROUP), F32),
                        pltpu.VMEM((N_HEADS, tq, 1), F32),
                        pltpu.VMEM((N_HEADS, tq, 1), F32)],
        compiler_params=pltpu.CompilerParams(
            dimension_semantics=("parallel", "parallel"),
            vmem_limit_bytes=VMEM_LIMIT_BYTES),
        name="fox_attention" if fox else "mla_attention",
    )(*args)


def _stick_kernel(q_ref, k_ref, v_ref, gain_ref, o_ref, acc_ref, r_ref, *, tq, tk):
    i = pl.program_id(1)
    q = q_ref[...]
    head = lax.broadcasted_iota(jnp.int32, (1, GROUP), 1) >> 6
    zq = jnp.zeros_like(q)
    qh = [jnp.where(head == h, q, zq) for h in range(N_HEADS)]
    rj = lax.broadcasted_iota(jnp.int32, (tk, tk), 0)
    cs = lax.broadcasted_iota(jnp.int32, (tk, tk), 1)
    upper = jnp.where(rj > cs, 1.0, 0.0).astype(BF16)

    acc_ref[...] = jnp.zeros_like(acc_ref)
    r_ref[...] = jnp.zeros_like(r_ref)

    def block(j, masked):
        start = pl.multiple_of(j * tk, tk)
        kb = k_ref[pl.ds(start, tk), :]
        vb = v_ref[pl.ds(start, tk), :]
        zv = jnp.zeros_like(vb)
        if masked:
            qpos = i * tq + lax.broadcasted_iota(jnp.int32, (tq, tk), 0)
            kpos = j * tk + lax.broadcasted_iota(jnp.int32, (tq, tk), 1)
            visible = kpos < qpos
        pv = None
        for h in range(N_HEADS):
            z = _dot_nt(qh[h], kb)
            log_stay = _log_sigmoid(-z)
            if masked:
                log_stay = jnp.where(visible, log_stay, 0.0)
            hi = log_stay.astype(BF16)
            lo = (log_stay - hi.astype(F32)).astype(BF16)
            later = _dot(hi, upper) + _dot(lo, upper)
            r_prev = r_ref[h]
            w = jnp.exp(z + log_stay + later + r_prev)
            if masked:
                w = jnp.where(visible, w, 0.0)
            r_ref[h] = r_prev + jnp.sum(log_stay, axis=-1, keepdims=True)
            d = _dot(w.astype(BF16), jnp.where(head == h, vb, zv))
            pv = d if pv is None else pv + d
        acc_ref[...] += pv

    n_full = (i * tq) // tk
    block(n_full, True)

    def body(t, carry):
        block(n_full - 1 - t, False)
        return carry

    lax.fori_loop(0, n_full, body, 0)
    o_ref[...] = _rms(acc_ref[...], gain_ref[...]).astype(o_ref.dtype)


def _stick_attention(q, k, v, gain):
    b, s, _ = q.shape
    tq, tk = min(TQ_ATTN, s), min(TK_ATTN, s)
    return pl.pallas_call(
        functools.partial(_stick_kernel, tq=tq, tk=tk),
        out_shape=jax.ShapeDtypeStruct((b, s, GROUP), BF16),
        grid=(b, s // tq),
        in_specs=[pl.BlockSpec((None, tq, GROUP), lambda i, j: (i, j, 0)),
                  pl.BlockSpec((None, s, GROUP), lambda i, j: (i, 0, 0)),
                  pl.BlockSpec((None, s, GROUP), lambda i, j: (i, 0, 0)),
                  pl.BlockSpec((1, GROUP), lambda i, j: (0, 0))],
        out_specs=pl.BlockSpec((None, tq, GROUP), lambda i, j: (i, j, 0)),
        scratch_shapes=[pltpu.VMEM((tq, GROUP), F32),
                        pltpu.VMEM((N_HEADS, tq, 1), F32)],
        compiler_params=pltpu.CompilerParams(
            dimension_semantics=("parallel", "parallel"),
            vmem_limit_bytes=VMEM_LIMIT_BYTES),
        name="stick_attention",
    )(q, k, v, gain)


def _retention_kernel(q_ref, k_ref, v_ref, g_ref, lg_ref, gain_ref, o_ref, state_ref, *, t):
    @pl.when(pl.program_id(1) == 0)
    def _():
        state_ref[...] = jnp.zeros_like(state_ref)

    q = q_ref[...]
    k = k_ref[...]
    v = v_ref[...]
    lg = lg_ref[...]
    head = lax.broadcasted_iota(jnp.int32, (1, GROUP), 1) >> 6
    head_r = lax.broadcasted_iota(jnp.int32, (GROUP, 1), 0) >> 6
    same_head = head_r == head
    ri = lax.broadcasted_iota(jnp.int32, (t, t), 0)
    ci = lax.broadcasted_iota(jnp.int32, (t, t), 1)
    dist = jnp.abs(ri - ci).astype(F32)
    reach = (ci >> CHUNK_SHIFT) <= (ri >> CHUNK_SHIFT)
    pos = lax.broadcasted_iota(jnp.int32, (t, 1), 0).astype(F32)

    q_b = q.astype(BF16)
    k_b = k.astype(BF16)
    zq = jnp.zeros_like(q_b)
    zv = jnp.zeros_like(v)
    out = None
    for h in range(N_HEADS):
        lg_h = lg[:, h * HEAD_DIM:h * HEAD_DIM + 1]
        s = _dot_nt(jnp.where(head == h, q_b, zq), k_b)
        w = s * jnp.where(reach, jnp.exp(lg_h * dist), 0.0)
        d = _dot(w.astype(BF16), jnp.where(head == h, v, zv))
        out = d if out is None else out + d

    state = state_ref[...]
    q_head = (q * jnp.exp(lg * (pos + 1.0))).astype(BF16)
    out = out + _dot(q_head, state.astype(BF16))

    k_tail = k * jnp.exp(lg * (float(t - 1) - pos))
    kv = _dot(k_tail.T.astype(BF16), v)
    state_ref[...] = state * jnp.exp(lg * float(t)) + jnp.where(same_head, kv, 0.0)

    avg = jnp.where(same_head, 1.0 / HEAD_DIM, 0.0).astype(F32)
    mu = _dot_f32(out, avg)
    cen = out - mu
    var = _dot_f32(cen * cen, avg)
    y = cen * lax.rsqrt(var + EPS) * gain_ref[...]
    g = g_ref[...]
    o_ref[...] = (y * (g * (1.0 / (1.0 + jnp.exp(-g))))).astype(o_ref.dtype)


def _retention(q, k, v, g, log_gamma, gain):
    b, s, _ = q.shape
    t = min(T_RET, s)
    row = pl.BlockSpec((None, t, GROUP), lambda i, j: (i, j, 0))
    const = pl.BlockSpec((1, GROUP), lambda i, j: (0, 0))
    return pl.pallas_call(
        functools.partial(_retention_kernel, t=t),
        out_shape=jax.ShapeDtypeStruct((b, s, GROUP), BF16),
        grid=(b, s // t),
        in_specs=[row, row, row, row, const, const],
        out_specs=row,
        scratch_shapes=[pltpu.VMEM((GROUP, GROUP), F32)],
        compiler_params=pltpu.CompilerParams(
            dimension_semantics=("parallel", "arbitrary"),
            vmem_limit_bytes=VMEM_LIMIT_BYTES),
        name="retention",
    )(q, k, v, g, log_gamma, gain)


def _post_kernel(x_ref, a_ref, b_ref, c_ref, d_ref, wo_ref, gpost_ref, gpre_ref,
                 wup_ref, wdn_ref, gfpost_ref, o_ref, *, f_chunk):
    mixed = jnp.concatenate([a_ref[...], b_ref[...], c_ref[...], d_ref[...]], axis=-1)
    x1 = x_ref[...] + _rms(_dot(mixed, wo_ref[...]), gpost_ref[...])
    h = _rms(x1, gpre_ref[...]).astype(BF16)
    y = None
    for c in range(wup_ref.shape[1] // f_chunk):
        u = jnp.maximum(_dot(h, wup_ref[:, c * f_chunk:(c + 1) * f_chunk]), 0.0)
        d = _dot((u * u).astype(BF16), wdn_ref[c * f_chunk:(c + 1) * f_chunk, :])
        y = d if y is None else y + d
    o_ref[...] = x1 + _rms(y, gfpost_ref[...])


def _post(x, a, bb, c, dd, w_out, g_post, g_pre, w_up, w_dn, g_fpost):
    b, s, d = x.shape
    tm = min(TM_POST, s)
    row = lambda w: pl.BlockSpec((None, tm, w), lambda i, j: (i, j, 0))
    const = lambda shape: pl.BlockSpec(shape, lambda i, j: (0,) * len(shape))
    return pl.pallas_call(
        functools.partial(_post_kernel, f_chunk=1024),
        out_shape=jax.ShapeDtypeStruct((b, s, d), F32),
        grid=(b, s // tm),
        in_specs=[row(d), row(GROUP), row(GROUP), row(GROUP), row(GROUP),
                  const(w_out.shape), const((1, d)), const((1, d)),
                  const(w_up.shape), const(w_dn.shape), const((1, d))],
        out_specs=row(d),
        compiler_params=pltpu.CompilerParams(
            dimension_semantics=("parallel", "parallel"),
            vmem_limit_bytes=VMEM_LIMIT_BYTES),
        name="out_proj_mlp",
    )(x, a, bb, c, dd, w_out, g_post, g_pre, w_up, w_dn, g_fpost)


def _rot_perm(width, period):
    j = np.arange(width)
    return (j // period) * period + (j % period + period // 2) % period


def _layer_weights(w_in, b_forget, w_q_up, w_kv_up):
    sizes = [GROUP, GROUP, GROUP, N_HEADS, 256, 128, MLA_ROPE] + [GROUP] * 7
    offs = np.concatenate([[0], np.cumsum(sizes)])
    (fq, fk, fv, ff, cq, ckv, kr, rq, rk, rv, rg, sq, sk, sv) = [
        w_in[:, int(offs[n]):int(offs[n + 1])] for n in range(len(sizes))]
    d = w_in.shape[0]
    scale = HEAD_DIM ** -0.5
    ret_perm = _rot_perm(GROUP, HEAD_DIM)
    kr_perm = _rot_perm(MLA_ROPE, MLA_ROPE)
    ff_pad = jnp.concatenate([ff, jnp.zeros((d, LANES - N_HEADS), F32)], axis=1)
    w_cat = jnp.concatenate(
        [fq * scale, fk, fv, ff_pad, cq, ckv,
         jnp.tile(kr, (1, N_HEADS)), jnp.tile(kr[:, kr_perm], (1, N_HEADS)),
         rq, rq[:, ret_perm], rk * scale, rk[:, ret_perm] * scale, rv, rg,
         sq * scale, sk, sv], axis=1).astype(BF16)
    bf_pad = jnp.concatenate([b_forget, jnp.zeros((LANES - N_HEADS,), F32)])[None, :]
    wq3 = w_q_up.reshape(w_q_up.shape[0], N_HEADS, MLA_QK)
    q_nope = wq3[:, :, :HEAD_DIM].reshape(-1, GROUP)
    q_rope = wq3[:, :, HEAD_DIM:].reshape(-1, N_HEADS * MLA_ROPE)
    w_q = jnp.concatenate(
        [q_nope, q_rope, q_rope[:, _rot_perm(N_HEADS * MLA_ROPE, MLA_ROPE)]], axis=1).astype(BF16)
    wkv3 = w_kv_up.reshape(w_kv_up.shape[0], N_HEADS, 2 * HEAD_DIM)
    w_kv = jnp.concatenate([wkv3[:, :, :HEAD_DIM].reshape(-1, GROUP),
                            wkv3[:, :, HEAD_DIM:].reshape(-1, GROUP)], axis=1).astype(BF16)
    return w_cat, bf_pad, w_q, w_kv


def kernel(x, positions, g_mix_pre, w_in, b_forget, g_q_lora, w_q_up, g_kv_lora, w_kv_up,
           g_mix_out, w_out, g_mix_post, g_ffn_pre, w_ffn_up, w_ffn_down, g_ffn_post):
    depth = w_in.shape[0]
    tab = _rope_tables(positions)
    log_gamma = jnp.log1p(-jnp.power(2.0, -RET_DECAY_OFFSET - jnp.arange(N_HEADS, dtype=F32)))
    log_gamma = jnp.repeat(log_gamma, HEAD_DIM)[None, :]
    for layer in range(depth):
        w_cat, bf_pad, w_q, w_kv = _layer_weights(
            w_in[layer], b_forget[layer], w_q_up[layer], w_kv_up[layer])
        (fq, fk, fv, cum, mq, mk, mv, rq, rk, rv, rg, sq, sk, sv) = _projection(
            x, tab, g_mix_pre[layer][None, :], w_cat, bf_pad,
            g_q_lora[layer][None, :], w_q, g_kv_lora[layer][None, :], w_kv)
        gmo = g_mix_out[layer]
        out_a = _softmax_attention(fq, fk, fv, gmo[None, 0:GROUP], cum=cum)
        out_b = _softmax_attention(mq, mk, mv, gmo[None, GROUP:2 * GROUP])
        out_c = _retention(rq, rk, rv, rg, log_gamma, gmo[None, 2 * GROUP:3 * GROUP])
        out_d = _stick_attention(sq, sk, sv, gmo[None, 3 * GROUP:])
        x = _post(x, out_a, out_b, out_c, out_d, w_out[layer].astype(BF16),
                  g_mix_post[layer][None, :], g_ffn_pre[layer][None, :],
                  w_ffn_up[layer].astype(BF16), w_ffn_down[layer].astype(BF16),
                  g_ffn_post[layer][None, :])
    return x
```

```python
import functools

import numpy as np
import jax
import jax.numpy as jnp
from jax import lax
from jax.experimental import pallas as pl
from jax.experimental.pallas import tpu as pltpu

F32 = jnp.float32
BF16 = jnp.bfloat16

N_HEADS = 4
HEAD_DIM = 64
GROUP = N_HEADS * HEAD_DIM
CHUNK = 64
CHUNK_SHIFT = 6
MLA_ROPE = 32
MLA_QK = HEAD_DIM + MLA_ROPE
ROPE_BASE = 10000.0
EPS = 1e-6
RET_DECAY_OFFSET = 5.0
NEG_BIG = -1e30

LANES = 128
VMEM_LIMIT_BYTES = 56 * 1024 * 1024

T_KEY = 512
TM_POST = 512
TQ_ATTN = 512
TC_STICK = 256
T_RET = 256
TM_TAB = 1024

C_FOX = 0
C_FF = 768
C_CQ = 896
C_CKV = 1152
C_KR = 1280
C_RET = 1536
C_RVG = 2560
C_SB = 3072
C_END = 3840


def _rms(x, g):
    return x * lax.rsqrt(jnp.mean(x * x, axis=-1, keepdims=True) + EPS) * g


def _log_sigmoid(x):
    return jnp.minimum(x, 0.0) - jnp.log1p(jnp.exp(-jnp.abs(x)))


def _dot(a, b):
    return jnp.dot(a, b, preferred_element_type=F32)


def _dot_nt(a, b):
    return lax.dot_general(a, b, (((1,), (1,)), ((), ())), preferred_element_type=F32)


def _dot_f32(a, b):
    return jnp.dot(a, b, preferred_element_type=F32, precision=lax.Precision.HIGHEST)


def _head_select(lane_head, per_head):
    out = per_head[N_HEADS - 1]
    for h in range(N_HEADS - 2, -1, -1):
        out = jnp.where(lane_head == h, per_head[h], out)
    return out


def _per_head(fn, x, tq):
    return jnp.concatenate([fn(h, x[h * tq:(h + 1) * tq]) for h in range(N_HEADS)], axis=0)


def _table_kernel(pos_ref, invf_ref, sign_ref, tab_ref):
    ang = pos_ref[...].astype(F32) * invf_ref[...]
    c = jnp.cos(ang)
    s = jnp.sin(ang) * sign_ref[...]
    tab_ref[...] = jnp.concatenate(
        [c[:, :LANES], s[:, :LANES], c[:, LANES:], s[:, LANES:]], axis=-1)


def _rope_tables(positions):
    b, s = positions.shape
    half_r, half_m = HEAD_DIM // 2, MLA_ROPE // 2
    invf_r = ROPE_BASE ** (-jnp.arange(half_r, dtype=F32) / half_r)
    invf_m = ROPE_BASE ** (-jnp.arange(half_m, dtype=F32) / half_m)
    invf = jnp.concatenate([jnp.tile(invf_r, LANES // half_r),
                            jnp.tile(invf_m, LANES // half_m)])[None, :]
    lane = np.arange(LANES)
    sign = np.concatenate([np.where(lane % HEAD_DIM < half_r, -1.0, 1.0),
                           np.where(lane % MLA_ROPE < half_m, -1.0, 1.0)])
    sign = jnp.asarray(sign, F32)[None, :]
    tm = min(TM_TAB, s)
    return pl.pallas_call(
        _table_kernel,
        out_shape=jax.ShapeDtypeStruct((b, s, 4 * LANES), F32),
        grid=(b, s // tm),
        in_specs=[pl.BlockSpec((None, tm, 1), lambda i, j: (i, j, 0)),
                  pl.BlockSpec((1, 2 * LANES), lambda i, j: (0, 0)),
                  pl.BlockSpec((1, 2 * LANES), lambda i, j: (0, 0))],
        out_specs=pl.BlockSpec((None, tm, 4 * LANES), lambda i, j: (i, j, 0)),
        compiler_params=pltpu.CompilerParams(
            dimension_semantics=("parallel", "parallel")),
        name="rope_tables",
    )(positions[:, :, None], invf, sign)


def _proj_kernel(x_ref, tab_ref, g_ref, w_ref, bf_ref, gq_ref, wq_ref, gkv_ref, wkv_ref,
                 fq_o, fkt_o, fv_o, cum_o, mq_o, mkt_o, mv_o,
                 rq_o, rk_o, rv_o, rg_o, sq_o, skt_o, sv_o, carry_ref, *, mla_scale):
    tm = x_ref.shape[0]
    h = _rms(x_ref[...], g_ref[...]).astype(BF16)

    def proj(a, b):
        return _dot(h, w_ref[:, a:b])

    tab = tab_ref[...]
    cos_r, sin_r = tab[:, 0:LANES], tab[:, LANES:2 * LANES]
    cos_m, sin_m = tab[:, 2 * LANES:3 * LANES], tab[:, 3 * LANES:]
    cos_r2 = jnp.concatenate([cos_r, cos_r], axis=-1)
    sin_r2 = jnp.concatenate([sin_r, sin_r], axis=-1)
    head64 = lax.broadcasted_iota(jnp.int32, (1, GROUP), 1) >> 6

    def store_heads(o_ref, q):
        qb = q.astype(BF16)
        zero = jnp.zeros_like(qb)
        for hh in range(N_HEADS):
            o_ref[hh] = jnp.where(head64 == hh, qb, zero)

    p = proj(C_FOX, C_FF)
    store_heads(fq_o, p[:, 0:GROUP])
    fkt_o[...] = p[:, GROUP:2 * GROUP].T.astype(BF16)
    fv_o[...] = p[:, 2 * GROUP:].astype(BF16)

    @pl.when(pl.program_id(1) == 0)
    def _():
        carry_ref[...] = jnp.zeros_like(carry_ref)

    ls = _log_sigmoid(proj(C_FF, C_CQ) + bf_ref[...])
    row = lax.broadcasted_iota(jnp.int32, (tm, tm), 0)
    col = lax.broadcasted_iota(jnp.int32, (tm, tm), 1)
    tri = jnp.where(col <= row, 1.0, 0.0).astype(F32)
    cum = _dot_f32(tri, ls) + carry_ref[...]
    cum_o[...] = cum
    carry_ref[...] = cum[tm - 1:tm, :]

    hq = _rms(proj(C_CQ, C_CKV), gq_ref[...]).astype(BF16)
    qm = _dot(hq, wq_ref[...]) * mla_scale
    q_nope = qm[:, :GROUP].astype(BF16)
    q_rope = (qm[:, GROUP:GROUP + LANES] * cos_m + qm[:, GROUP + LANES:] * sin_m).astype(BF16)
    lane = lax.broadcasted_iota(jnp.int32, (1, LANES), 1)
    zero = jnp.zeros_like(q_rope)
    for hh in range(N_HEADS):
        pair = hh // 2
        nope = jnp.where((lane >> 6) == hh % 2, q_nope[:, pair * LANES:(pair + 1) * LANES], zero)
        rope = jnp.where((lane >> 5) == hh, q_rope, zero)
        mq_o[hh] = jnp.concatenate([nope, rope], axis=-1)
    hkv = _rms(proj(C_CKV, C_KR), gkv_ref[...]).astype(BF16)
    kv = _dot(hkv, wkv_ref[...])
    pk = proj(C_KR, C_RET)
    k_rope = pk[:, :LANES] * cos_m + pk[:, LANES:] * sin_m
    for pair in range(2):
        kp = jnp.concatenate([kv[:, pair * LANES:(pair + 1) * LANES], k_rope], axis=-1)
        mkt_o[pair * GROUP:(pair + 1) * GROUP, :] = kp.T.astype(BF16)
    mv_o[...] = kv[:, GROUP:].astype(BF16)

    pr = proj(C_RET, C_RVG)
    rq_o[...] = pr[:, 0:GROUP] * cos_r2 + pr[:, GROUP:2 * GROUP] * sin_r2
    rk_o[...] = pr[:, 2 * GROUP:3 * GROUP] * cos_r2 + pr[:, 3 * GROUP:] * sin_r2
    pv = proj(C_RVG, C_SB)
    rv_o[...] = pv[:, :GROUP].astype(BF16)
    rg_o[...] = pv[:, GROUP:]

    ps = proj(C_SB, C_END)
    store_heads(sq_o, ps[:, 0:GROUP])
    skt_o[...] = ps[:, GROUP:2 * GROUP].T.astype(BF16)
    sv_o[...] = ps[:, 2 * GROUP:].astype(BF16)


def _projection(x, tab, g_pre, w_cat, bf_pad, g_q, w_q, g_kv, w_kv):
    b, s, d = x.shape
    tm = min(T_KEY, s)
    nk = s // tm
    row = lambda w: pl.BlockSpec((None, tm, w), lambda i, j: (i, j, 0))
    heads = pl.BlockSpec((None, N_HEADS, tm, GROUP), lambda i, j: (i, 0, j, 0))
    keyt = lambda r: pl.BlockSpec((None, None, r, tm), lambda i, j: (i, j, 0, 0))
    const = lambda shape: pl.BlockSpec(shape, lambda i, j: (0,) * len(shape))
    bf = lambda w: jax.ShapeDtypeStruct((b, s, w), BF16)
    f32 = lambda w: jax.ShapeDtypeStruct((b, s, w), F32)
    q4 = jax.ShapeDtypeStruct((b, N_HEADS, s, GROUP), BF16)
    kt = lambda r: jax.ShapeDtypeStruct((b, nk, r, tm), BF16)
    out_shape = [q4, kt(GROUP), bf(GROUP), f32(LANES),
                 q4, kt(2 * GROUP), bf(GROUP),
                 f32(GROUP), f32(GROUP), bf(GROUP), f32(GROUP),
                 q4, kt(GROUP), bf(GROUP)]
    out_specs = [heads, keyt(GROUP), row(GROUP), row(LANES),
                 heads, keyt(2 * GROUP), row(GROUP),
                 row(GROUP), row(GROUP), row(GROUP), row(GROUP),
                 heads, keyt(GROUP), row(GROUP)]
    return pl.pallas_call(
        functools.partial(_proj_kernel, mla_scale=MLA_QK ** -0.5),
        out_shape=out_shape,
        grid=(b, nk),
        in_specs=[row(d), row(4 * LANES), const((1, d)), const(w_cat.shape),
                  const((1, LANES)), const(g_q.shape), const(w_q.shape),
                  const(g_kv.shape), const(w_kv.shape)],
        out_specs=out_specs,
        scratch_shapes=[pltpu.VMEM((1, LANES), F32)],
        compiler_params=pltpu.CompilerParams(
            dimension_semantics=("parallel", "arbitrary"),
            vmem_limit_bytes=VMEM_LIMIT_BYTES),
        name="in_projection",
    )(x, tab, g_pre, w_cat, bf_pad, g_q, w_q, g_kv, w_kv)


def _softmax_attn_kernel(*refs, fox, groups, tq, tk):
    if fox:
        q_ref, kt_ref, v_ref, cq_ref, ck_ref, gain_ref, o_ref, acc_ref, m_ref, l_ref = refs
    else:
        q_ref, kt_ref, v_ref, gain_ref, o_ref, acc_ref, m_ref, l_ref = refs
    i = pl.program_id(1)
    rows = N_HEADS * tq
    rpg = rows // groups
    q = q_ref[...].reshape(rows, GROUP)
    if fox:
        cqb = cq_ref[...]
        cq = jnp.concatenate([jnp.broadcast_to(cqb[:, h:h + 1], (tq, LANES))
                              for h in range(N_HEADS)], axis=0)

    acc_ref[...] = jnp.zeros_like(acc_ref)
    m_ref[...] = jnp.full_like(m_ref, NEG_BIG)
    l_ref[...] = jnp.zeros_like(l_ref)

    def block(j, masked):
        kt = kt_ref[j]
        s = jnp.concatenate([_dot(q[g * rpg:(g + 1) * rpg], kt[g * GROUP:(g + 1) * GROUP])
                             for g in range(groups)], axis=0)
        if fox:
            ckb = ck_ref[j]
            s = _per_head(lambda h, sh: sh - ckb[h:h + 1, :], s, tq)
        if masked:
            qpos = i * tq + lax.broadcasted_iota(jnp.int32, (tq, tk), 0)
            kpos = j * tk + lax.broadcasted_iota(jnp.int32, (tq, tk), 1)
            if fox:
                visible = kpos <= qpos
            else:
                visible = (kpos >> CHUNK_SHIFT) <= (qpos >> CHUNK_SHIFT)
            s = _per_head(lambda h, sh: jnp.where(visible, sh, NEG_BIG), s, tq)
        m_cur = jnp.broadcast_to(jnp.max(s, axis=-1, keepdims=True), (rows, LANES))
        if fox:
            m_cur = m_cur + cq
        m_prev = m_ref[...]
        m_new = jnp.maximum(m_prev, m_cur)
        alpha = jnp.exp(m_prev - m_new)
        shift = (cq - m_new) if fox else (-m_new)
        l_new = alpha * l_ref[...]
        ps = []
        for c in range(tk // LANES):
            pc = jnp.exp(s[:, c * LANES:(c + 1) * LANES] + shift)
            l_new = l_new + pc
            ps.append(pc.astype(BF16))
        l_ref[...] = l_new
        m_ref[...] = m_new
        start = pl.multiple_of(j * tk, tk)
        pv = _dot(jnp.concatenate(ps, axis=-1), v_ref[pl.ds(start, tk), :])
        acc_ref[...] = acc_ref[...] * jnp.concatenate([alpha, alpha], axis=-1) + pv

    n_full = (i * tq) // tk

    def body(j, carry):
        block(j, False)
        return carry

    lax.fori_loop(0, n_full, body, 0)
    block(n_full, True)

    inv_l = 1.0 / jnp.sum(l_ref[...], axis=-1, keepdims=True)
    o_all = acc_ref[...] * inv_l
    head_v = lax.broadcasted_iota(jnp.int32, (1, GROUP), 1) >> 6
    o = _head_select(head_v, [o_all[h * tq:(h + 1) * tq] for h in range(N_HEADS)])
    o_ref[...] = _rms(o, gain_ref[...]).astype(o_ref.dtype)


def _softmax_attention(q, kt, v, gain, cum=None):
    b, _, s, _ = q.shape
    nk, krows, tk = kt.shape[1:]
    tq = min(TQ_ATTN, s)
    fox = cum is not None
    in_specs = [pl.BlockSpec((None, N_HEADS, tq, GROUP), lambda i, j: (i, 0, j, 0)),
                pl.BlockSpec((None, nk, krows, tk), lambda i, j: (i, 0, 0, 0)),
                pl.BlockSpec((None, s, GROUP), lambda i, j: (i, 0, 0))]
    args = [q, kt, v]
    if fox:
        ck = cum[:, :, :8].reshape(b, nk, tk, 8).transpose(0, 1, 3, 2)
        in_specs += [pl.BlockSpec((None, tq, LANES), lambda i, j: (i, j, 0)),
                     pl.BlockSpec((None, nk, 8, tk), lambda i, j: (i, 0, 0, 0))]
        args += [cum, ck]
    in_specs.append(pl.BlockSpec((1, GROUP), lambda i, j: (0, 0)))
    args.append(gain)
    rows = N_HEADS * tq
    return pl.pallas_call(
        functools.partial(_softmax_attn_kernel, fox=fox, groups=krows // GROUP, tq=tq, tk=tk),
        out_shape=jax.ShapeDtypeStruct((b, s, GROUP), BF16),
        grid=(b, s // tq),
        in_specs=in_specs,
        out_specs=pl.BlockSpec((None, tq, GROUP), lambda i, j: (i, j, 0)),
        scratch_shapes=[pltpu.VMEM((rows, GROUP), F32),
                        pltpu.VMEM((rows, LANES), F32),
                        pltpu.VMEM((rows, LANES), F32)],
        compiler_params=pltpu.CompilerParams(
            dimension_semantics=("parallel", "parallel"),
            vmem_limit_bytes=VMEM_LIMIT_BYTES),
        name="fox_attention" if fox else "mla_attention",
    )(*args)


def _stick_kernel(q_ref, kt_ref, v_ref, gain_ref, o_ref, acc_ref, r_ref, *, tq, tk, tc):
    i = pl.program_id(1)
    rows = N_HEADS * tq
    q = q_ref[...].reshape(rows, GROUP)
    rj = lax.broadcasted_iota(jnp.int32, (tc, tc), 0)
    cs = lax.broadcasted_iota(jnp.int32, (tc, tc), 1)
    upper = jnp.where(rj > cs, 1.0, 0.0).astype(BF16)
    upper2 = jnp.concatenate([upper, upper], axis=0)

    acc_ref[...] = jnp.zeros_like(acc_ref)
    r_ref[...] = jnp.zeros_like(r_ref)

    def block(j, masked):
        z_all = _dot(q, kt_ref[j])
        for c in reversed(range(tk // tc)):
            z = z_all[:, c * tc:(c + 1) * tc]
            leave = jnp.maximum(z, 0.0) + jnp.log(1.0 + jnp.exp(-jnp.abs(z)))
            if masked:
                qpos = i * tq + lax.broadcasted_iota(jnp.int32, (tq, tc), 0)
                kpos = j * tk + c * tc + lax.broadcasted_iota(jnp.int32, (tq, tc), 1)
                visible = kpos < qpos
                leave = _per_head(lambda h, x: jnp.where(visible, x, 0.0), leave, tq)
            hi = leave.astype(BF16)
            lo = (leave - hi.astype(F32)).astype(BF16)
            r_prev = r_ref[...]
            later = (_dot(jnp.concatenate([hi, lo], axis=-1), upper2)
                     + jnp.concatenate([r_prev] * (tc // LANES), axis=-1))
            w = jnp.exp(z - leave - later)
            if masked:
                w = _per_head(lambda h, x: jnp.where(visible, x, 0.0), w, tq)
            r_ref[...] = r_prev + jnp.broadcast_to(
                jnp.sum(leave, axis=-1, keepdims=True), (rows, LANES))
            start = pl.multiple_of(j * tk + c * tc, tc)
            acc_ref[...] += _dot(w.astype(BF16), v_ref[pl.ds(start, tc), :])

    n_full = (i * tq) // tk
    block(n_full, True)

    def body(t, carry):
        block(n_full - 1 - t, False)
        return carry

    lax.fori_loop(0, n_full, body, 0)
    acc = acc_ref[...]
    head_v = lax.broadcasted_iota(jnp.int32, (1, GROUP), 1) >> 6
    o = _head_select(head_v, [acc[h * tq:(h + 1) * tq] for h in range(N_HEADS)])
    o_ref[...] = _rms(o, gain_ref[...]).astype(o_ref.dtype)


def _stick_attention(q, kt, v, gain):
    b, _, s, _ = q.shape
    nk, krows, tk = kt.shape[1:]
    tq = min(TQ_ATTN, s)
    rows = N_HEADS * tq
    return pl.pallas_call(
        functools.partial(_stick_kernel, tq=tq, tk=tk, tc=min(TC_STICK, tk)),
        out_shape=jax.ShapeDtypeStruct((b, s, GROUP), BF16),
        grid=(b, s // tq),
        in_specs=[pl.BlockSpec((None, N_HEADS, tq, GROUP), lambda i, j: (i, 0, j, 0)),
                  pl.BlockSpec((None, nk, krows, tk), lambda i, j: (i, 0, 0, 0)),
                  pl.BlockSpec((None, s, GROUP), lambda i, j: (i, 0, 0)),
                  pl.BlockSpec((1, GROUP), lambda i, j: (0, 0))],
        out_specs=pl.BlockSpec((None, tq, GROUP), lambda i, j: (i, j, 0)),
        scratch_shapes=[pltpu.VMEM((rows, GROUP), F32),
                        pltpu.VMEM((rows, LANES), F32)],
        compiler_params=pltpu.CompilerParams(
            dimension_semantics=("parallel", "parallel"),
            vmem_limit_bytes=VMEM_LIMIT_BYTES),
        name="stick_attention",
    )(q, kt, v, gain)


def _retention_kernel(q_ref, k_ref, v_ref, g_ref, lg_ref, gain_ref, o_ref, state_ref, *, t):
    @pl.when(pl.program_id(1) == 0)
    def _():
        state_ref[...] = jnp.zeros_like(state_ref)

    q = q_ref[...]
    k = k_ref[...]
    v = v_ref[...]
    lg = lg_ref[...]
    head = lax.broadcasted_iota(jnp.int32, (1, GROUP), 1) >> 6
    head_r = lax.broadcasted_iota(jnp.int32, (GROUP, 1), 0) >> 6
    same_head = head_r == head
    ri = lax.broadcasted_iota(jnp.int32, (t, t), 0)
    ci = lax.broadcasted_iota(jnp.int32, (t, t), 1)
    dist = jnp.abs(ri - ci).astype(F32)
    reach = (ci >> CHUNK_SHIFT) <= (ri >> CHUNK_SHIFT)
    pos = lax.broadcasted_iota(jnp.int32, (t, 1), 0).astype(F32)

    q_b = q.astype(BF16)
    k_b = k.astype(BF16)
    zq = jnp.zeros_like(q_b)
    zv = jnp.zeros_like(v)
    out = None
    for h in range(N_HEADS):
        lg_h = lg[:, h * HEAD_DIM:h * HEAD_DIM + 1]
        s = _dot_nt(jnp.where(head == h, q_b, zq), k_b)
        w = s * jnp.where(reach, jnp.exp(lg_h * dist), 0.0)
        d = _dot(w.astype(BF16), jnp.where(head == h, v, zv))
        out = d if out is None else out + d

    state = state_ref[...]
    q_head = (q * jnp.exp(lg * (pos + 1.0))).astype(BF16)
    out = out + _dot(q_head, state.astype(BF16))

    k_tail = k * jnp.exp(lg * (float(t - 1) - pos))
    kv = _dot(k_tail.T.astype(BF16), v)
    state_ref[...] = state * jnp.exp(lg * float(t)) + jnp.where(same_head, kv, 0.0)

    avg = jnp.where(same_head, 1.0 / HEAD_DIM, 0.0).astype(F32)
    mu = _dot_f32(out, avg)
    cen = out - mu
    var = _dot_f32(cen * cen, avg)
    y = cen * lax.rsqrt(var + EPS) * gain_ref[...]
    g = g_ref[...]
    o_ref[...] = (y * (g * (1.0 / (1.0 + jnp.exp(-g))))).astype(o_ref.dtype)


def _retention(q, k, v, g, log_gamma, gain):
    b, s, _ = q.shape
    t = min(T_RET, s)
    row = pl.BlockSpec((None, t, GROUP), lambda i, j: (i, j, 0))
    const = pl.BlockSpec((1, GROUP), lambda i, j: (0, 0))
    return pl.pallas_call(
        functools.partial(_retention_kernel, t=t),
        out_shape=jax.ShapeDtypeStruct((b, s, GROUP), BF16),
        grid=(b, s // t),
        in_specs=[row, row, row, row, const, const],
        out_specs=row,
        scratch_shapes=[pltpu.VMEM((GROUP, GROUP), F32)],
        compiler_params=pltpu.CompilerParams(
            dimension_semantics=("parallel", "arbitrary"),
            vmem_limit_bytes=VMEM_LIMIT_BYTES),
        name="retention",
    )(q, k, v, g, log_gamma, gain)


def _post_kernel(x_ref, a_ref, b_ref, c_ref, d_ref, wo_ref, gpost_ref, gpre_ref,
                 wup_ref, wdn_ref, gfpost_ref, o_ref, *, f_chunk):
    mixed = jnp.concatenate([a_ref[...], b_ref[...], c_ref[...], d_ref[...]], axis=-1)
    x1 = x_ref[...] + _rms(_dot(mixed, wo_ref[...]), gpost_ref[...])
    h = _rms(x1, gpre_ref[...]).astype(BF16)
    y = None
    for c in range(wup_ref.shape[1] // f_chunk):
        u = jnp.maximum(_dot(h, wup_ref[:, c * f_chunk:(c + 1) * f_chunk]), 0.0)
        d = _dot((u * u).astype(BF16), wdn_ref[c * f_chunk:(c + 1) * f_chunk, :])
        y = d if y is None else y + d
    o_ref[...] = x1 + _rms(y, gfpost_ref[...])


def _post(x, a, bb, c, dd, w_out, g_post, g_pre, w_up, w_dn, g_fpost):
    b, s, d = x.shape
    tm = min(TM_POST, s)
    row = lambda w: pl.BlockSpec((None, tm, w), lambda i, j: (i, j, 0))
    const = lambda shape: pl.BlockSpec(shape, lambda i, j: (0,) * len(shape))
    return pl.pallas_call(
        functools.partial(_post_kernel, f_chunk=1024),
        out_shape=jax.ShapeDtypeStruct((b, s, d), F32),
        grid=(b, s // tm),
        in_specs=[row(d), row(GROUP), row(GROUP), row(GROUP), row(GROUP),
                  const(w_out.shape), const((1, d)), const((1, d)),
                  const(w_up.shape), const(w_dn.shape), const((1, d))],
        out_specs=row(d),
        compiler_params=pltpu.CompilerParams(
            dimension_semantics=("parallel", "parallel"),
            vmem_limit_bytes=VMEM_LIMIT_BYTES),
        name="out_proj_mlp",
    )(x, a, bb, c, dd, w_out, g_post, g_pre, w_up, w_dn, g_fpost)


def _rot_perm(width, period):
    j = np.arange(width)
    return (j // period) * period + (j % period + period // 2) % period


def _layer_weights(w_in, b_forget, w_q_up, w_kv_up):
    sizes = [GROUP, GROUP, GROUP, N_HEADS, 256, 128, MLA_ROPE] + [GROUP] * 7
    offs = np.concatenate([[0], np.cumsum(sizes)])
    (fq, fk, fv, ff, cq, ckv, kr, rq, rk, rv, rg, sq, sk, sv) = [
        w_in[:, int(offs[n]):int(offs[n + 1])] for n in range(len(sizes))]
    d = w_in.shape[0]
    scale = HEAD_DIM ** -0.5
    ret_perm = _rot_perm(GROUP, HEAD_DIM)
    kr_perm = _rot_perm(MLA_ROPE, MLA_ROPE)
    ff_pad = jnp.concatenate([ff, jnp.zeros((d, LANES - N_HEADS), F32)], axis=1)
    w_cat = jnp.concatenate(
        [fq * scale, fk, fv, ff_pad, cq, ckv,
         jnp.tile(kr, (1, N_HEADS)), jnp.tile(kr[:, kr_perm], (1, N_HEADS)),
         rq, rq[:, ret_perm], rk * scale, rk[:, ret_perm] * scale, rv, rg,
         sq * scale, sk, sv], axis=1).astype(BF16)
    bf_pad = jnp.concatenate([b_forget, jnp.zeros((LANES - N_HEADS,), F32)])[None, :]
    wq3 = w_q_up.reshape(w_q_up.shape[0], N_HEADS, MLA_QK)
    q_nope = wq3[:, :, :HEAD_DIM].reshape(-1, GROUP)
    q_rope = wq3[:, :, HEAD_DIM:].reshape(-1, N_HEADS * MLA_ROPE)
    w_q = jnp.concatenate(
        [q_nope, q_rope, q_rope[:, _rot_perm(N_HEADS * MLA_ROPE, MLA_ROPE)]], axis=1).astype(BF16)
    wkv3 = w_kv_up.reshape(w_kv_up.shape[0], N_HEADS, 2 * HEAD_DIM)
    w_kv = jnp.concatenate([wkv3[:, :, :HEAD_DIM].reshape(-1, GROUP),
                            wkv3[:, :, HEAD_DIM:].reshape(-1, GROUP)], axis=1).astype(BF16)
    return w_cat, bf_pad, w_q, w_kv


def kernel(x, positions, g_mix_pre, w_in, b_forget, g_q_lora, w_q_up, g_kv_lora, w_kv_up,
           g_mix_out, w_out, g_mix_post, g_ffn_pre, w_ffn_up, w_ffn_down, g_ffn_post):
    depth = w_in.shape[0]
    tab = _rope_tables(positions)
    log_gamma = jnp.log1p(-jnp.power(2.0, -RET_DECAY_OFFSET - jnp.arange(N_HEADS, dtype=F32)))
    log_gamma = jnp.repeat(log_gamma, HEAD_DIM)[None, :]
    for layer in range(depth):
        w_cat, bf_pad, w_q, w_kv = _layer_weights(
            w_in[layer], b_forget[layer], w_q_up[layer], w_kv_up[layer])
        (fq, fkt, fv, cum, mq, mkt, mv, rq, rk, rv, rg, sq, skt, sv) = _projection(
            x, tab, g_mix_pre[layer][None, :], w_cat, bf_pad,
            g_q_lora[layer][None, :], w_q, g_kv_lora[layer][None, :], w_kv)
        gmo = g_mix_out[layer]
        out_a = _softmax_attention(fq, fkt, fv, gmo[None, 0:GROUP], cum=cum)
        out_b = _softmax_attention(mq, mkt, mv, gmo[None, GROUP:2 * GROUP])
        out_c = _retention(rq, rk, rv, rg, log_gamma, gmo[None, 2 * GROUP:3 * GROUP])
        out_d = _stick_attention(sq, skt, sv, gmo[None, 3 * GROUP:])
        x = _post(x, out_a, out_b, out_c, out_d, w_out[layer].astype(BF16),
                  g_mix_post[layer][None, :], g_ffn_pre[layer][None, :],
                  w_ffn_up[layer].astype(BF16), w_ffn_down[layer].astype(BF16),
                  g_ffn_post[layer][None, :])
    return x
```

```python
import functools

import numpy as np
import jax
import jax.numpy as jnp
from jax import lax
from jax.experimental import pallas as pl
from jax.experimental.pallas import tpu as pltpu

F32 = jnp.float32
BF16 = jnp.bfloat16

N_HEADS = 4
HEAD_DIM = 64
GROUP = N_HEADS * HEAD_DIM
CHUNK = 64
CHUNK_SHIFT = 6
MLA_ROPE = 32
MLA_QK = HEAD_DIM + MLA_ROPE
ROPE_BASE = 10000.0
EPS = 1e-6
RET_DECAY_OFFSET = 5.0
NEG_BIG = -1e30
LOG2E = 1.4426950408889634
STICK_EXIT = 128.0

LANES = 128
VMEM_LIMIT_BYTES = 56 * 1024 * 1024

T_KEY = 512
TM_POST = 512
TQ_ATTN = 512
HPC_FOX = 1
HPC_MLA = 4
T_STICK = 256
T_RET = 256
TM_TAB = 1024

C_FOX = 0
C_FF = 768
C_CQ = 896
C_CKV = 1152
C_KR = 1280
C_RET = 1536
C_RVG = 2560
C_SB = 3072
C_END = 3840


def _rms(x, g):
    return x * lax.rsqrt(jnp.mean(x * x, axis=-1, keepdims=True) + EPS) * g


def _log_sigmoid(x):
    return jnp.minimum(x, 0.0) - jnp.log1p(jnp.exp(-jnp.abs(x)))


def _dot(a, b):
    return jnp.dot(a, b, preferred_element_type=F32)


def _dot_nt(a, b):
    return lax.dot_general(a, b, (((1,), (1,)), ((), ())), preferred_element_type=F32)


def _dot_f32(a, b):
    return jnp.dot(a, b, preferred_element_type=F32, precision=lax.Precision.HIGHEST)


def _head_select(lane_head, per_head):
    out = per_head[N_HEADS - 1]
    for h in range(N_HEADS - 2, -1, -1):
        out = jnp.where(lane_head == h, per_head[h], out)
    return out


def _per_head(fn, x, tq):
    return jnp.concatenate([fn(h, x[h * tq:(h + 1) * tq]) for h in range(N_HEADS)], axis=0)


def _table_kernel(pos_ref, invf_ref, sign_ref, tab_ref):
    ang = pos_ref[...].astype(F32) * invf_ref[...]
    c = jnp.cos(ang)
    s = jnp.sin(ang) * sign_ref[...]
    tab_ref[...] = jnp.concatenate(
        [c[:, :LANES], s[:, :LANES], c[:, LANES:], s[:, LANES:]], axis=-1)


def _rope_tables(positions):
    b, s = positions.shape
    half_r, half_m = HEAD_DIM // 2, MLA_ROPE // 2
    invf_r = ROPE_BASE ** (-jnp.arange(half_r, dtype=F32) / half_r)
    invf_m = ROPE_BASE ** (-jnp.arange(half_m, dtype=F32) / half_m)
    invf = jnp.concatenate([jnp.tile(invf_r, LANES // half_r),
                            jnp.tile(invf_m, LANES // half_m)])[None, :]
    lane = np.arange(LANES)
    sign = np.concatenate([np.where(lane % HEAD_DIM < half_r, -1.0, 1.0),
                           np.where(lane % MLA_ROPE < half_m, -1.0, 1.0)])
    sign = jnp.asarray(sign, F32)[None, :]
    tm = min(TM_TAB, s)
    return pl.pallas_call(
        _table_kernel,
        out_shape=jax.ShapeDtypeStruct((b, s, 4 * LANES), F32),
        grid=(b, s // tm),
        in_specs=[pl.BlockSpec((None, tm, 1), lambda i, j: (i, j, 0)),
                  pl.BlockSpec((1, 2 * LANES), lambda i, j: (0, 0)),
                  pl.BlockSpec((1, 2 * LANES), lambda i, j: (0, 0))],
        out_specs=pl.BlockSpec((None, tm, 4 * LANES), lambda i, j: (i, j, 0)),
        compiler_params=pltpu.CompilerParams(
            dimension_semantics=("parallel", "parallel")),
        name="rope_tables",
    )(positions[:, :, None], invf, sign)


def _proj_kernel(x_ref, tab_ref, g_ref, w_ref, bf_ref, gq_ref, wq_ref, gkv_ref, wkv_ref,
                 fq_o, fkt_o, fv_o, cum_o, mq_o, mkt_o, mv_o,
                 rq_o, rk_o, rv_o, rg_o, sq_o, skt_o, sv_o, carry_ref, *, mla_scale):
    tm = x_ref.shape[0]
    h = _rms(x_ref[...], g_ref[...]).astype(BF16)

    def proj(a, b):
        return _dot(h, w_ref[:, a:b])

    tab = tab_ref[...]
    cos_r, sin_r = tab[:, 0:LANES], tab[:, LANES:2 * LANES]
    cos_m, sin_m = tab[:, 2 * LANES:3 * LANES], tab[:, 3 * LANES:]
    cos_r2 = jnp.concatenate([cos_r, cos_r], axis=-1)
    sin_r2 = jnp.concatenate([sin_r, sin_r], axis=-1)
    head64 = lax.broadcasted_iota(jnp.int32, (1, GROUP), 1) >> 6

    def store_heads(o_ref, q):
        qb = q.astype(BF16)
        zero = jnp.zeros_like(qb)
        for hh in range(N_HEADS):
            o_ref[hh] = jnp.where(head64 == hh, qb, zero)

    p = proj(C_FOX, C_FF)
    store_heads(fq_o, p[:, 0:GROUP] * LOG2E)
    fkt_o[...] = p[:, GROUP:2 * GROUP].T.astype(BF16)
    fv_o[...] = p[:, 2 * GROUP:].astype(BF16)

    @pl.when(pl.program_id(1) == 0)
    def _():
        carry_ref[...] = jnp.zeros_like(carry_ref)

    ls = _log_sigmoid(proj(C_FF, C_CQ) + bf_ref[...])
    row = lax.broadcasted_iota(jnp.int32, (tm, tm), 0)
    col = lax.broadcasted_iota(jnp.int32, (tm, tm), 1)
    tri = jnp.where(col <= row, 1.0, 0.0).astype(F32)
    cum = _dot_f32(tri, ls) + carry_ref[...]
    cum_o[...] = cum * LOG2E
    carry_ref[...] = cum[tm - 1:tm, :]

    hq = _rms(proj(C_CQ, C_CKV), gq_ref[...]).astype(BF16)
    qm = _dot(hq, wq_ref[...]) * mla_scale
    q_nope = qm[:, :GROUP].astype(BF16)
    q_rope = (qm[:, GROUP:GROUP + LANES] * cos_m + qm[:, GROUP + LANES:] * sin_m).astype(BF16)
    lane = lax.broadcasted_iota(jnp.int32, (1, LANES), 1)
    zero = jnp.zeros_like(q_rope)
    for hh in range(N_HEADS):
        pair = hh // 2
        nope = jnp.where((lane >> 6) == hh % 2, q_nope[:, pair * LANES:(pair + 1) * LANES], zero)
        rope = jnp.where((lane >> 5) == hh, q_rope, zero)
        mq_o[hh] = jnp.concatenate([nope, rope], axis=-1)
    hkv = _rms(proj(C_CKV, C_KR), gkv_ref[...]).astype(BF16)
    kv = _dot(hkv, wkv_ref[...])
    pk = proj(C_KR, C_RET)
    k_rope = pk[:, :LANES] * cos_m + pk[:, LANES:] * sin_m
    for pair in range(2):
        kp = jnp.concatenate([kv[:, pair * LANES:(pair + 1) * LANES], k_rope], axis=-1)
        mkt_o[pair * GROUP:(pair + 1) * GROUP, :] = kp.T.astype(BF16)
    mv_o[...] = kv[:, GROUP:].astype(BF16)

    pr = proj(C_RET, C_RVG)
    rq_o[...] = pr[:, 0:GROUP] * cos_r2 + pr[:, GROUP:2 * GROUP] * sin_r2
    rk_o[...] = pr[:, 2 * GROUP:3 * GROUP] * cos_r2 + pr[:, 3 * GROUP:] * sin_r2
    pv = proj(C_RVG, C_SB)
    rv_o[...] = pv[:, :GROUP].astype(BF16)
    rg_o[...] = pv[:, GROUP:]

    ps = proj(C_SB, C_END)
    store_heads(sq_o, ps[:, 0:GROUP])
    skt = ps[:, GROUP:2 * GROUP].T.astype(BF16)
    for c in range(tm // T_STICK):
        skt_o[c] = skt[:, c * T_STICK:(c + 1) * T_STICK]
    sv_o[...] = ps[:, 2 * GROUP:].astype(BF16)


def _projection(x, tab, g_pre, w_cat, bf_pad, g_q, w_q, g_kv, w_kv):
    b, s, d = x.shape
    tm = min(T_KEY, s)
    nk = s // tm
    row = lambda w: pl.BlockSpec((None, tm, w), lambda i, j: (i, j, 0))
    heads = pl.BlockSpec((None, N_HEADS, tm, GROUP), lambda i, j: (i, 0, j, 0))
    keyt = lambda r: pl.BlockSpec((None, None, r, tm), lambda i, j: (i, j, 0, 0))
    const = lambda shape: pl.BlockSpec(shape, lambda i, j: (0,) * len(shape))
    bf = lambda w: jax.ShapeDtypeStruct((b, s, w), BF16)
    f32 = lambda w: jax.ShapeDtypeStruct((b, s, w), F32)
    q4 = jax.ShapeDtypeStruct((b, N_HEADS, s, GROUP), BF16)
    kt = lambda r: jax.ShapeDtypeStruct((b, nk, r, tm), BF16)
    n_st = tm // T_STICK
    skt_shape = jax.ShapeDtypeStruct((b, nk * n_st, GROUP, T_STICK), BF16)
    skt_spec = pl.BlockSpec((None, n_st, GROUP, T_STICK), lambda i, j: (i, j, 0, 0))
    out_shape = [q4, kt(GROUP), bf(GROUP), f32(LANES),
                 q4, kt(2 * GROUP), bf(GROUP),
                 f32(GROUP), f32(GROUP), bf(GROUP), f32(GROUP),
                 q4, skt_shape, bf(GROUP)]
    out_specs = [heads, keyt(GROUP), row(GROUP), row(LANES),
                 heads, keyt(2 * GROUP), row(GROUP),
                 row(GROUP), row(GROUP), row(GROUP), row(GROUP),
                 heads, skt_spec, row(GROUP)]
    return pl.pallas_call(
        functools.partial(_proj_kernel, mla_scale=MLA_QK ** -0.5 * LOG2E),
        out_shape=out_shape,
        grid=(b, nk),
        in_specs=[row(d), row(4 * LANES), const((1, d)), const(w_cat.shape),
                  const((1, LANES)), const(g_q.shape), const(w_q.shape),
                  const(g_kv.shape), const(w_kv.shape)],
        out_specs=out_specs,
        scratch_shapes=[pltpu.VMEM((1, LANES), F32)],
        compiler_params=pltpu.CompilerParams(
            dimension_semantics=("parallel", "arbitrary"),
            vmem_limit_bytes=VMEM_LIMIT_BYTES),
        name="in_projection",
    )(x, tab, g_pre, w_cat, bf_pad, g_q, w_q, g_kv, w_kv)


def _softmax_attn_kernel(*refs, fox, groups, hpc, tq, tk):
    if fox:
        q_ref, kt_ref, v_ref, cq_ref, ck_ref, gain_ref, o_ref, acc_ref, m_ref, l_ref = refs
    else:
        q_ref, kt_ref, v_ref, gain_ref, o_ref, acc_ref, m_ref, l_ref = refs
    i = pl.program_id(1)
    rows = N_HEADS * tq
    hpg = N_HEADS // groups
    if fox:
        cqb = cq_ref[...]
        cq = [jnp.broadcast_to(cqb[:, h:h + 1], (tq, LANES)) for h in range(N_HEADS)]

    acc_ref[...] = jnp.zeros_like(acc_ref)
    m_ref[...] = jnp.full_like(m_ref, NEG_BIG)
    l_ref[...] = jnp.zeros_like(l_ref)

    def block(j, masked):
        kt = kt_ref[j]
        if fox:
            ckb = ck_ref[j]
        if masked:
            qpos = i * tq + lax.broadcasted_iota(jnp.int32, (tq, tk), 0)
            kpos = j * tk + lax.broadcasted_iota(jnp.int32, (tq, tk), 1)
            if fox:
                visible = kpos <= qpos
            else:
                visible = (kpos >> CHUNK_SHIFT) <= (qpos >> CHUNK_SHIFT)
        start = pl.multiple_of(j * tk, tk)
        vb = v_ref[pl.ds(start, tk), :]
        for ch in range(N_HEADS // hpc):
            cr = slice(ch * hpc * tq, (ch + 1) * hpc * tq)
            n_dot = min(hpc, hpg)
            s_parts = []
            for h0 in range(ch * hpc, (ch + 1) * hpc, n_dot):
                g = h0 // hpg
                s_parts.append(_dot(q_ref[h0:h0 + n_dot].reshape(n_dot * tq, GROUP),
                                    kt[g * GROUP:(g + 1) * GROUP]))
            p_rows, a_rows = [], []
            for hl in range(hpc):
                h = ch * hpc + hl
                hr = slice(h * tq, (h + 1) * tq)
                s = s_parts[hl // n_dot][(hl % n_dot) * tq:(hl % n_dot + 1) * tq]
                if fox:
                    s = s - ckb[h:h + 1, :]
                if masked:
                    s = jnp.where(visible, s, NEG_BIG)
                m_cur = jnp.broadcast_to(jnp.max(s, axis=-1, keepdims=True), (tq, LANES))
                if fox:
                    m_cur = m_cur + cq[h]
                m_prev = m_ref[hr]
                m_new = jnp.maximum(m_prev, m_cur)
                alpha = jnp.exp2(m_prev - m_new)
                shift = (cq[h] - m_new) if fox else (-m_new)
                l_new = alpha * l_ref[hr]
                ps = []
                for c in range(tk // LANES):
                    pc = jnp.exp2(s[:, c * LANES:(c + 1) * LANES] + shift)
                    l_new = l_new + pc
                    ps.append(pc.astype(BF16))
                l_ref[hr] = l_new
                m_ref[hr] = m_new
                p_rows.append(jnp.concatenate(ps, axis=-1))
                a_rows.append(jnp.concatenate([alpha, alpha], axis=-1))
            pv = _dot(jnp.concatenate(p_rows, axis=0), vb)
            acc_ref[cr] = acc_ref[cr] * jnp.concatenate(a_rows, axis=0) + pv

    n_full = (i * tq) // tk

    def body(j, carry):
        block(j, False)
        return carry

    lax.fori_loop(0, n_full, body, 0)
    block(n_full, True)

    inv_l = 1.0 / jnp.sum(l_ref[...], axis=-1, keepdims=True)
    o_all = acc_ref[...] * inv_l
    head_v = lax.broadcasted_iota(jnp.int32, (1, GROUP), 1) >> 6
    o = _head_select(head_v, [o_all[h * tq:(h + 1) * tq] for h in range(N_HEADS)])
    o_ref[...] = _rms(o, gain_ref[...]).astype(o_ref.dtype)


def _softmax_attention(q, kt, v, gain, cum=None):
    b, _, s, _ = q.shape
    nk, krows, tk = kt.shape[1:]
    tq = min(TQ_ATTN, s)
    fox = cum is not None
    in_specs = [pl.BlockSpec((None, N_HEADS, tq, GROUP), lambda i, j: (i, 0, j, 0)),
                pl.BlockSpec((None, nk, krows, tk), lambda i, j: (i, 0, 0, 0)),
                pl.BlockSpec((None, s, GROUP), lambda i, j: (i, 0, 0))]
    args = [q, kt, v]
    if fox:
        ck = cum[:, :, :8].reshape(b, nk, tk, 8).transpose(0, 1, 3, 2)
        in_specs += [pl.BlockSpec((None, tq, LANES), lambda i, j: (i, j, 0)),
                     pl.BlockSpec((None, nk, 8, tk), lambda i, j: (i, 0, 0, 0))]
        args += [cum, ck]
    in_specs.append(pl.BlockSpec((1, GROUP), lambda i, j: (0, 0)))
    args.append(gain)
    rows = N_HEADS * tq
    return pl.pallas_call(
        functools.partial(_softmax_attn_kernel, fox=fox, groups=krows // GROUP,
                          hpc=HPC_FOX if fox else HPC_MLA, tq=tq, tk=tk),
        out_shape=jax.ShapeDtypeStruct((b, s, GROUP), BF16),
        grid=(b, s // tq),
        in_specs=in_specs,
        out_specs=pl.BlockSpec((None, tq, GROUP), lambda i, j: (i, j, 0)),
        scratch_shapes=[pltpu.VMEM((rows, GROUP), F32),
                        pltpu.VMEM((rows, LANES), F32),
                        pltpu.VMEM((rows, LANES), F32)],
        compiler_params=pltpu.CompilerParams(
            dimension_semantics=("parallel", "parallel"),
            vmem_limit_bytes=VMEM_LIMIT_BYTES),
        name="fox_attention" if fox else "mla_attention",
    )(*args)


def _stick_kernel(q_ref, kt_ref, v_ref, gain_ref, o_ref, acc_ref, r_ref, *, t):
    i = pl.program_id(1)
    rows = N_HEADS * t
    q = q_ref[...].reshape(rows, GROUP)
    rj = lax.broadcasted_iota(jnp.int32, (t, t), 0)
    cs = lax.broadcasted_iota(jnp.int32, (t, t), 1)
    upper = jnp.where(rj > cs, 1.0, 0.0).astype(BF16)
    upper2 = jnp.concatenate([upper, upper], axis=0)
    visible = cs < rj

    def block(j, diagonal):
        z = _dot(q, kt_ref[j])
        leave = jnp.maximum(z, 0.0) + jnp.log(1.0 + jnp.exp(-jnp.abs(z)))
        if diagonal:
            leave = _per_head(lambda h, x: jnp.where(visible, x, 0.0), leave, t)
        hi = leave.astype(BF16)
        lo = (leave - hi.astype(F32)).astype(BF16)
        later = _dot(jnp.concatenate([hi, lo], axis=-1), upper2)
        row_sum = jnp.broadcast_to(jnp.sum(leave, axis=-1, keepdims=True), (rows, LANES))
        if diagonal:
            r_new = row_sum
        else:
            r_prev = r_ref[...]
            later = later + jnp.concatenate([r_prev] * (t // LANES), axis=-1)
            r_new = r_prev + row_sum
        w = jnp.exp(z - leave - later)
        if diagonal:
            w = _per_head(lambda h, x: jnp.where(visible, x, 0.0), w, t)
        r_ref[...] = r_new
        start = pl.multiple_of(j * t, t)
        pv = _dot(w.astype(BF16), v_ref[pl.ds(start, t), :])
        if diagonal:
            acc_ref[...] = pv
        else:
            acc_ref[...] += pv
        return jnp.min(r_new)

    r_min = block(i, True)

    def cond(carry):
        j, r_min = carry
        return jnp.logical_and(j >= 0, r_min < STICK_EXIT)

    def body(carry):
        j, _ = carry
        return j - 1, block(j, False)

    lax.while_loop(cond, body, (i - 1, r_min))
    acc = acc_ref[...]
    head_v = lax.broadcasted_iota(jnp.int32, (1, GROUP), 1) >> 6
    o = _head_select(head_v, [acc[h * t:(h + 1) * t] for h in range(N_HEADS)])
    o_ref[...] = _rms(o, gain_ref[...]).astype(o_ref.dtype)


def _stick_attention(q, kt, v, gain):
    b, _, s, _ = q.shape
    nk, krows, tk = kt.shape[1:]
    tq = tk
    rows = N_HEADS * tq
    return pl.pallas_call(
        functools.partial(_stick_kernel, t=tk),
        out_shape=jax.ShapeDtypeStruct((b, s, GROUP), BF16),
        grid=(b, s // tq),
        in_specs=[pl.BlockSpec((None, N_HEADS, tq, GROUP), lambda i, j: (i, 0, j, 0)),
                  pl.BlockSpec((None, nk, krows, tk), lambda i, j: (i, 0, 0, 0)),
                  pl.BlockSpec((None, s, GROUP), lambda i, j: (i, 0, 0)),
                  pl.BlockSpec((1, GROUP), lambda i, j: (0, 0))],
        out_specs=pl.BlockSpec((None, tq, GROUP), lambda i, j: (i, j, 0)),
        scratch_shapes=[pltpu.VMEM((rows, GROUP), F32),
                        pltpu.VMEM((rows, LANES), F32)],
        compiler_params=pltpu.CompilerParams(
            dimension_semantics=("parallel", "parallel"),
            vmem_limit_bytes=VMEM_LIMIT_BYTES),
        name="stick_attention",
    )(q, kt, v, gain)


def _retention_kernel(q_ref, k_ref, v_ref, g_ref, lg_ref, gain_ref, o_ref, state_ref, *, t):
    @pl.when(pl.program_id(1) == 0)
    def _():
        state_ref[...] = jnp.zeros_like(state_ref)

    q = q_ref[...]
    k = k_ref[...]
    v = v_ref[...]
    lg = lg_ref[...]
    head = lax.broadcasted_iota(jnp.int32, (1, GROUP), 1) >> 6
    head_r = lax.broadcasted_iota(jnp.int32, (GROUP, 1), 0) >> 6
    same_head = head_r == head
    ri = lax.broadcasted_iota(jnp.int32, (t, t), 0)
    ci = lax.broadcasted_iota(jnp.int32, (t, t), 1)
    dist = jnp.abs(ri - ci).astype(F32)
    reach = (ci >> CHUNK_SHIFT) <= (ri >> CHUNK_SHIFT)
    pos = lax.broadcasted_iota(jnp.int32, (t, 1), 0).astype(F32)

    q_b = q.astype(BF16)
    k_b = k.astype(BF16)
    zq = jnp.zeros_like(q_b)
    zv = jnp.zeros_like(v)
    out = None
    for h in range(N_HEADS):
        lg_h = lg[:, h * HEAD_DIM:h * HEAD_DIM + 1]
        s = _dot_nt(jnp.where(head == h, q_b, zq), k_b)
        w = s * jnp.where(reach, jnp.exp(lg_h * dist), 0.0)
        d = _dot(w.astype(BF16), jnp.where(head == h, v, zv))
        out = d if out is None else out + d

    state = state_ref[...]
    q_head = (q * jnp.exp(lg * (pos + 1.0))).astype(BF16)
    out = out + _dot(q_head, state.astype(BF16))

    k_tail = k * jnp.exp(lg * (float(t - 1) - pos))
    kv = _dot(k_tail.T.astype(BF16), v)
    state_ref[...] = state * jnp.exp(lg * float(t)) + jnp.where(same_head, kv, 0.0)

    avg = jnp.where(same_head, 1.0 / HEAD_DIM, 0.0).astype(F32)
    mu = _dot_f32(out, avg)
    cen = out - mu
    var = _dot_f32(cen * cen, avg)
    y = cen * lax.rsqrt(var + EPS) * gain_ref[...]
    g = g_ref[...]
    o_ref[...] = (y * (g * (1.0 / (1.0 + jnp.exp(-g))))).astype(o_ref.dtype)


def _retention(q, k, v, g, log_gamma, gain):
    b, s, _ = q.shape
    t = min(T_RET, s)
    row = pl.BlockSpec((None, t, GROUP), lambda i, j: (i, j, 0))
    const = pl.BlockSpec((1, GROUP), lambda i, j: (0, 0))
    return pl.pallas_call(
        functools.partial(_retention_kernel, t=t),
        out_shape=jax.ShapeDtypeStruct((b, s, GROUP), BF16),
        grid=(b, s // t),
        in_specs=[row, row, row, row, const, const],
        out_specs=row,
        scratch_shapes=[pltpu.VMEM((GROUP, GROUP), F32)],
        compiler_params=pltpu.CompilerParams(
            dimension_semantics=("parallel", "arbitrary"),
            vmem_limit_bytes=VMEM_LIMIT_BYTES),
        name="retention",
    )(q, k, v, g, log_gamma, gain)


def _post_kernel(x_ref, a_ref, b_ref, c_ref, d_ref, wo_ref, gpost_ref, gpre_ref,
                 wup_ref, wdn_ref, gfpost_ref, o_ref, *, f_chunk):
    mixed = jnp.concatenate([a_ref[...], b_ref[...], c_ref[...], d_ref[...]], axis=-1)
    x1 = x_ref[...] + _rms(_dot(mixed, wo_ref[...]), gpost_ref[...])
    h = _rms(x1, gpre_ref[...]).astype(BF16)
    y = None
    for c in range(wup_ref.shape[1] // f_chunk):
        u = jnp.maximum(_dot(h, wup_ref[:, c * f_chunk:(c + 1) * f_chunk]), 0.0)
        d = _dot((u * u).astype(BF16), wdn_ref[c * f_chunk:(c + 1) * f_chunk, :])
        y = d if y is None else y + d
    o_ref[...] = x1 + _rms(y, gfpost_ref[...])


def _post(x, a, bb, c, dd, w_out, g_post, g_pre, w_up, w_dn, g_fpost):
    b, s, d = x.shape
    tm = min(TM_POST, s)
    row = lambda w: pl.BlockSpec((None, tm, w), lambda i, j: (i, j, 0))
    const = lambda shape: pl.BlockSpec(shape, lambda i, j: (0,) * len(shape))
    return pl.pallas_call(
        functools.partial(_post_kernel, f_chunk=1024),
        out_shape=jax.ShapeDtypeStruct((b, s, d), F32),
        grid=(b, s // tm),
        in_specs=[row(d), row(GROUP), row(GROUP), row(GROUP), row(GROUP),
                  const(w_out.shape), const((1, d)), const((1, d)),
                  const(w_up.shape), const(w_dn.shape), const((1, d))],
        out_specs=row(d),
        compiler_params=pltpu.CompilerParams(
            dimension_semantics=("parallel", "parallel"),
            vmem_limit_bytes=VMEM_LIMIT_BYTES),
        name="out_proj_mlp",
    )(x, a, bb, c, dd, w_out, g_post, g_pre, w_up, w_dn, g_fpost)


def _rot_perm(width, period):
    j = np.arange(width)
    return (j // period) * period + (j % period + period // 2) % period


def _layer_weights(w_in, b_forget, w_q_up, w_kv_up):
    sizes = [GROUP, GROUP, GROUP, N_HEADS, 256, 128, MLA_ROPE] + [GROUP] * 7
    offs = np.concatenate([[0], np.cumsum(sizes)])
    (fq, fk, fv, ff, cq, ckv, kr, rq, rk, rv, rg, sq, sk, sv) = [
        w_in[:, int(offs[n]):int(offs[n + 1])] for n in range(len(sizes))]
    d = w_in.shape[0]
    scale = HEAD_DIM ** -0.5
    ret_perm = _rot_perm(GROUP, HEAD_DIM)
    kr_perm = _rot_perm(MLA_ROPE, MLA_ROPE)
    ff_pad = jnp.concatenate([ff, jnp.zeros((d, LANES - N_HEADS), F32)], axis=1)
    w_cat = jnp.concatenate(
        [fq * scale, fk, fv, ff_pad, cq, ckv,
         jnp.tile(kr, (1, N_HEADS)), jnp.tile(kr[:, kr_perm], (1, N_HEADS)),
         rq, rq[:, ret_perm], rk * scale, rk[:, ret_perm] * scale, rv, rg,
         sq * scale, sk, sv], axis=1).astype(BF16)
    bf_pad = jnp.concatenate([b_forget, jnp.zeros((LANES - N_HEADS,), F32)])[None, :]
    wq3 = w_q_up.reshape(w_q_up.shape[0], N_HEADS, MLA_QK)
    q_nope = wq3[:, :, :HEAD_DIM].reshape(-1, GROUP)
    q_rope = wq3[:, :, HEAD_DIM:].reshape(-1, N_HEADS * MLA_ROPE)
    w_q = jnp.concatenate(
        [q_nope, q_rope, q_rope[:, _rot_perm(N_HEADS * MLA_ROPE, MLA_ROPE)]], axis=1).astype(BF16)
    wkv3 = w_kv_up.reshape(w_kv_up.shape[0], N_HEADS, 2 * HEAD_DIM)
    w_kv = jnp.concatenate([wkv3[:, :, :HEAD_DIM].reshape(-1, GROUP),
                            wkv3[:, :, HEAD_DIM:].reshape(-1, GROUP)], axis=1).astype(BF16)
    return w_cat, bf_pad, w_q, w_kv


def kernel(x, positions, g_mix_pre, w_in, b_forget, g_q_lora, w_q_up, g_kv_lora, w_kv_up,
           g_mix_out, w_out, g_mix_post, g_ffn_pre, w_ffn_up, w_ffn_down, g_ffn_post):
    depth = w_in.shape[0]
    tab = _rope_tables(positions)
    log_gamma = jnp.log1p(-jnp.power(2.0, -RET_DECAY_OFFSET - jnp.arange(N_HEADS, dtype=F32)))
    log_gamma = jnp.repeat(log_gamma, HEAD_DIM)[None, :]
    for layer in range(depth):
        w_cat, bf_pad, w_q, w_kv = _layer_weights(
            w_in[layer], b_forget[layer], w_q_up[layer], w_kv_up[layer])
        (fq, fkt, fv, cum, mq, mkt, mv, rq, rk, rv, rg, sq, skt, sv) = _projection(
            x, tab, g_mix_pre[layer][None, :], w_cat, bf_pad,
            g_q_lora[layer][None, :], w_q, g_kv_lora[layer][None, :], w_kv)
        gmo = g_mix_out[layer]
        out_a = _softmax_attention(fq, fkt, fv, gmo[None, 0:GROUP], cum=cum)
        out_b = _softmax_attention(mq, mkt, mv, gmo[None, GROUP:2 * GROUP])
        out_c = _retention(rq, rk, rv, rg, log_gamma, gmo[None, 2 * GROUP:3 * GROUP])
        out_d = _stick_attention(sq, skt, sv, gmo[None, 3 * GROUP:])
        x = _post(x, out_a, out_b, out_c, out_d, w_out[layer].astype(BF16),
                  g_mix_post[layer][None, :], g_ffn_pre[layer][None, :],
                  w_ffn_up[layer].astype(BF16), w_ffn_down[layer].astype(BF16),
                  g_ffn_post[layer][None, :])
    return x
```

```python
import functools

import numpy as np
import jax
import jax.numpy as jnp
from jax import lax
from jax.experimental import pallas as pl
from jax.experimental.pallas import tpu as pltpu

F32 = jnp.float32
BF16 = jnp.bfloat16

N_HEADS = 4
HEAD_DIM = 64
GROUP = N_HEADS * HEAD_DIM
CHUNK = 64
CHUNK_SHIFT = 6
MLA_ROPE = 32
MLA_QK = HEAD_DIM + MLA_ROPE
ROPE_BASE = 10000.0
EPS = 1e-6
RET_DECAY_OFFSET = 5.0
NEG_BIG = -1e30
LOG2E = 1.4426950408889634
STICK_EXIT = 128.0

LANES = 128
VMEM_LIMIT_BYTES = 56 * 1024 * 1024

T_KEY = 512
TM_POST = 512
TQ_ATTN = 512
T_STICK = 256
T_RET = 256
TM_TAB = 1024

C_FOX = 0
C_FF = 768
C_CQ = 896
C_CKV = 1152
C_KR = 1280
C_RET = 1536
C_RVG = 2560
C_SB = 3072
C_END = 3840


def _rms(x, g):
    return x * lax.rsqrt(jnp.mean(x * x, axis=-1, keepdims=True) + EPS) * g


def _log_sigmoid(x):
    return jnp.minimum(x, 0.0) - jnp.log1p(jnp.exp(-jnp.abs(x)))


def _dot(a, b):
    return jnp.dot(a, b, preferred_element_type=F32)


def _head_select(lane_head, per_head):
    out = per_head[N_HEADS - 1]
    for h in range(N_HEADS - 2, -1, -1):
        out = jnp.where(lane_head == h, per_head[h], out)
    return out


def _per_head(fn, x, tq):
    return jnp.concatenate([fn(h, x[h * tq:(h + 1) * tq]) for h in range(N_HEADS)], axis=0)


def _table_kernel(pos_ref, invf_ref, sign_ref, tab_ref):
    ang = pos_ref[...].astype(F32) * invf_ref[...]
    c = jnp.cos(ang)
    s = jnp.sin(ang) * sign_ref[...]
    tab_ref[...] = jnp.concatenate(
        [c[:, :LANES], s[:, :LANES], c[:, LANES:], s[:, LANES:]], axis=-1)


def _rope_tables(positions):
    b, s = positions.shape
    half_r, half_m = HEAD_DIM // 2, MLA_ROPE // 2
    invf_r = ROPE_BASE ** (-jnp.arange(half_r, dtype=F32) / half_r)
    invf_m = ROPE_BASE ** (-jnp.arange(half_m, dtype=F32) / half_m)
    invf = jnp.concatenate([jnp.tile(invf_r, LANES // half_r),
                            jnp.tile(invf_m, LANES // half_m)])[None, :]
    lane = np.arange(LANES)
    sign = np.concatenate([np.where(lane % HEAD_DIM < half_r, -1.0, 1.0),
                           np.where(lane % MLA_ROPE < half_m, -1.0, 1.0)])
    sign = jnp.asarray(sign, F32)[None, :]
    tm = min(TM_TAB, s)
    return pl.pallas_call(
        _table_kernel,
        out_shape=jax.ShapeDtypeStruct((b, s, 4 * LANES), F32),
        grid=(b, s // tm),
        in_specs=[pl.BlockSpec((None, tm, 1), lambda i, j: (i, j, 0)),
                  pl.BlockSpec((1, 2 * LANES), lambda i, j: (0, 0)),
                  pl.BlockSpec((1, 2 * LANES), lambda i, j: (0, 0))],
        out_specs=pl.BlockSpec((None, tm, 4 * LANES), lambda i, j: (i, j, 0)),
        compiler_params=pltpu.CompilerParams(
            dimension_semantics=("parallel", "parallel")),
        name="rope_tables",
    )(positions[:, :, None], invf, sign)


def _proj_kernel(x_ref, tab_ref, g_ref, w_ref, bf_ref, gq_ref, wq_ref, gkv_ref, wkv_ref,
                 fq_o, fkt_o, fv_o, mq_o, mkt_o, mv_o,
                 rq_o, rk_o, rv_o, rg_o, sq_o, skt_o, sv_o, carry_ref, *, mla_scale):
    tm = x_ref.shape[0]
    h = _rms(x_ref[...], g_ref[...]).astype(BF16)

    def proj(a, b):
        return _dot(h, w_ref[:, a:b])

    tab = tab_ref[...]
    cos_r, sin_r = tab[:, 0:LANES], tab[:, LANES:2 * LANES]
    cos_m, sin_m = tab[:, 2 * LANES:3 * LANES], tab[:, 3 * LANES:]
    cos_r2 = jnp.concatenate([cos_r, cos_r], axis=-1)
    sin_r2 = jnp.concatenate([sin_r, sin_r], axis=-1)
    head64 = lax.broadcasted_iota(jnp.int32, (1, GROUP), 1) >> 6

    def store_heads(o_ref, q):
        qb = q.astype(BF16)
        zero = jnp.zeros_like(qb)
        for hh in range(N_HEADS):
            o_ref[hh] = jnp.where(head64 == hh, qb, zero)

    @pl.when(pl.program_id(1) == 0)
    def _():
        carry_ref[...] = jnp.zeros_like(carry_ref)

    ls = _log_sigmoid(proj(C_FF, C_CQ) + bf_ref[...])
    row = lax.broadcasted_iota(jnp.int32, (tm, tm), 0)
    col = lax.broadcasted_iota(jnp.int32, (tm, tm), 1)
    tri = jnp.where(col <= row, 1.0, 0.0).astype(BF16)
    ls_hi = ls.astype(BF16)
    ls_rest = ls - ls_hi.astype(F32)
    ls_mid = ls_rest.astype(BF16)
    ls_lo = (ls_rest - ls_mid.astype(F32)).astype(BF16)
    parts = _dot(tri, jnp.concatenate([ls_hi, ls_mid, ls_lo], axis=-1))
    cum = (parts[:, :LANES] + parts[:, LANES:2 * LANES] + parts[:, 2 * LANES:]) + carry_ref[...]
    carry_ref[...] = cum[tm - 1:tm, :]
    c_hi = (cum * LOG2E).astype(BF16).astype(F32)
    c_rest = cum * LOG2E - c_hi
    c_mid = c_rest.astype(BF16).astype(F32)
    c_lo = (c_rest - c_mid).astype(BF16).astype(F32)
    lane = lax.broadcasted_iota(jnp.int32, (1, LANES), 1)
    term_of_lane = lane & 3
    spread = jnp.zeros((tm, LANES), F32)
    for hh in range(N_HEADS):
        for t, term in enumerate((c_hi, c_mid, c_lo)):
            val = jnp.broadcast_to(term[:, hh:hh + 1], (tm, LANES))
            spread = jnp.where(((lane >> 3) == hh) & (term_of_lane == t), val, spread)
    used = (lane < 8 * N_HEADS) & (term_of_lane < 3)
    first = used & ((lane & 4) == 0)
    second = used & ((lane & 4) != 0)
    gate_q = jnp.where(first, spread, jnp.where(second, 1.0, 0.0))
    gate_k = jnp.where(second, -spread, jnp.where(first, 1.0, 0.0))

    p = proj(C_FOX, C_FF)
    fq = p[:, 0:GROUP] * LOG2E
    for hh in range(N_HEADS):
        pair = hh // 2
        nope = jnp.where((lane >> 6) == hh % 2, fq[:, pair * LANES:(pair + 1) * LANES], 0.0)
        gate = jnp.where((lane >> 3) == hh, gate_q, 0.0)
        fq_o[hh] = jnp.concatenate([nope, gate], axis=-1).astype(BF16)
    for pair in range(2):
        kp = jnp.concatenate([p[:, GROUP + pair * LANES:GROUP + (pair + 1) * LANES], gate_k],
                             axis=-1)
        fkt_o[pair * GROUP:(pair + 1) * GROUP, :] = kp.T.astype(BF16)
    fv_o[...] = p[:, 2 * GROUP:].astype(BF16)

    hq = _rms(proj(C_CQ, C_CKV), gq_ref[...]).astype(BF16)
    qm = _dot(hq, wq_ref[...]) * mla_scale
    q_nope = qm[:, :GROUP].astype(BF16)
    q_rope = (qm[:, GROUP:GROUP + LANES] * cos_m + qm[:, GROUP + LANES:] * sin_m).astype(BF16)
    zero = jnp.zeros_like(q_rope)
    for hh in range(N_HEADS):
        pair = hh // 2
        nope = jnp.where((lane >> 6) == hh % 2, q_nope[:, pair * LANES:(pair + 1) * LANES], zero)
        rope = jnp.where((lane >> 5) == hh, q_rope, zero)
        mq_o[hh] = jnp.concatenate([nope, rope], axis=-1)
    hkv = _rms(proj(C_CKV, C_KR), gkv_ref[...]).astype(BF16)
    kv = _dot(hkv, wkv_ref[...])
    pk = proj(C_KR, C_RET)
    k_rope = pk[:, :LANES] * cos_m + pk[:, LANES:] * sin_m
    for pair in range(2):
        kp = jnp.concatenate([kv[:, pair * LANES:(pair + 1) * LANES], k_rope], axis=-1)
        mkt_o[pair * GROUP:(pair + 1) * GROUP, :] = kp.T.astype(BF16)
    mv_o[...] = kv[:, GROUP:].astype(BF16)

    pr = proj(C_RET, C_RVG)
    rq_o[...] = pr[:, 0:GROUP] * cos_r2 + pr[:, GROUP:2 * GROUP] * sin_r2
    rk_o[...] = pr[:, 2 * GROUP:3 * GROUP] * cos_r2 + pr[:, 3 * GROUP:] * sin_r2
    pv = proj(C_RVG, C_SB)
    rv_o[...] = pv[:, :GROUP].astype(BF16)
    rg_o[...] = pv[:, GROUP:]

    ps = proj(C_SB, C_END)
    store_heads(sq_o, ps[:, 0:GROUP])
    skt = ps[:, GROUP:2 * GROUP].T.astype(BF16)
    for c in range(tm // T_STICK):
        skt_o[c] = skt[:, c * T_STICK:(c + 1) * T_STICK]
    sv_o[...] = ps[:, 2 * GROUP:].astype(BF16)


def _projection(x, tab, g_pre, w_cat, bf_pad, g_q, w_q, g_kv, w_kv):
    b, s, d = x.shape
    tm = min(T_KEY, s)
    nk = s // tm
    row = lambda w: pl.BlockSpec((None, tm, w), lambda i, j: (i, j, 0))
    heads = pl.BlockSpec((None, N_HEADS, tm, GROUP), lambda i, j: (i, 0, j, 0))
    keyt = lambda r: pl.BlockSpec((None, None, r, tm), lambda i, j: (i, j, 0, 0))
    const = lambda shape: pl.BlockSpec(shape, lambda i, j: (0,) * len(shape))
    bf = lambda w: jax.ShapeDtypeStruct((b, s, w), BF16)
    f32 = lambda w: jax.ShapeDtypeStruct((b, s, w), F32)
    q4 = jax.ShapeDtypeStruct((b, N_HEADS, s, GROUP), BF16)
    kt = lambda r: jax.ShapeDtypeStruct((b, nk, r, tm), BF16)
    n_st = tm // T_STICK
    skt_shape = jax.ShapeDtypeStruct((b, nk * n_st, GROUP, T_STICK), BF16)
    skt_spec = pl.BlockSpec((None, n_st, GROUP, T_STICK), lambda i, j: (i, j, 0, 0))
    out_shape = [q4, kt(2 * GROUP), bf(GROUP),
                 q4, kt(2 * GROUP), bf(GROUP),
                 f32(GROUP), f32(GROUP), bf(GROUP), f32(GROUP),
                 q4, skt_shape, bf(GROUP)]
    out_specs = [heads, keyt(2 * GROUP), row(GROUP),
                 heads, keyt(2 * GROUP), row(GROUP),
                 row(GROUP), row(GROUP), row(GROUP), row(GROUP),
                 heads, skt_spec, row(GROUP)]
    return pl.pallas_call(
        functools.partial(_proj_kernel, mla_scale=MLA_QK ** -0.5 * LOG2E),
        out_shape=out_shape,
        grid=(b, nk),
        in_specs=[row(d), row(4 * LANES), const((1, d)), const(w_cat.shape),
                  const((1, LANES)), const(g_q.shape), const(w_q.shape),
                  const(g_kv.shape), const(w_kv.shape)],
        out_specs=out_specs,
        scratch_shapes=[pltpu.VMEM((1, LANES), F32)],
        compiler_params=pltpu.CompilerParams(
            dimension_semantics=("parallel", "arbitrary"),
            vmem_limit_bytes=VMEM_LIMIT_BYTES),
        name="in_projection",
    )(x, tab, g_pre, w_cat, bf_pad, g_q, w_q, g_kv, w_kv)


def _softmax_attn_kernel(q_ref, kt_ref, v_ref, gain_ref, o_ref, acc_ref, m_ref, l_ref,
                         *, chunk_causal, tq, tk):
    i = pl.program_id(1)
    groups = kt_ref.shape[1] // GROUP
    hpg = N_HEADS // groups

    acc_ref[...] = jnp.zeros_like(acc_ref)
    m_ref[...] = jnp.full_like(m_ref, NEG_BIG)
    l_ref[...] = jnp.zeros_like(l_ref)

    def block(j, masked):
        kt = kt_ref[j]
        if masked:
            qpos = i * tq + lax.broadcasted_iota(jnp.int32, (tq, tk), 0)
            kpos = j * tk + lax.broadcasted_iota(jnp.int32, (tq, tk), 1)
            if chunk_causal:
                visible = (kpos >> CHUNK_SHIFT) <= (qpos >> CHUNK_SHIFT)
            else:
                visible = kpos <= qpos
        s_parts = [_dot(q_ref[g * hpg:(g + 1) * hpg].reshape(hpg * tq, GROUP),
                        kt[g * GROUP:(g + 1) * GROUP]) for g in range(groups)]
        p_rows, a_rows = [], []
        for h in range(N_HEADS):
            hr = slice(h * tq, (h + 1) * tq)
            s = s_parts[h // hpg][(h % hpg) * tq:(h % hpg + 1) * tq]
            if masked:
                s = jnp.where(visible, s, NEG_BIG)
            m_cur = jnp.broadcast_to(jnp.max(s, axis=-1, keepdims=True), (tq, LANES))
            m_prev = m_ref[hr]
            m_new = jnp.maximum(m_prev, m_cur)
            alpha = jnp.exp2(m_prev - m_new)
            l_new = alpha * l_ref[hr]
            ps = []
            for c in range(tk // LANES):
                pc = jnp.exp2(s[:, c * LANES:(c + 1) * LANES] - m_new)
                l_new = l_new + pc
                ps.append(pc.astype(BF16))
            l_ref[hr] = l_new
            m_ref[hr] = m_new
            p_rows.append(jnp.concatenate(ps, axis=-1))
            a_rows.append(jnp.concatenate([alpha, alpha], axis=-1))
        start = pl.multiple_of(j * tk, tk)
        pv = _dot(jnp.concatenate(p_rows, axis=0), v_ref[pl.ds(start, tk), :])
        acc_ref[...] = acc_ref[...] * jnp.concatenate(a_rows, axis=0) + pv

    n_full = (i * tq) // tk

    def body(j, carry):
        block(j, False)
        return carry

    lax.fori_loop(0, n_full, body, 0)
    block(n_full, True)

    inv_l = 1.0 / jnp.sum(l_ref[...], axis=-1, keepdims=True)
    o_all = acc_ref[...] * inv_l
    head_v = lax.broadcasted_iota(jnp.int32, (1, GROUP), 1) >> 6
    o = _head_select(head_v, [o_all[h * tq:(h + 1) * tq] for h in range(N_HEADS)])
    o_ref[...] = _rms(o, gain_ref[...]).astype(o_ref.dtype)


def _softmax_attention(q, kt, v, gain, *, chunk_causal, name):
    b, _, s, _ = q.shape
    nk, krows, tk = kt.shape[1:]
    tq = min(TQ_ATTN, s)
    rows = N_HEADS * tq
    return pl.pallas_call(
        functools.partial(_softmax_attn_kernel, chunk_causal=chunk_causal, tq=tq, tk=tk),
        out_shape=jax.ShapeDtypeStruct((b, s, GROUP), BF16),
        grid=(b, s // tq),
        in_specs=[pl.BlockSpec((None, N_HEADS, tq, GROUP), lambda i, j: (i, 0, j, 0)),
                  pl.BlockSpec((None, nk, krows, tk), lambda i, j: (i, 0, 0, 0)),
                  pl.BlockSpec((None, s, GROUP), lambda i, j: (i, 0, 0)),
                  pl.BlockSpec((1, GROUP), lambda i, j: (0, 0))],
        out_specs=pl.BlockSpec((None, tq, GROUP), lambda i, j: (i, j, 0)),
        scratch_shapes=[pltpu.VMEM((rows, GROUP), F32),
                        pltpu.VMEM((rows, LANES), F32),
                        pltpu.VMEM((rows, LANES), F32)],
        compiler_params=pltpu.CompilerParams(
            dimension_semantics=("parallel", "parallel"),
            vmem_limit_bytes=VMEM_LIMIT_BYTES),
        name=name,
    )(q, kt, v, gain)


def _stick_kernel(q_ref, kt_ref, v_ref, gain_ref, o_ref, acc_ref, r_ref, *, t):
    i = pl.program_id(1)
    rows = N_HEADS * t
    q = q_ref[...].reshape(rows, GROUP)
    rj = lax.broadcasted_iota(jnp.int32, (t, t), 0)
    cs = lax.broadcasted_iota(jnp.int32, (t, t), 1)
    upper = jnp.where(rj > cs, 1.0, 0.0).astype(BF16)
    upper2 = jnp.concatenate([upper, upper], axis=0)
    visible = cs < rj

    def block(j, diagonal):
        z = _dot(q, kt_ref[j])
        leave = jnp.maximum(z, 0.0) + jnp.log(1.0 + jnp.exp(-jnp.abs(z)))
        if diagonal:
            leave = _per_head(lambda h, x: jnp.where(visible, x, 0.0), leave, t)
        hi = leave.astype(BF16)
        lo = (leave - hi.astype(F32)).astype(BF16)
        later = _dot(jnp.concatenate([hi, lo], axis=-1), upper2)
        row_sum = jnp.broadcast_to(jnp.sum(leave, axis=-1, keepdims=True), (rows, LANES))
        if diagonal:
            r_new = row_sum
        else:
            r_prev = r_ref[...]
            later = later + jnp.concatenate([r_prev] * (t // LANES), axis=-1)
            r_new = r_prev + row_sum
        w = jnp.exp(z - leave - later)
        if diagonal:
            w = _per_head(lambda h, x: jnp.where(visible, x, 0.0), w, t)
        r_ref[...] = r_new
        start = pl.multiple_of(j * t, t)
        pv = _dot(w.astype(BF16), v_ref[pl.ds(start, t), :])
        if diagonal:
            acc_ref[...] = pv
        else:
            acc_ref[...] += pv
        return jnp.min(r_new)

    r_min = block(i, True)

    def cond(carry):
        j, r_min = carry
        return jnp.logical_and(j >= 0, r_min < STICK_EXIT)

    def body(carry):
        j, _ = carry
        return j - 1, block(j, False)

    lax.while_loop(cond, body, (i - 1, r_min))
    acc = acc_ref[...]
    head_v = lax.broadcasted_iota(jnp.int32, (1, GROUP), 1) >> 6
    o = _head_select(head_v, [acc[h * t:(h + 1) * t] for h in range(N_HEADS)])
    o_ref[...] = _rms(o, gain_ref[...]).astype(o_ref.dtype)


def _stick_attention(q, kt, v, gain):
    b, _, s, _ = q.shape
    nk, krows, tk = kt.shape[1:]
    tq = tk
    rows = N_HEADS * tq
    return pl.pallas_call(
        functools.partial(_stick_kernel, t=tk),
        out_shape=jax.ShapeDtypeStruct((b, s, GROUP), BF16),
        grid=(b, s // tq),
        in_specs=[pl.BlockSpec((None, N_HEADS, tq, GROUP), lambda i, j: (i, 0, j, 0)),
                  pl.BlockSpec((None, nk, krows, tk), lambda i, j: (i, 0, 0, 0)),
                  pl.BlockSpec((None, s, GROUP), lambda i, j: (i, 0, 0)),
                  pl.BlockSpec((1, GROUP), lambda i, j: (0, 0))],
        out_specs=pl.BlockSpec((None, tq, GROUP), lambda i, j: (i, j, 0)),
        scratch_shapes=[pltpu.VMEM((rows, GROUP), F32),
                        pltpu.VMEM((rows, LANES), F32)],
        compiler_params=pltpu.CompilerParams(
            dimension_semantics=("parallel", "parallel"),
            vmem_limit_bytes=VMEM_LIMIT_BYTES),
        name="stick_attention",
    )(q, kt, v, gain)


def _retention_kernel(q_ref, k_ref, v_ref, g_ref, d_intra_ref, d_q_ref, d_kt_ref, d_state_ref,
                      gain_ref, o_ref, state_ref, *, t):
    @pl.when(pl.program_id(1) == 0)
    def _():
        state_ref[...] = jnp.zeros_like(state_ref)

    q = q_ref[...]
    v = v_ref[...]
    kt = k_ref[...].T
    head = lax.broadcasted_iota(jnp.int32, (1, GROUP), 1) >> 6
    head_r = lax.broadcasted_iota(jnp.int32, (GROUP, 1), 0) >> 6
    same_head = head_r == head

    q_b = q.astype(BF16)
    zq = jnp.zeros_like(q_b)
    q4 = jnp.concatenate([jnp.where(head == h, q_b, zq) for h in range(N_HEADS)], axis=0)
    w = (_dot(q4, kt.astype(BF16)) * d_intra_ref[...]).astype(BF16)
    wv = _dot(w, v)
    out = _head_select(head, [wv[h * t:(h + 1) * t] for h in range(N_HEADS)])

    state = state_ref[...]
    out = out + _dot((q * d_q_ref[...]).astype(BF16), state.astype(BF16))
    kv = _dot((kt * d_kt_ref[...]).astype(BF16), v)
    state_ref[...] = state * d_state_ref[...] + jnp.where(same_head, kv, 0.0)

    avg = jnp.where(same_head, 1.0 / HEAD_DIM, 0.0).astype(BF16)
    avg3 = jnp.concatenate([avg, avg, avg], axis=0)

    def head_mean(x):
        hi = x.astype(BF16)
        rest = x - hi.astype(F32)
        mid = rest.astype(BF16)
        lo = (rest - mid.astype(F32)).astype(BF16)
        return _dot(jnp.concatenate([hi, mid, lo], axis=-1), avg3)

    mu = head_mean(out)
    cen = out - mu
    var = head_mean(cen * cen)
    y = cen * lax.rsqrt(var + EPS) * gain_ref[...]
    g = g_ref[...]
    o_ref[...] = (y * (g * (1.0 / (1.0 + jnp.exp(-g))))).astype(o_ref.dtype)


def _retention_decay_tables(t):
    log_gamma = jnp.log1p(-jnp.power(2.0, -RET_DECAY_OFFSET - jnp.arange(N_HEADS, dtype=F32)))
    lg_lane = jnp.repeat(log_gamma, HEAD_DIM)
    r = jnp.arange(t, dtype=F32)
    dist = jnp.abs(r[:, None] - r[None, :])
    reach = (jnp.arange(t)[None, :] // CHUNK) <= (jnp.arange(t)[:, None] // CHUNK)
    d_intra = jnp.where(reach[None], jnp.exp(log_gamma[:, None, None] * dist[None]), 0.0)
    d_q = jnp.exp(lg_lane[None, :] * (r[:, None] + 1.0))
    d_kt = jnp.exp(lg_lane[:, None] * (float(t - 1) - r)[None, :])
    d_state = jnp.exp(lg_lane * float(t))[None, :]
    return d_intra.reshape(N_HEADS * t, t), d_q, d_kt, d_state


def _retention(q, k, v, g, gain):
    b, s, _ = q.shape
    t = min(T_RET, s)
    tables = _retention_decay_tables(t)
    row = pl.BlockSpec((None, t, GROUP), lambda i, j: (i, j, 0))
    const = lambda a: pl.BlockSpec(a.shape, lambda i, j: (0, 0))
    return pl.pallas_call(
        functools.partial(_retention_kernel, t=t),
        out_shape=jax.ShapeDtypeStruct((b, s, GROUP), BF16),
        grid=(b, s // t),
        in_specs=[row, row, row, row] + [const(a) for a in tables] + [const(gain)],
        out_specs=row,
        scratch_shapes=[pltpu.VMEM((GROUP, GROUP), F32)],
        compiler_params=pltpu.CompilerParams(
            dimension_semantics=("parallel", "arbitrary"),
            vmem_limit_bytes=VMEM_LIMIT_BYTES),
        name="retention",
    )(q, k, v, g, *tables, gain)


def _post_kernel(x_ref, a_ref, b_ref, c_ref, d_ref, wo_ref, gpost_ref, gpre_ref,
                 wup_ref, wdn_ref, gfpost_ref, o_ref, *, f_chunk):
    mixed = jnp.concatenate([a_ref[...], b_ref[...], c_ref[...], d_ref[...]], axis=-1)
    x1 = x_ref[...] + _rms(_dot(mixed, wo_ref[...]), gpost_ref[...])
    h = _rms(x1, gpre_ref[...]).astype(BF16)
    y = None
    for c in range(wup_ref.shape[1] // f_chunk):
        u = jnp.maximum(_dot(h, wup_ref[:, c * f_chunk:(c + 1) * f_chunk]), 0.0)
        d = _dot((u * u).astype(BF16), wdn_ref[c * f_chunk:(c + 1) * f_chunk, :])
        y = d if y is None else y + d
    o_ref[...] = x1 + _rms(y, gfpost_ref[...])


def _post(x, a, bb, c, dd, w_out, g_post, g_pre, w_up, w_dn, g_fpost):
    b, s, d = x.shape
    tm = min(TM_POST, s)
    row = lambda w: pl.BlockSpec((None, tm, w), lambda i, j: (i, j, 0))
    const = lambda shape: pl.BlockSpec(shape, lambda i, j: (0,) * len(shape))
    return pl.pallas_call(
        functools.partial(_post_kernel, f_chunk=1024),
        out_shape=jax.ShapeDtypeStruct((b, s, d), F32),
        grid=(b, s // tm),
        in_specs=[row(d), row(GROUP), row(GROUP), row(GROUP), row(GROUP),
                  const(w_out.shape), const((1, d)), const((1, d)),
                  const(w_up.shape), const(w_dn.shape), const((1, d))],
        out_specs=row(d),
        compiler_params=pltpu.CompilerParams(
            dimension_semantics=("parallel", "parallel"),
            vmem_limit_bytes=VMEM_LIMIT_BYTES),
        name="out_proj_mlp",
    )(x, a, bb, c, dd, w_out, g_post, g_pre, w_up, w_dn, g_fpost)


def _rot_perm(width, period):
    j = np.arange(width)
    return (j // period) * period + (j % period + period // 2) % period


def _layer_weights(w_in, b_forget, w_q_up, w_kv_up):
    sizes = [GROUP, GROUP, GROUP, N_HEADS, 256, 128, MLA_ROPE] + [GROUP] * 7
    offs = np.concatenate([[0], np.cumsum(sizes)])
    (fq, fk, fv, ff, cq, ckv, kr, rq, rk, rv, rg, sq, sk, sv) = [
        w_in[:, int(offs[n]):int(offs[n + 1])] for n in range(len(sizes))]
    d = w_in.shape[0]
    scale = HEAD_DIM ** -0.5
    ret_perm = _rot_perm(GROUP, HEAD_DIM)
    kr_perm = _rot_perm(MLA_ROPE, MLA_ROPE)
    ff_pad = jnp.concatenate([ff, jnp.zeros((d, LANES - N_HEADS), F32)], axis=1)
    w_cat = jnp.concatenate(
        [fq * scale, fk, fv, ff_pad, cq, ckv,
         jnp.tile(kr, (1, N_HEADS)), jnp.tile(kr[:, kr_perm], (1, N_HEADS)),
         rq, rq[:, ret_perm], rk * scale, rk[:, ret_perm] * scale, rv, rg,
         sq * scale, sk, sv], axis=1).astype(BF16)
    bf_pad = jnp.concatenate([b_forget, jnp.zeros((LANES - N_HEADS,), F32)])[None, :]
    wq3 = w_q_up.reshape(w_q_up.shape[0], N_HEADS, MLA_QK)
    q_nope = wq3[:, :, :HEAD_DIM].reshape(-1, GROUP)
    q_rope = wq3[:, :, HEAD_DIM:].reshape(-1, N_HEADS * MLA_ROPE)
    w_q = jnp.concatenate(
        [q_nope, q_rope, q_rope[:, _rot_perm(N_HEADS * MLA_ROPE, MLA_ROPE)]], axis=1).astype(BF16)
    wkv3 = w_kv_up.reshape(w_kv_up.shape[0], N_HEADS, 2 * HEAD_DIM)
    w_kv = jnp.concatenate([wkv3[:, :, :HEAD_DIM].reshape(-1, GROUP),
                            wkv3[:, :, HEAD_DIM:].reshape(-1, GROUP)], axis=1).astype(BF16)
    return w_cat, bf_pad, w_q, w_kv


def kernel(x, positions, g_mix_pre, w_in, b_forget, g_q_lora, w_q_up, g_kv_lora, w_kv_up,
           g_mix_out, w_out, g_mix_post, g_ffn_pre, w_ffn_up, w_ffn_down, g_ffn_post):
    depth = w_in.shape[0]
    tab = _rope_tables(positions)
    for layer in range(depth):
        w_cat, bf_pad, w_q, w_kv = _layer_weights(
            w_in[layer], b_forget[layer], w_q_up[layer], w_kv_up[layer])
        (fq, fkt, fv, mq, mkt, mv, rq, rk, rv, rg, sq, skt, sv) = _projection(
            x, tab, g_mix_pre[layer][None, :], w_cat, bf_pad,
            g_q_lora[layer][None, :], w_q, g_kv_lora[layer][None, :], w_kv)
        gmo = g_mix_out[layer]
        out_a = _softmax_attention(fq, fkt, fv, gmo[None, 0:GROUP],
                                   chunk_causal=False, name="fox_attention")
        out_b = _softmax_attention(mq, mkt, mv, gmo[None, GROUP:2 * GROUP],
                                   chunk_causal=True, name="mla_attention")
        out_c = _retention(rq, rk, rv, rg, gmo[None, 2 * GROUP:3 * GROUP])
        out_d = _stick_attention(sq, skt, sv, gmo[None, 3 * GROUP:])
        x = _post(x, out_a, out_b, out_c, out_d, w_out[layer].astype(BF16),
                  g_mix_post[layer][None, :], g_ffn_pre[layer][None, :],
                  w_ffn_up[layer].astype(BF16), w_ffn_down[layer].astype(BF16),
                  g_ffn_post[layer][None, :])
    return x
```

```python
import functools

import numpy as np
import jax
import jax.numpy as jnp
from jax import lax
from jax.experimental import pallas as pl
from jax.experimental.pallas import tpu as pltpu

F32 = jnp.float32
BF16 = jnp.bfloat16

N_HEADS = 4
HEAD_DIM = 64
GROUP = N_HEADS * HEAD_DIM
CHUNK = 64
CHUNK_SHIFT = 6
MLA_ROPE = 32
MLA_QK = HEAD_DIM + MLA_ROPE
ROPE_BASE = 10000.0
EPS = 1e-6
RET_DECAY_OFFSET = 5.0
NEG_BIG = -1e30
LOG2E = 1.4426950408889634
STICK_EXIT = 128.0

LANES = 128
VMEM_LIMIT_BYTES = 56 * 1024 * 1024

T_KEY = 512
TM_POST = 512
TQ_ATTN = 512
T_STICK = 256
T_RET = 256
TM_TAB = 1024

C_FOX = 0
C_CQ = 768
C_CKV = 1024
C_MISC = 1152
C_RET = 1280
C_RVG = 1792
C_SB = 2304
C_END = 3072
MISC_ROT = 32
MISC_FF = 64


def _rms(x, g):
    return x * lax.rsqrt(jnp.mean(x * x, axis=-1, keepdims=True) + EPS) * g


def _log_sigmoid(x):
    return jnp.minimum(x, 0.0) - jnp.log1p(jnp.exp(-jnp.abs(x)))


def _dot(a, b):
    return jnp.dot(a, b, preferred_element_type=F32)


def _head_select(lane_head, per_head):
    out = per_head[N_HEADS - 1]
    for h in range(N_HEADS - 2, -1, -1):
        out = jnp.where(lane_head == h, per_head[h], out)
    return out


def _layer_spec(a, layer):
    zeros = (0,) * (a.ndim - 1)
    return pl.BlockSpec((None,) + a.shape[1:], lambda i, j: (layer,) + zeros)


def _gain_spec(layer, group):
    return pl.BlockSpec((None, None, 1, GROUP), lambda i, j: (layer, group, 0, 0))


def _per_head(fn, x, tq):
    return jnp.concatenate([fn(h, x[h * tq:(h + 1) * tq]) for h in range(N_HEADS)], axis=0)


def _table_kernel(pos_ref, invf_ref, sign_ref, tab_ref):
    ang = pos_ref[...].astype(F32) * invf_ref[...]
    c = jnp.cos(ang)
    s = jnp.sin(ang) * sign_ref[...]
    tab_ref[...] = jnp.concatenate(
        [c[:, :LANES], s[:, :LANES], c[:, LANES:], s[:, LANES:]], axis=-1)


def _rope_tables(positions):
    b, s = positions.shape
    half_r, half_m = HEAD_DIM // 2, MLA_ROPE // 2
    invf_r = ROPE_BASE ** (-jnp.arange(half_r, dtype=F32) / half_r)
    invf_m = ROPE_BASE ** (-jnp.arange(half_m, dtype=F32) / half_m)
    invf = jnp.concatenate([jnp.tile(invf_r, LANES // half_r),
                            jnp.tile(invf_m, LANES // half_m)])[None, :]
    lane = np.arange(LANES)
    sign = np.concatenate([np.ones(LANES), np.where(lane % MLA_ROPE < half_m, -1.0, 1.0)])
    sign = jnp.asarray(sign, F32)[None, :]
    tm = min(TM_TAB, s)
    return pl.pallas_call(
        _table_kernel,
        out_shape=jax.ShapeDtypeStruct((b, s, 4 * LANES), F32),
        grid=(b, s // tm),
        in_specs=[pl.BlockSpec((None, tm, 1), lambda i, j: (i, j, 0)),
                  pl.BlockSpec((1, 2 * LANES), lambda i, j: (0, 0)),
                  pl.BlockSpec((1, 2 * LANES), lambda i, j: (0, 0))],
        out_specs=pl.BlockSpec((None, tm, 4 * LANES), lambda i, j: (i, j, 0)),
        compiler_params=pltpu.CompilerParams(
            dimension_semantics=("parallel", "parallel")),
        name="rope_tables",
    )(positions[:, :, None], invf, sign)


def _proj_kernel(x_ref, tab_ref, g_ref, w_ref, bf_ref, gq_ref, wq_ref, gkv_ref, wkv_ref,
                 fq_o, fkt_o, fv_o, mq_o, mkt_o, mv_o,
                 rq_o, rk_o, rv_o, rg_o, sq_o, skt_o, sv_o, carry_ref, *, mla_scale):
    tm = x_ref.shape[0]
    h = _rms(x_ref[...], g_ref[...]).astype(BF16)

    def proj(a, b):
        return _dot(h, w_ref[:, a:b])

    tab = tab_ref[...]
    cos_r, sin_r = tab[:, 0:LANES], tab[:, LANES:2 * LANES]
    cos_m, sin_m = tab[:, 2 * LANES:3 * LANES], tab[:, 3 * LANES:]
    head64 = lax.broadcasted_iota(jnp.int32, (1, GROUP), 1) >> 6
    lane = lax.broadcasted_iota(jnp.int32, (1, LANES), 1)
    misc = proj(C_MISC, C_RET)

    def store_heads(o_ref, q):
        qb = q.astype(BF16)
        zero = jnp.zeros_like(qb)
        for hh in range(N_HEADS):
            o_ref[hh] = jnp.where(head64 == hh, qb, zero)

    @pl.when(pl.program_id(1) == 0)
    def _():
        carry_ref[...] = jnp.zeros_like(carry_ref)

    ls = _log_sigmoid(pltpu.roll(misc, LANES - MISC_FF, 1) + bf_ref[...])
    row = lax.broadcasted_iota(jnp.int32, (tm, tm), 0)
    col = lax.broadcasted_iota(jnp.int32, (tm, tm), 1)
    tri = jnp.where(col <= row, 1.0, 0.0).astype(BF16)
    ls_hi = ls.astype(BF16)
    ls_rest = ls - ls_hi.astype(F32)
    ls_mid = ls_rest.astype(BF16)
    ls_lo = (ls_rest - ls_mid.astype(F32)).astype(BF16)
    parts = _dot(tri, jnp.concatenate([ls_hi, ls_mid, ls_lo], axis=-1))
    cum = (parts[:, :LANES] + parts[:, LANES:2 * LANES] + parts[:, 2 * LANES:]) + carry_ref[...]
    carry_ref[...] = cum[tm - 1:tm, :]
    c_hi = (cum * LOG2E).astype(BF16).astype(F32)
    c_rest = cum * LOG2E - c_hi
    c_mid = c_rest.astype(BF16).astype(F32)
    c_lo = (c_rest - c_mid).astype(BF16).astype(F32)
    term_of_lane = lane & 3
    spread = jnp.zeros((tm, LANES), F32)
    for hh in range(N_HEADS):
        for t, term in enumerate((c_hi, c_mid, c_lo)):
            val = jnp.broadcast_to(term[:, hh:hh + 1], (tm, LANES))
            spread = jnp.where(((lane >> 3) == hh) & (term_of_lane == t), val, spread)
    used = (lane < 8 * N_HEADS) & (term_of_lane < 3)
    first = used & ((lane & 4) == 0)
    second = used & ((lane & 4) != 0)
    gate_q = jnp.where(first, spread, jnp.where(second, 1.0, 0.0))
    gate_k = jnp.where(second, -spread, jnp.where(first, 1.0, 0.0))

    p = proj(C_FOX, C_CQ)
    fq = p[:, 0:GROUP] * LOG2E
    for hh in range(N_HEADS):
        pair = hh // 2
        nope = jnp.where((lane >> 6) == hh % 2, fq[:, pair * LANES:(pair + 1) * LANES], 0.0)
        gate = jnp.where((lane >> 3) == hh, gate_q, 0.0)
        fq_o[hh] = jnp.concatenate([nope, gate], axis=-1).astype(BF16)
    for pair in range(2):
        kp = jnp.concatenate([p[:, GROUP + pair * LANES:GROUP + (pair + 1) * LANES], gate_k],
                             axis=-1)
        fkt_o[pair * GROUP:(pair + 1) * GROUP, :] = kp.T.astype(BF16)
    fv_o[...] = p[:, 2 * GROUP:].astype(BF16)

    hq = _rms(proj(C_CQ, C_CKV), gq_ref[...]).astype(BF16)
    qm = _dot(hq, wq_ref[...]) * mla_scale
    q_nope = qm[:, :GROUP].astype(BF16)
    q_rope = (qm[:, GROUP:GROUP + LANES] * cos_m + qm[:, GROUP + LANES:] * sin_m).astype(BF16)
    zero = jnp.zeros_like(q_rope)
    for hh in range(N_HEADS):
        pair = hh // 2
        nope = jnp.where((lane >> 6) == hh % 2, q_nope[:, pair * LANES:(pair + 1) * LANES], zero)
        rope = jnp.where((lane >> 5) == hh, q_rope, zero)
        mq_o[hh] = jnp.concatenate([nope, rope], axis=-1)
    hkv = _rms(proj(C_CKV, C_MISC), gkv_ref[...]).astype(BF16)
    kv = _dot(hkv, wkv_ref[...])
    in_rope = lane < MLA_ROPE
    k_rope = (misc * jnp.where(in_rope, cos_m, 0.0)
              + pltpu.roll(misc, LANES - MISC_ROT, 1) * jnp.where(in_rope, sin_m, 0.0))
    k_rope = (k_rope + pltpu.roll(k_rope, MLA_ROPE, 1)
              + pltpu.roll(k_rope, 2 * MLA_ROPE, 1) + pltpu.roll(k_rope, 3 * MLA_ROPE, 1))
    for pair in range(2):
        kp = jnp.concatenate([kv[:, pair * LANES:(pair + 1) * LANES], k_rope], axis=-1)
        mkt_o[pair * GROUP:(pair + 1) * GROUP, :] = kp.T.astype(BF16)
    mv_o[...] = kv[:, GROUP:].astype(BF16)

    pr = proj(C_RET, C_RVG)

    def rotary(first, second):
        return jnp.concatenate([first * cos_r - second * sin_r, second * cos_r + first * sin_r],
                               axis=-1)

    rq_o[...] = rotary(pr[:, 0:LANES], pr[:, LANES:GROUP])
    rk_o[...] = rotary(pr[:, GROUP:GROUP + LANES], pr[:, GROUP + LANES:2 * GROUP])
    pv = proj(C_RVG, C_SB)
    rv_o[...] = pv[:, :GROUP].astype(BF16)
    rg_o[...] = pv[:, GROUP:]

    ps = proj(C_SB, C_END)
    store_heads(sq_o, ps[:, 0:GROUP])
    skt = ps[:, GROUP:2 * GROUP].T.astype(BF16)
    for c in range(tm // T_STICK):
        skt_o[c] = skt[:, c * T_STICK:(c + 1) * T_STICK]
    sv_o[...] = ps[:, 2 * GROUP:].astype(BF16)


def _projection(x, tab, layer, params):
    b, s, d = x.shape
    tm = min(T_KEY, s)
    nk = s // tm
    row = lambda w: pl.BlockSpec((None, tm, w), lambda i, j: (i, j, 0))
    heads = pl.BlockSpec((None, N_HEADS, tm, GROUP), lambda i, j: (i, 0, j, 0))
    keyt = lambda r: pl.BlockSpec((None, None, r, tm), lambda i, j: (i, j, 0, 0))
    bf = lambda w: jax.ShapeDtypeStruct((b, s, w), BF16)
    f32 = lambda w: jax.ShapeDtypeStruct((b, s, w), F32)
    q4 = jax.ShapeDtypeStruct((b, N_HEADS, s, GROUP), BF16)
    kt = lambda r: jax.ShapeDtypeStruct((b, nk, r, tm), BF16)
    n_st = tm // T_STICK
    skt_shape = jax.ShapeDtypeStruct((b, nk * n_st, GROUP, T_STICK), BF16)
    skt_spec = pl.BlockSpec((None, n_st, GROUP, T_STICK), lambda i, j: (i, j, 0, 0))
    out_shape = [q4, kt(2 * GROUP), bf(GROUP),
                 q4, kt(2 * GROUP), bf(GROUP),
                 f32(GROUP), f32(GROUP), bf(GROUP), f32(GROUP),
                 q4, skt_shape, bf(GROUP)]
    out_specs = [heads, keyt(2 * GROUP), row(GROUP),
                 heads, keyt(2 * GROUP), row(GROUP),
                 row(GROUP), row(GROUP), row(GROUP), row(GROUP),
                 heads, skt_spec, row(GROUP)]
    return pl.pallas_call(
        functools.partial(_proj_kernel, mla_scale=MLA_QK ** -0.5 * LOG2E),
        out_shape=out_shape,
        grid=(b, nk),
        in_specs=[row(d), row(4 * LANES)] + [_layer_spec(a, layer) for a in params],
        out_specs=out_specs,
        scratch_shapes=[pltpu.VMEM((1, LANES), F32)],
        compiler_params=pltpu.CompilerParams(
            dimension_semantics=("parallel", "arbitrary"),
            vmem_limit_bytes=VMEM_LIMIT_BYTES),
        name="in_projection",
    )(x, tab, *params)


def _softmax_attn_kernel(q_ref, kt_ref, v_ref, gain_ref, o_ref, acc_ref, m_ref, l_ref,
                         *, chunk_causal, tq, tk):
    i = pl.program_id(1)
    groups = kt_ref.shape[1] // GROUP
    hpg = N_HEADS // groups

    acc_ref[...] = jnp.zeros_like(acc_ref)
    m_ref[...] = jnp.full_like(m_ref, NEG_BIG)
    l_ref[...] = jnp.zeros_like(l_ref)

    def block(j, masked):
        kt = kt_ref[j]
        if masked:
            qpos = i * tq + lax.broadcasted_iota(jnp.int32, (tq, tk), 0)
            kpos = j * tk + lax.broadcasted_iota(jnp.int32, (tq, tk), 1)
            if chunk_causal:
                visible = (kpos >> CHUNK_SHIFT) <= (qpos >> CHUNK_SHIFT)
            else:
                visible = kpos <= qpos
        s_parts = [_dot(q_ref[g * hpg:(g + 1) * hpg].reshape(hpg * tq, GROUP),
                        kt[g * GROUP:(g + 1) * GROUP]) for g in range(groups)]
        p_rows, a_rows = [], []
        for h in range(N_HEADS):
            hr = slice(h * tq, (h + 1) * tq)
            s = s_parts[h // hpg][(h % hpg) * tq:(h % hpg + 1) * tq]
            if masked:
                s = jnp.where(visible, s, NEG_BIG)
            m_cur = jnp.broadcast_to(jnp.max(s, axis=-1, keepdims=True), (tq, LANES))
            m_prev = m_ref[hr]
            m_new = jnp.maximum(m_prev, m_cur)
            alpha = jnp.exp2(m_prev - m_new)
            l_new = alpha * l_ref[hr]
            ps = []
            for c in range(tk // LANES):
                pc = jnp.exp2(s[:, c * LANES:(c + 1) * LANES] - m_new)
                l_new = l_new + pc
                ps.append(pc.astype(BF16))
            l_ref[hr] = l_new
            m_ref[hr] = m_new
            p_rows.append(jnp.concatenate(ps, axis=-1))
            a_rows.append(jnp.concatenate([alpha, alpha], axis=-1))
        start = pl.multiple_of(j * tk, tk)
        pv = _dot(jnp.concatenate(p_rows, axis=0), v_ref[pl.ds(start, tk), :])
        acc_ref[...] = acc_ref[...] * jnp.concatenate(a_rows, axis=0) + pv

    n_full = (i * tq) // tk

    def body(j, carry):
        block(j, False)
        return carry

    lax.fori_loop(0, n_full, body, 0)
    block(n_full, True)

    inv_l = 1.0 / jnp.sum(l_ref[...], axis=-1, keepdims=True)
    o_all = acc_ref[...] * inv_l
    head_v = lax.broadcasted_iota(jnp.int32, (1, GROUP), 1) >> 6
    o = _head_select(head_v, [o_all[h * tq:(h + 1) * tq] for h in range(N_HEADS)])
    o_ref[...] = _rms(o, gain_ref[...]).astype(o_ref.dtype)


def _softmax_attention(q, kt, v, gains, layer, group, *, chunk_causal, name):
    b, _, s, _ = q.shape
    nk, krows, tk = kt.shape[1:]
    tq = min(TQ_ATTN, s)
    rows = N_HEADS * tq
    return pl.pallas_call(
        functools.partial(_softmax_attn_kernel, chunk_causal=chunk_causal, tq=tq, tk=tk),
        out_shape=jax.ShapeDtypeStruct((b, s, GROUP), BF16),
        grid=(b, s // tq),
        in_specs=[pl.BlockSpec((None, N_HEADS, tq, GROUP), lambda i, j: (i, 0, j, 0)),
                  pl.BlockSpec((None, nk, krows, tk), lambda i, j: (i, 0, 0, 0)),
                  pl.BlockSpec((None, s, GROUP), lambda i, j: (i, 0, 0)),
                  _gain_spec(layer, group)],
        out_specs=pl.BlockSpec((None, tq, GROUP), lambda i, j: (i, j, 0)),
        scratch_shapes=[pltpu.VMEM((rows, GROUP), F32),
                        pltpu.VMEM((rows, LANES), F32),
                        pltpu.VMEM((rows, LANES), F32)],
        compiler_params=pltpu.CompilerParams(
            dimension_semantics=("parallel", "parallel"),
            vmem_limit_bytes=VMEM_LIMIT_BYTES),
        name=name,
    )(q, kt, v, gains)


def _stick_kernel(q_ref, kt_ref, v_ref, gain_ref, o_ref, acc_ref, r_ref, *, t):
    i = pl.program_id(1)
    rows = N_HEADS * t
    q = q_ref[...].reshape(rows, GROUP)
    rj = lax.broadcasted_iota(jnp.int32, (t, t), 0)
    cs = lax.broadcasted_iota(jnp.int32, (t, t), 1)
    upper = jnp.where(rj > cs, 1.0, 0.0).astype(BF16)
    upper2 = jnp.concatenate([upper, upper], axis=0)
    visible = cs < rj

    def block(j, diagonal):
        z = _dot(q, kt_ref[j])
        leave = jnp.maximum(z, 0.0) + jnp.log(1.0 + jnp.exp(-jnp.abs(z)))
        if diagonal:
            leave = _per_head(lambda h, x: jnp.where(visible, x, 0.0), leave, t)
        hi = leave.astype(BF16)
        lo = (leave - hi.astype(F32)).astype(BF16)
        later = _dot(jnp.concatenate([hi, lo], axis=-1), upper2)
        row_sum = jnp.broadcast_to(jnp.sum(leave, axis=-1, keepdims=True), (rows, LANES))
        if diagonal:
            r_new = row_sum
        else:
            r_prev = r_ref[...]
            later = later + jnp.concatenate([r_prev] * (t // LANES), axis=-1)
            r_new = r_prev + row_sum
        w = jnp.exp(z - leave - later)
        if diagonal:
            w = _per_head(lambda h, x: jnp.where(visible, x, 0.0), w, t)
        r_ref[...] = r_new
        start = pl.multiple_of(j * t, t)
        pv = _dot(w.astype(BF16), v_ref[pl.ds(start, t), :])
        if diagonal:
            acc_ref[...] = pv
        else:
            acc_ref[...] += pv
        return jnp.min(r_new)

    r_min = block(i, True)

    def cond(carry):
        j, r_min = carry
        return jnp.logical_and(j >= 0, r_min < STICK_EXIT)

    def body(carry):
        j, _ = carry
        return j - 1, block(j, False)

    lax.while_loop(cond, body, (i - 1, r_min))
    acc = acc_ref[...]
    head_v = lax.broadcasted_iota(jnp.int32, (1, GROUP), 1) >> 6
    o = _head_select(head_v, [acc[h * t:(h + 1) * t] for h in range(N_HEADS)])
    o_ref[...] = _rms(o, gain_ref[...]).astype(o_ref.dtype)


def _stick_attention(q, kt, v, gains, layer, group):
    b, _, s, _ = q.shape
    nk, krows, tk = kt.shape[1:]
    tq = tk
    rows = N_HEADS * tq
    return pl.pallas_call(
        functools.partial(_stick_kernel, t=tk),
        out_shape=jax.ShapeDtypeStruct((b, s, GROUP), BF16),
        grid=(b, s // tq),
        in_specs=[pl.BlockSpec((None, N_HEADS, tq, GROUP), lambda i, j: (i, 0, j, 0)),
                  pl.BlockSpec((None, nk, krows, tk), lambda i, j: (i, 0, 0, 0)),
                  pl.BlockSpec((None, s, GROUP), lambda i, j: (i, 0, 0)),
                  _gain_spec(layer, group)],
        out_specs=pl.BlockSpec((None, tq, GROUP), lambda i, j: (i, j, 0)),
        scratch_shapes=[pltpu.VMEM((rows, GROUP), F32),
                        pltpu.VMEM((rows, LANES), F32)],
        compiler_params=pltpu.CompilerParams(
            dimension_semantics=("parallel", "parallel"),
            vmem_limit_bytes=VMEM_LIMIT_BYTES),
        name="stick_attention",
    )(q, kt, v, gains)


def _retention_kernel(q_ref, k_ref, v_ref, g_ref, d_intra_ref, d_q_ref, d_kt_ref, d_state_ref,
                      gain_ref, o_ref, state_ref, *, t):
    @pl.when(pl.program_id(1) == 0)
    def _():
        state_ref[...] = jnp.zeros_like(state_ref)

    q = q_ref[...]
    v = v_ref[...]
    kt = k_ref[...].T
    head = lax.broadcasted_iota(jnp.int32, (1, GROUP), 1) >> 6
    head_qk = (lax.broadcasted_iota(jnp.int32, (1, GROUP), 1) & (LANES - 1)) >> 5
    head_qk_r = (lax.broadcasted_iota(jnp.int32, (GROUP, 1), 0) & (LANES - 1)) >> 5
    same_head = head_qk_r == head

    q_b = q.astype(BF16)
    zq = jnp.zeros_like(q_b)
    q4 = jnp.concatenate([jnp.where(head_qk == h, q_b, zq) for h in range(N_HEADS)], axis=0)
    w = (_dot(q4, kt.astype(BF16)) * d_intra_ref[...]).astype(BF16)
    wv = _dot(w, v)
    out = _head_select(head, [wv[h * t:(h + 1) * t] for h in range(N_HEADS)])

    state = state_ref[...]
    out = out + _dot((q * d_q_ref[...]).astype(BF16), state.astype(BF16))
    kv = _dot((kt * d_kt_ref[...]).astype(BF16), v)
    state_ref[...] = state * d_state_ref[...] + jnp.where(same_head, kv, 0.0)

    head_r = lax.broadcasted_iota(jnp.int32, (GROUP, 1), 0) >> 6
    avg = jnp.where(head_r == head, 1.0 / HEAD_DIM, 0.0).astype(BF16)
    avg3 = jnp.concatenate([avg, avg, avg], axis=0)

    def head_mean(x):
        hi = x.astype(BF16)
        rest = x - hi.astype(F32)
        mid = rest.astype(BF16)
        lo = (rest - mid.astype(F32)).astype(BF16)
        return _dot(jnp.concatenate([hi, mid, lo], axis=-1), avg3)

    mu = head_mean(out)
    cen = out - mu
    var = head_mean(cen * cen)
    y = cen * lax.rsqrt(var + EPS) * gain_ref[...]
    g = g_ref[...]
    o_ref[...] = (y * (g * (1.0 / (1.0 + jnp.exp(-g))))).astype(o_ref.dtype)


def _retention_decay_tables(t):
    log_gamma = jnp.log1p(-jnp.power(2.0, -RET_DECAY_OFFSET - jnp.arange(N_HEADS, dtype=F32)))
    lg_v = jnp.repeat(log_gamma, HEAD_DIM)
    lg_qk = jnp.tile(jnp.repeat(log_gamma, HEAD_DIM // 2), 2)
    r = jnp.arange(t, dtype=F32)
    dist = jnp.abs(r[:, None] - r[None, :])
    reach = (jnp.arange(t)[None, :] // CHUNK) <= (jnp.arange(t)[:, None] // CHUNK)
    d_intra = jnp.where(reach[None], jnp.exp(log_gamma[:, None, None] * dist[None]), 0.0)
    d_q = jnp.exp(lg_qk[None, :] * (r[:, None] + 1.0))
    d_kt = jnp.exp(lg_qk[:, None] * (float(t - 1) - r)[None, :])
    d_state = jnp.exp(lg_v * float(t))[None, :]
    return d_intra.reshape(N_HEADS * t, t), d_q, d_kt, d_state


def _retention(q, k, v, g, tables, gains, layer, group):
    b, s, _ = q.shape
    t = tables[1].shape[0]
    row = pl.BlockSpec((None, t, GROUP), lambda i, j: (i, j, 0))
    const = lambda a: pl.BlockSpec(a.shape, lambda i, j: (0, 0))
    return pl.pallas_call(
        functools.partial(_retention_kernel, t=t),
        out_shape=jax.ShapeDtypeStruct((b, s, GROUP), BF16),
        grid=(b, s // t),
        in_specs=[row, row, row, row] + [const(a) for a in tables] + [_gain_spec(layer, group)],
        out_specs=row,
        scratch_shapes=[pltpu.VMEM((GROUP, GROUP), F32)],
        compiler_params=pltpu.CompilerParams(
            dimension_semantics=("parallel", "arbitrary"),
            vmem_limit_bytes=VMEM_LIMIT_BYTES),
        name="retention",
    )(q, k, v, g, *tables, gains)


def _post_kernel(x_ref, a_ref, b_ref, c_ref, d_ref, wo_ref, gpost_ref, gpre_ref,
                 wup_ref, wdn_ref, gfpost_ref, o_ref, *, f_chunk):
    mixed = jnp.concatenate([a_ref[...], b_ref[...], c_ref[...], d_ref[...]], axis=-1)
    x1 = x_ref[...] + _rms(_dot(mixed, wo_ref[...]), gpost_ref[...])
    h = _rms(x1, gpre_ref[...]).astype(BF16)
    y = None
    for c in range(wup_ref.shape[1] // f_chunk):
        u = jnp.maximum(_dot(h, wup_ref[:, c * f_chunk:(c + 1) * f_chunk]), 0.0)
        d = _dot((u * u).astype(BF16), wdn_ref[c * f_chunk:(c + 1) * f_chunk, :])
        y = d if y is None else y + d
    o_ref[...] = x1 + _rms(y, gfpost_ref[...])


def _post(x, a, bb, c, dd, layer, params):
    b, s, d = x.shape
    tm = min(TM_POST, s)
    row = lambda w: pl.BlockSpec((None, tm, w), lambda i, j: (i, j, 0))
    return pl.pallas_call(
        functools.partial(_post_kernel, f_chunk=1024),
        out_shape=jax.ShapeDtypeStruct((b, s, d), F32),
        grid=(b, s // tm),
        in_specs=([row(d), row(GROUP), row(GROUP), row(GROUP), row(GROUP)]
                  + [_layer_spec(p, layer) for p in params]),
        out_specs=row(d),
        compiler_params=pltpu.CompilerParams(
            dimension_semantics=("parallel", "parallel"),
            vmem_limit_bytes=VMEM_LIMIT_BYTES),
        name="out_proj_mlp",
    )(x, a, bb, c, dd, *params)


def _rot_perm(width, period):
    j = np.arange(width)
    return (j // period) * period + (j % period + period // 2) % period


def _projection_weights(w_in, b_forget, w_q_up, w_kv_up):
    depth, d, _ = w_in.shape
    sizes = [GROUP, GROUP, GROUP, N_HEADS, 256, 128, MLA_ROPE] + [GROUP] * 7
    offs = np.concatenate([[0], np.cumsum(sizes)])
    (fq, fk, fv, ff, cq, ckv, kr, rq, rk, rv, rg, sq, sk, sv) = [
        w_in[..., int(offs[n]):int(offs[n + 1])] for n in range(len(sizes))]
    scale = HEAD_DIM ** -0.5
    col = np.arange(GROUP)
    halves = np.concatenate([col[col % HEAD_DIM < HEAD_DIM // 2],
                             col[col % HEAD_DIM >= HEAD_DIM // 2]])
    misc = jnp.concatenate(
        [kr, kr[..., _rot_perm(MLA_ROPE, MLA_ROPE)], ff,
         jnp.zeros((depth, d, LANES - 2 * MLA_ROPE - N_HEADS), F32)], axis=-1)
    w_cat = jnp.concatenate(
        [fq * scale, fk, fv, cq, ckv, misc, rq[..., halves], rk[..., halves] * scale, rv, rg,
         sq * scale, sk, sv], axis=-1).astype(BF16)
    bf_pad = jnp.concatenate(
        [b_forget, jnp.zeros((depth, LANES - N_HEADS), F32)], axis=-1)[:, None, :]
    wq4 = w_q_up.reshape(depth, -1, N_HEADS, MLA_QK)
    q_nope = wq4[..., :HEAD_DIM].reshape(depth, -1, GROUP)
    q_rope = wq4[..., HEAD_DIM:].reshape(depth, -1, N_HEADS * MLA_ROPE)
    w_q = jnp.concatenate(
        [q_nope, q_rope, q_rope[..., _rot_perm(N_HEADS * MLA_ROPE, MLA_ROPE)]],
        axis=-1).astype(BF16)
    wkv4 = w_kv_up.reshape(depth, -1, N_HEADS, 2 * HEAD_DIM)
    w_kv = jnp.concatenate([wkv4[..., :HEAD_DIM].reshape(depth, -1, GROUP),
                            wkv4[..., HEAD_DIM:].reshape(depth, -1, GROUP)],
                           axis=-1).astype(BF16)
    return w_cat, bf_pad, w_q, w_kv


def kernel(x, positions, g_mix_pre, w_in, b_forget, g_q_lora, w_q_up, g_kv_lora, w_kv_up,
           g_mix_out, w_out, g_mix_post, g_ffn_pre, w_ffn_up, w_ffn_down, g_ffn_post):
    depth = w_in.shape[0]
    vec = lambda g: g[:, None, :]
    w_cat, bf_pad, w_q, w_kv = _projection_weights(w_in, b_forget, w_q_up, w_kv_up)
    proj_params = (vec(g_mix_pre), w_cat, bf_pad, vec(g_q_lora), w_q, vec(g_kv_lora), w_kv)
    post_params = (w_out.astype(BF16), vec(g_mix_post), vec(g_ffn_pre),
                   w_ffn_up.astype(BF16), w_ffn_down.astype(BF16), vec(g_ffn_post))
    gains = g_mix_out.reshape(depth, 4, 1, GROUP)
    tab = _rope_tables(positions)
    ret_tables = _retention_decay_tables(min(T_RET, x.shape[1]))
    for layer in range(depth):
        (fq, fkt, fv, mq, mkt, mv, rq, rk, rv, rg, sq, skt, sv) = _projection(
            x, tab, layer, proj_params)
        out_a = _softmax_attention(fq, fkt, fv, gains, layer, 0,
                                   chunk_causal=False, name="fox_attention")
        out_b = _softmax_attention(mq, mkt, mv, gains, layer, 1,
                                   chunk_causal=True, name="mla_attention")
        out_c = _retention(rq, rk, rv, rg, ret_tables, gains, layer, 2)
        out_d = _stick_attention(sq, skt, sv, gains, layer, 3)
        x = _post(x, out_a, out_b, out_c, out_d, layer, post_params)
    return x
```

```python
import functools

import numpy as np
import jax
import jax.numpy as jnp
from jax import lax
from jax.experimental import pallas as pl
from jax.experimental.pallas import tpu as pltpu

F32 = jnp.float32
BF16 = jnp.bfloat16

N_HEADS = 4
HEAD_DIM = 64
GROUP = N_HEADS * HEAD_DIM
CHUNK = 64
CHUNK_SHIFT = 6
MLA_ROPE = 32
MLA_QK = HEAD_DIM + MLA_ROPE
ROPE_BASE = 10000.0
EPS = 1e-6
RET_DECAY_OFFSET = 5.0
NEG_BIG = -1e30
LOG2E = 1.4426950408889634
STICK_EXIT = 128.0

LANES = 128
VMEM_LIMIT_BYTES = 56 * 1024 * 1024

T_KEY = 512
TM_POST = 512
TQ_ATTN = 512
T_STICK = 256
T_RET = 256
TM_TAB = 1024

C_FOX = 0
C_CQ = 768
C_CKV = 1024
C_MISC = 1152
C_RET = 1280
C_RVG = 1792
C_SB = 2304
C_END = 3072
MISC_ROT = 32
MISC_FF = 64


def _rms(x, g):
    return x * lax.rsqrt(jnp.mean(x * x, axis=-1, keepdims=True) + EPS) * g


def _log_sigmoid(x):
    return jnp.minimum(x, 0.0) - jnp.log1p(jnp.exp(-jnp.abs(x)))


def _dot(a, b):
    return jnp.dot(a, b, preferred_element_type=F32)


def _head_select(lane_head, per_head):
    out = per_head[N_HEADS - 1]
    for h in range(N_HEADS - 2, -1, -1):
        out = jnp.where(lane_head == h, per_head[h], out)
    return out


def _layer_spec(a, layer):
    zeros = (0,) * (a.ndim - 1)
    return pl.BlockSpec((None,) + a.shape[1:], lambda i, j: (layer,) + zeros)


def _gain_spec(layer, group):
    return pl.BlockSpec((None, None, 1, GROUP), lambda i, j: (layer, group, 0, 0))


def _per_head(fn, x, tq):
    return jnp.concatenate([fn(h, x[h * tq:(h + 1) * tq]) for h in range(N_HEADS)], axis=0)


def _table_kernel(pos_ref, invf_ref, sign_ref, tab_ref):
    half_r, half_m = HEAD_DIM // 2, MLA_ROPE // 2
    ang = pos_ref[...].astype(F32) * invf_ref[...]
    lane = lax.broadcasted_iota(jnp.int32, (1, LANES), 1)
    is_r = lane < half_r
    is_m = (lane >= half_r) & (lane < half_r + half_m)

    def spread_r(x):
        x = jnp.where(is_r, x, 0.0)
        x = x + pltpu.roll(x, half_r, 1)
        return x + pltpu.roll(x, 2 * half_r, 1)

    def spread_m(x):
        x = pltpu.roll(jnp.where(is_m, x, 0.0), LANES - half_r, 1)
        x = x + pltpu.roll(x, half_m, 1)
        x = x + pltpu.roll(x, 2 * half_m, 1)
        return x + pltpu.roll(x, 4 * half_m, 1)

    c = jnp.cos(ang)
    s = jnp.sin(ang)
    tab_ref[...] = jnp.concatenate(
        [spread_r(c), spread_r(s), spread_m(c), spread_m(s) * sign_ref[...]], axis=-1)


def _rope_tables(positions):
    b, s = positions.shape
    half_r, half_m = HEAD_DIM // 2, MLA_ROPE // 2
    invf_r = ROPE_BASE ** (-jnp.arange(half_r, dtype=F32) / half_r)
    invf_m = ROPE_BASE ** (-jnp.arange(half_m, dtype=F32) / half_m)
    invf = jnp.concatenate([invf_r, invf_m, jnp.zeros((LANES - half_r - half_m,), F32)])[None, :]
    lane = np.arange(LANES)
    sign = jnp.asarray(np.where(lane % MLA_ROPE < half_m, -1.0, 1.0), F32)[None, :]
    tm = min(TM_TAB, s)
    return pl.pallas_call(
        _table_kernel,
        out_shape=jax.ShapeDtypeStruct((b, s, 4 * LANES), F32),
        grid=(b, s // tm),
        in_specs=[pl.BlockSpec((None, tm, 1), lambda i, j: (i, j, 0)),
                  pl.BlockSpec((1, LANES), lambda i, j: (0, 0)),
                  pl.BlockSpec((1, LANES), lambda i, j: (0, 0))],
        out_specs=pl.BlockSpec((None, tm, 4 * LANES), lambda i, j: (i, j, 0)),
        compiler_params=pltpu.CompilerParams(
            dimension_semantics=("parallel", "parallel")),
        name="rope_tables",
    )(positions[:, :, None], invf, sign)


def _proj_kernel(x_ref, tab_ref, g_ref, w_ref, bf_ref, gq_ref, wq_ref, gkv_ref, wkv_ref,
                 fq_o, fkt_o, fv_o, mq_o, mkt_o, mv_o,
                 rq_o, rk_o, rv_o, rg_o, sq_o, skt_o, sv_o, carry_ref, *, mla_scale):
    tm = x_ref.shape[0]
    h = _rms(x_ref[...], g_ref[...]).astype(BF16)

    def proj(a, b):
        return _dot(h, w_ref[:, a:b])

    tab = tab_ref[...]
    cos_r, sin_r = tab[:, 0:LANES], tab[:, LANES:2 * LANES]
    cos_m, sin_m = tab[:, 2 * LANES:3 * LANES], tab[:, 3 * LANES:]
    head64 = lax.broadcasted_iota(jnp.int32, (1, GROUP), 1) >> 6
    lane = lax.broadcasted_iota(jnp.int32, (1, LANES), 1)
    misc = proj(C_MISC, C_RET)

    def store_heads(o_ref, q):
        qb = q.astype(BF16)
        zero = jnp.zeros_like(qb)
        for hh in range(N_HEADS):
            o_ref[hh] = jnp.where(head64 == hh, qb, zero)

    @pl.when(pl.program_id(1) == 0)
    def _():
        carry_ref[...] = jnp.zeros_like(carry_ref)

    ls = _log_sigmoid(pltpu.roll(misc, LANES - MISC_FF, 1) + bf_ref[...])
    row = lax.broadcasted_iota(jnp.int32, (tm, tm), 0)
    col = lax.broadcasted_iota(jnp.int32, (tm, tm), 1)
    tri = jnp.where(col <= row, 1.0, 0.0).astype(BF16)
    ls_hi = ls.astype(BF16)
    ls_rest = ls - ls_hi.astype(F32)
    ls_mid = ls_rest.astype(BF16)
    ls_lo = (ls_rest - ls_mid.astype(F32)).astype(BF16)
    parts = _dot(tri, jnp.concatenate([ls_hi, ls_mid, ls_lo], axis=-1))
    cum = (parts[:, :LANES] + parts[:, LANES:2 * LANES] + parts[:, 2 * LANES:]) + carry_ref[...]
    carry_ref[...] = cum[tm - 1:tm, :]
    c_hi = (cum * LOG2E).astype(BF16).astype(F32)
    c_rest = cum * LOG2E - c_hi
    c_mid = c_rest.astype(BF16).astype(F32)
    c_lo = (c_rest - c_mid).astype(BF16).astype(F32)
    term_of_lane = lane & 3
    spread = jnp.zeros((tm, LANES), F32)
    for hh in range(N_HEADS):
        for t, term in enumerate((c_hi, c_mid, c_lo)):
            val = jnp.broadcast_to(term[:, hh:hh + 1], (tm, LANES))
            spread = jnp.where(((lane >> 3) == hh) & (term_of_lane == t), val, spread)
    used = (lane < 8 * N_HEADS) & (term_of_lane < 3)
    first = used & ((lane & 4) == 0)
    second = used & ((lane & 4) != 0)
    gate_q = jnp.where(first, spread, jnp.where(second, 1.0, 0.0))
    gate_k = jnp.where(second, -spread, jnp.where(first, 1.0, 0.0))

    p = proj(C_FOX, C_CQ)
    fq = p[:, 0:GROUP] * LOG2E
    for hh in range(N_HEADS):
        pair = hh // 2
        nope = jnp.where((lane >> 6) == hh % 2, fq[:, pair * LANES:(pair + 1) * LANES], 0.0)
        gate = jnp.where((lane >> 3) == hh, gate_q, 0.0)
        fq_o[hh] = jnp.concatenate([nope, gate], axis=-1).astype(BF16)
    for pair in range(2):
        kp = jnp.concatenate([p[:, GROUP + pair * LANES:GROUP + (pair + 1) * LANES], gate_k],
                             axis=-1)
        fkt_o[pair * GROUP:(pair + 1) * GROUP, :] = kp.T.astype(BF16)
    fv_o[...] = p[:, 2 * GROUP:].astype(BF16)

    hq = _rms(proj(C_CQ, C_CKV), gq_ref[...]).astype(BF16)
    qm = _dot(hq, wq_ref[...]) * mla_scale
    q_nope = qm[:, :GROUP].astype(BF16)
    q_rope = (qm[:, GROUP:GROUP + LANES] * cos_m + qm[:, GROUP + LANES:] * sin_m).astype(BF16)
    zero = jnp.zeros_like(q_rope)
    for hh in range(N_HEADS):
        pair = hh // 2
        nope = jnp.where((lane >> 6) == hh % 2, q_nope[:, pair * LANES:(pair + 1) * LANES], zero)
        rope = jnp.where((lane >> 5) == hh, q_rope, zero)
        mq_o[hh] = jnp.concatenate([nope, rope], axis=-1)
    hkv = _rms(proj(C_CKV, C_MISC), gkv_ref[...]).astype(BF16)
    kv = _dot(hkv, wkv_ref[...])
    in_rope = lane < MLA_ROPE
    k_rope = (misc * jnp.where(in_rope, cos_m, 0.0)
              + pltpu.roll(misc, LANES - MISC_ROT, 1) * jnp.where(in_rope, sin_m, 0.0))
    k_rope = (k_rope + pltpu.roll(k_rope, MLA_ROPE, 1)
              + pltpu.roll(k_rope, 2 * MLA_ROPE, 1) + pltpu.roll(k_rope, 3 * MLA_ROPE, 1))
    for pair in range(2):
        kp = jnp.concatenate([kv[:, pair * LANES:(pair + 1) * LANES], k_rope], axis=-1)
        mkt_o[pair * GROUP:(pair + 1) * GROUP, :] = kp.T.astype(BF16)
    mv_o[...] = kv[:, GROUP:].astype(BF16)

    pr = proj(C_RET, C_RVG)

    def rotary(first, second):
        return jnp.concatenate([first * cos_r - second * sin_r, second * cos_r + first * sin_r],
                               axis=-1)

    rq_o[...] = rotary(pr[:, 0:LANES], pr[:, LANES:GROUP])
    rk_o[...] = rotary(pr[:, GROUP:GROUP + LANES], pr[:, GROUP + LANES:2 * GROUP])
    pv = proj(C_RVG, C_SB)
    rv_o[...] = pv[:, :GROUP].astype(BF16)
    rg_o[...] = pv[:, GROUP:]

    ps = proj(C_SB, C_END)
    store_heads(sq_o, ps[:, 0:GROUP])
    skt = ps[:, GROUP:2 * GROUP].T.astype(BF16)
    for c in range(tm // T_STICK):
        skt_o[c] = skt[:, c * T_STICK:(c + 1) * T_STICK]
    sv_o[...] = ps[:, 2 * GROUP:].astype(BF16)


def _projection(x, tab, layer, params):
    b, s, d = x.shape
    tm = min(T_KEY, s)
    nk = s // tm
    row = lambda w: pl.BlockSpec((None, tm, w), lambda i, j: (i, j, 0))
    heads = pl.BlockSpec((None, N_HEADS, tm, GROUP), lambda i, j: (i, 0, j, 0))
    keyt = lambda r: pl.BlockSpec((None, None, r, tm), lambda i, j: (i, j, 0, 0))
    bf = lambda w: jax.ShapeDtypeStruct((b, s, w), BF16)
    f32 = lambda w: jax.ShapeDtypeStruct((b, s, w), F32)
    q4 = jax.ShapeDtypeStruct((b, N_HEADS, s, GROUP), BF16)
    kt = lambda r: jax.ShapeDtypeStruct((b, nk, r, tm), BF16)
    n_st = tm // T_STICK
    skt_shape = jax.ShapeDtypeStruct((b, nk * n_st, GROUP, T_STICK), BF16)
    skt_spec = pl.BlockSpec((None, n_st, GROUP, T_STICK), lambda i, j: (i, j, 0, 0))
    out_shape = [q4, kt(2 * GROUP), bf(GROUP),
                 q4, kt(2 * GROUP), bf(GROUP),
                 f32(GROUP), f32(GROUP), bf(GROUP), f32(GROUP),
                 q4, skt_shape, bf(GROUP)]
    out_specs = [heads, keyt(2 * GROUP), row(GROUP),
                 heads, keyt(2 * GROUP), row(GROUP),
                 row(GROUP), row(GROUP), row(GROUP), row(GROUP),
                 heads, skt_spec, row(GROUP)]
    return pl.pallas_call(
        functools.partial(_proj_kernel, mla_scale=MLA_QK ** -0.5 * LOG2E),
        out_shape=out_shape,
        grid=(b, nk),
        in_specs=[row(d), row(4 * LANES)] + [_layer_spec(a, layer) for a in params],
        out_specs=out_specs,
        scratch_shapes=[pltpu.VMEM((1, LANES), F32)],
        compiler_params=pltpu.CompilerParams(
            dimension_semantics=("parallel", "arbitrary"),
            vmem_limit_bytes=VMEM_LIMIT_BYTES),
        name="in_projection",
    )(x, tab, *params)


def _softmax_attn_kernel(q_ref, kt_ref, v_ref, gain_ref, o_ref, acc_ref, m_ref, l_ref,
                         *, chunk_causal, tq, tk):
    i = pl.program_id(1)
    groups = kt_ref.shape[1] // GROUP
    hpg = N_HEADS // groups

    def block(j, c0, nk, r0, masked, first):
        nr = tq - r0
        kt = kt_ref[j, :, c0:c0 + nk]
        if masked:
            qpos = i * tq + r0 + lax.broadcasted_iota(jnp.int32, (nr, nk), 0)
            kpos = j * tk + c0 + lax.broadcasted_iota(jnp.int32, (nr, nk), 1)
            if chunk_causal:
                visible = (kpos >> CHUNK_SHIFT) <= (qpos >> CHUNK_SHIFT)
            else:
                visible = kpos <= qpos
        s_parts = [_dot(q_ref[g * hpg:(g + 1) * hpg, r0:tq, :].reshape(hpg * nr, GROUP),
                        kt[g * GROUP:(g + 1) * GROUP]) for g in range(groups)]
        p_rows, alphas = [], []
        for h in range(N_HEADS):
            hr = slice(h * tq + r0, (h + 1) * tq)
            s = s_parts[h // hpg][(h % hpg) * nr:(h % hpg + 1) * nr]
            if masked:
                s = jnp.where(visible, s, NEG_BIG)
            m_new = jnp.broadcast_to(jnp.max(s, axis=-1, keepdims=True), (nr, LANES))
            l_new = None
            if not first:
                m_prev = m_ref[hr]
                m_new = jnp.maximum(m_prev, m_new)
                alpha = jnp.exp2(m_prev - m_new)
                alphas.append(jnp.concatenate([alpha, alpha], axis=-1))
                l_new = alpha * l_ref[hr]
            ps = []
            for c in range(nk // LANES):
                pc = jnp.exp2(s[:, c * LANES:(c + 1) * LANES] - m_new)
                l_new = pc if l_new is None else l_new + pc
                ps.append(pc.astype(BF16))
            l_ref[hr] = l_new
            m_ref[hr] = m_new
            p_rows.append(jnp.concatenate(ps, axis=-1))
        start = pl.multiple_of(j * tk + c0, nk)
        pv = _dot(jnp.concatenate(p_rows, axis=0), v_ref[pl.ds(start, nk), :])
        if first:
            acc_ref[...] = pv
        elif r0 == 0:
            acc_ref[...] = acc_ref[...] * jnp.concatenate(alphas, axis=0) + pv
        else:
            for h in range(N_HEADS):
                hr = slice(h * tq + r0, (h + 1) * tq)
                acc_ref[hr] = acc_ref[hr] * alphas[h] + pv[h * nr:(h + 1) * nr]

    n_full = (i * tq) // tk
    half = tk // 2
    block(n_full, 0, half, 0, True, True)
    block(n_full, half, half, tq // 2, True, False)

    def body(j, carry):
        block(j, 0, tk, 0, False, False)
        return carry

    lax.fori_loop(0, n_full, body, 0)

    inv_l = 1.0 / jnp.sum(l_ref[...], axis=-1, keepdims=True)
    o_all = acc_ref[...] * inv_l
    head_v = lax.broadcasted_iota(jnp.int32, (1, GROUP), 1) >> 6
    o = _head_select(head_v, [o_all[h * tq:(h + 1) * tq] for h in range(N_HEADS)])
    o_ref[...] = _rms(o, gain_ref[...]).astype(o_ref.dtype)


def _softmax_attention(q, kt, v, gains, layer, group, *, chunk_causal, name):
    b, _, s, _ = q.shape
    nk, krows, tk = kt.shape[1:]
    tq = min(TQ_ATTN, s)
    assert tq == tk and tk % (2 * CHUNK) == 0, (tq, tk)
    rows = N_HEADS * tq
    return pl.pallas_call(
        functools.partial(_softmax_attn_kernel, chunk_causal=chunk_causal, tq=tq, tk=tk),
        out_shape=jax.ShapeDtypeStruct((b, s, GROUP), BF16),
        grid=(b, s // tq),
        in_specs=[pl.BlockSpec((None, N_HEADS, tq, GROUP), lambda i, j: (i, 0, j, 0)),
                  pl.BlockSpec((None, nk, krows, tk), lambda i, j: (i, 0, 0, 0)),
                  pl.BlockSpec((None, s, GROUP), lambda i, j: (i, 0, 0)),
                  _gain_spec(layer, group)],
        out_specs=pl.BlockSpec((None, tq, GROUP), lambda i, j: (i, j, 0)),
        scratch_shapes=[pltpu.VMEM((rows, GROUP), F32),
                        pltpu.VMEM((rows, LANES), F32),
                        pltpu.VMEM((rows, LANES), F32)],
        compiler_params=pltpu.CompilerParams(
            dimension_semantics=("parallel", "parallel"),
            vmem_limit_bytes=VMEM_LIMIT_BYTES),
        name=name,
    )(q, kt, v, gains)


def _stick_kernel(q_ref, kt_ref, v_ref, gain_ref, o_ref, acc_ref, r_ref, *, t):
    i = pl.program_id(1)
    rows = N_HEADS * t
    q = q_ref[...].reshape(rows, GROUP)
    rj = lax.broadcasted_iota(jnp.int32, (t, t), 0)
    cs = lax.broadcasted_iota(jnp.int32, (t, t), 1)
    upper = jnp.where(rj > cs, 1.0, 0.0).astype(BF16)
    upper2 = jnp.concatenate([upper, upper], axis=0)
    visible = cs < rj

    def block(j, diagonal):
        z = _dot(q, kt_ref[j])
        leave = jnp.maximum(z, 0.0) + jnp.log(1.0 + jnp.exp(-jnp.abs(z)))
        if diagonal:
            leave = _per_head(lambda h, x: jnp.where(visible, x, 0.0), leave, t)
        hi = leave.astype(BF16)
        lo = (leave - hi.astype(F32)).astype(BF16)
        later = _dot(jnp.concatenate([hi, lo], axis=-1), upper2)
        row_sum = jnp.broadcast_to(jnp.sum(leave, axis=-1, keepdims=True), (rows, LANES))
        if diagonal:
            r_new = row_sum
        else:
            r_prev = r_ref[...]
            later = later + jnp.concatenate([r_prev] * (t // LANES), axis=-1)
            r_new = r_prev + row_sum
        w = jnp.exp(z - leave - later)
        if diagonal:
            w = _per_head(lambda h, x: jnp.where(visible, x, 0.0), w, t)
        r_ref[...] = r_new
        start = pl.multiple_of(j * t, t)
        pv = _dot(w.astype(BF16), v_ref[pl.ds(start, t), :])
        if diagonal:
            acc_ref[...] = pv
        else:
            acc_ref[...] += pv
        return jnp.min(r_new)

    r_min = block(i, True)

    def cond(carry):
        j, r_min = carry
        return jnp.logical_and(j >= 0, r_min < STICK_EXIT)

    def body(carry):
        j, _ = carry
        return j - 1, block(j, False)

    lax.while_loop(cond, body, (i - 1, r_min))
    acc = acc_ref[...]
    head_v = lax.broadcasted_iota(jnp.int32, (1, GROUP), 1) >> 6
    o = _head_select(head_v, [acc[h * t:(h + 1) * t] for h in range(N_HEADS)])
    o_ref[...] = _rms(o, gain_ref[...]).astype(o_ref.dtype)


def _stick_attention(q, kt, v, gains, layer, group):
    b, _, s, _ = q.shape
    nk, krows, tk = kt.shape[1:]
    tq = tk
    rows = N_HEADS * tq
    return pl.pallas_call(
        functools.partial(_stick_kernel, t=tk),
        out_shape=jax.ShapeDtypeStruct((b, s, GROUP), BF16),
        grid=(b, s // tq),
        in_specs=[pl.BlockSpec((None, N_HEADS, tq, GROUP), lambda i, j: (i, 0, j, 0)),
                  pl.BlockSpec((None, nk, krows, tk), lambda i, j: (i, 0, 0, 0)),
                  pl.BlockSpec((None, s, GROUP), lambda i, j: (i, 0, 0)),
                  _gain_spec(layer, group)],
        out_specs=pl.BlockSpec((None, tq, GROUP), lambda i, j: (i, j, 0)),
        scratch_shapes=[pltpu.VMEM((rows, GROUP), F32),
                        pltpu.VMEM((rows, LANES), F32)],
        compiler_params=pltpu.CompilerParams(
            dimension_semantics=("parallel", "parallel"),
            vmem_limit_bytes=VMEM_LIMIT_BYTES),
        name="stick_attention",
    )(q, kt, v, gains)


def _retention_kernel(q_ref, k_ref, v_ref, g_ref, d_intra_ref, d_q_ref, d_kt_ref, d_state_ref,
                      gain_ref, o_ref, state_ref, *, t):
    @pl.when(pl.program_id(1) == 0)
    def _():
        state_ref[...] = jnp.zeros_like(state_ref)

    q = q_ref[...]
    v = v_ref[...]
    kt = k_ref[...].T
    head = lax.broadcasted_iota(jnp.int32, (1, GROUP), 1) >> 6
    head_qk = (lax.broadcasted_iota(jnp.int32, (1, GROUP), 1) & (LANES - 1)) >> 5
    head_qk_r = (lax.broadcasted_iota(jnp.int32, (GROUP, 1), 0) & (LANES - 1)) >> 5
    same_head = head_qk_r == head

    q_b = q.astype(BF16)
    zq = jnp.zeros_like(q_b)
    q4 = jnp.concatenate([jnp.where(head_qk == h, q_b, zq) for h in range(N_HEADS)], axis=0)
    w = (_dot(q4, kt.astype(BF16)) * d_intra_ref[...]).astype(BF16)
    wv = _dot(w, v)
    out = _head_select(head, [wv[h * t:(h + 1) * t] for h in range(N_HEADS)])

    state = state_ref[...]
    out = out + _dot((q * d_q_ref[...]).astype(BF16), state.astype(BF16))
    kv = _dot((kt * d_kt_ref[...]).astype(BF16), v)
    state_ref[...] = state * d_state_ref[...] + jnp.where(same_head, kv, 0.0)

    head_r = lax.broadcasted_iota(jnp.int32, (GROUP, 1), 0) >> 6
    avg = jnp.where(head_r == head, 1.0 / HEAD_DIM, 0.0).astype(BF16)
    avg3 = jnp.concatenate([avg, avg, avg], axis=0)

    def head_mean(x):
        hi = x.astype(BF16)
        rest = x - hi.astype(F32)
        mid = rest.astype(BF16)
        lo = (rest - mid.astype(F32)).astype(BF16)
        return _dot(jnp.concatenate([hi, mid, lo], axis=-1), avg3)

    mu = head_mean(out)
    cen = out - mu
    var = head_mean(cen * cen)
    y = cen * lax.rsqrt(var + EPS) * gain_ref[...]
    g = g_ref[...]
    o_ref[...] = (y * (g * (1.0 / (1.0 + jnp.exp(-g))))).astype(o_ref.dtype)


def _retention_decay_tables(t):
    log_gamma = jnp.log1p(-jnp.power(2.0, -RET_DECAY_OFFSET - jnp.arange(N_HEADS, dtype=F32)))
    lg_v = jnp.repeat(log_gamma, HEAD_DIM)
    lg_qk = jnp.tile(jnp.repeat(log_gamma, HEAD_DIM // 2), 2)
    r = jnp.arange(t, dtype=F32)
    dist = jnp.abs(r[:, None] - r[None, :])
    reach = (jnp.arange(t)[None, :] // CHUNK) <= (jnp.arange(t)[:, None] // CHUNK)
    d_intra = jnp.where(reach[None], jnp.exp(log_gamma[:, None, None] * dist[None]), 0.0)
    d_q = jnp.exp(lg_qk[None, :] * (r[:, None] + 1.0))
    d_kt = jnp.exp(lg_qk[:, None] * (float(t - 1) - r)[None, :])
    d_state = jnp.exp(lg_v * float(t))[None, :]
    return d_intra.reshape(N_HEADS * t, t), d_q, d_kt, d_state


def _retention(q, k, v, g, tables, gains, layer, group):
    b, s, _ = q.shape
    t = tables[1].shape[0]
    row = pl.BlockSpec((None, t, GROUP), lambda i, j: (i, j, 0))
    const = lambda a: pl.BlockSpec(a.shape, lambda i, j: (0, 0))
    return pl.pallas_call(
        functools.partial(_retention_kernel, t=t),
        out_shape=jax.ShapeDtypeStruct((b, s, GROUP), BF16),
        grid=(b, s // t),
        in_specs=[row, row, row, row] + [const(a) for a in tables] + [_gain_spec(layer, group)],
        out_specs=row,
        scratch_shapes=[pltpu.VMEM((GROUP, GROUP), F32)],
        compiler_params=pltpu.CompilerParams(
            dimension_semantics=("parallel", "arbitrary"),
            vmem_limit_bytes=VMEM_LIMIT_BYTES),
        name="retention",
    )(q, k, v, g, *tables, gains)


def _post_kernel(x_ref, a_ref, b_ref, c_ref, d_ref, wo_ref, gpost_ref, gpre_ref,
                 wup_ref, wdn_ref, gfpost_ref, o_ref, *, f_chunk):
    mixed = jnp.concatenate([a_ref[...], b_ref[...], c_ref[...], d_ref[...]], axis=-1)
    x1 = x_ref[...] + _rms(_dot(mixed, wo_ref[...]), gpost_ref[...])
    h = _rms(x1, gpre_ref[...]).astype(BF16)
    y = None
    for c in range(wup_ref.shape[1] // f_chunk):
        u = jnp.maximum(_dot(h, wup_ref[:, c * f_chunk:(c + 1) * f_chunk]), 0.0)
        d = _dot((u * u).astype(BF16), wdn_ref[c * f_chunk:(c + 1) * f_chunk, :])
        y = d if y is None else y + d
    o_ref[...] = x1 + _rms(y, gfpost_ref[...])


def _post(x, a, bb, c, dd, layer, params):
    b, s, d = x.shape
    tm = min(TM_POST, s)
    row = lambda w: pl.BlockSpec((None, tm, w), lambda i, j: (i, j, 0))
    return pl.pallas_call(
        functools.partial(_post_kernel, f_chunk=1024),
        out_shape=jax.ShapeDtypeStruct((b, s, d), F32),
        grid=(b, s // tm),
        in_specs=([row(d), row(GROUP), row(GROUP), row(GROUP), row(GROUP)]
                  + [_layer_spec(p, layer) for p in params]),
        out_specs=row(d),
        compiler_params=pltpu.CompilerParams(
            dimension_semantics=("parallel", "parallel"),
            vmem_limit_bytes=VMEM_LIMIT_BYTES),
        name="out_proj_mlp",
    )(x, a, bb, c, dd, *params)


def _rot_perm(width, period):
    j = np.arange(width)
    return (j // period) * period + (j % period + period // 2) % period


def _projection_weights(w_in, b_forget, w_q_up, w_kv_up):
    depth, d, _ = w_in.shape
    sizes = [GROUP, GROUP, GROUP, N_HEADS, 256, 128, MLA_ROPE] + [GROUP] * 7
    offs = np.concatenate([[0], np.cumsum(sizes)])
    (fq, fk, fv, ff, cq, ckv, kr, rq, rk, rv, rg, sq, sk, sv) = [
        w_in[..., int(offs[n]):int(offs[n + 1])] for n in range(len(sizes))]
    scale = HEAD_DIM ** -0.5
    col = np.arange(GROUP)
    halves = np.concatenate([col[col % HEAD_DIM < HEAD_DIM // 2],
                             col[col % HEAD_DIM >= HEAD_DIM // 2]])
    misc = jnp.concatenate(
        [kr, kr[..., _rot_perm(MLA_ROPE, MLA_ROPE)], ff,
         jnp.zeros((depth, d, LANES - 2 * MLA_ROPE - N_HEADS), F32)], axis=-1)
    w_cat = jnp.concatenate(
        [fq * scale, fk, fv, cq, ckv, misc, rq[..., halves], rk[..., halves] * scale, rv, rg,
         sq * scale, sk, sv], axis=-1).astype(BF16)
    bf_pad = jnp.concatenate(
        [b_forget, jnp.zeros((depth, LANES - N_HEADS), F32)], axis=-1)[:, None, :]
    wq4 = w_q_up.reshape(depth, -1, N_HEADS, MLA_QK)
    q_nope = wq4[..., :HEAD_DIM].reshape(depth, -1, GROUP)
    q_rope = wq4[..., HEAD_DIM:].reshape(depth, -1, N_HEADS * MLA_ROPE)
    w_q = jnp.concatenate(
        [q_nope, q_rope, q_rope[..., _rot_perm(N_HEADS * MLA_ROPE, MLA_ROPE)]],
        axis=-1).astype(BF16)
    wkv4 = w_kv_up.reshape(depth, -1, N_HEADS, 2 * HEAD_DIM)
    w_kv = jnp.concatenate([wkv4[..., :HEAD_DIM].reshape(depth, -1, GROUP),
                            wkv4[..., HEAD_DIM:].reshape(depth, -1, GROUP)],
                           axis=-1).astype(BF16)
    return w_cat, bf_pad, w_q, w_kv


def kernel(x, positions, g_mix_pre, w_in, b_forget, g_q_lora, w_q_up, g_kv_lora, w_kv_up,
           g_mix_out, w_out, g_mix_post, g_ffn_pre, w_ffn_up, w_ffn_down, g_ffn_post):
    depth = w_in.shape[0]
    vec = lambda g: g[:, None, :]
    w_cat, bf_pad, w_q, w_kv = _projection_weights(w_in, b_forget, w_q_up, w_kv_up)
    proj_params = (vec(g_mix_pre), w_cat, bf_pad, vec(g_q_lora), w_q, vec(g_kv_lora), w_kv)
    post_params = (w_out.astype(BF16), vec(g_mix_post), vec(g_ffn_pre),
                   w_ffn_up.astype(BF16), w_ffn_down.astype(BF16), vec(g_ffn_post))
    gains = g_mix_out.reshape(depth, 4, 1, GROUP)
    tab = _rope_tables(positions)
    ret_tables = _retention_decay_tables(min(T_RET, x.shape[1]))
    for layer in range(depth):
        (fq, fkt, fv, mq, mkt, mv, rq, rk, rv, rg, sq, skt, sv) = _projection(
            x, tab, layer, proj_params)
        out_a = _softmax_attention(fq, fkt, fv, gains, layer, 0,
                                   chunk_causal=False, name="fox_attention")
        out_b = _softmax_attention(mq, mkt, mv, gains, layer, 1,
                                   chunk_causal=True, name="mla_attention")
        out_c = _retention(rq, rk, rv, rg, ret_tables, gains, layer, 2)
        out_d = _stick_attention(sq, skt, sv, gains, layer, 3)
        x = _post(x, out_a, out_b, out_c, out_d, layer, post_params)
    return x
```

```python
import functools

import numpy as np
import jax
import jax.numpy as jnp
from jax import lax
from jax.experimental import pallas as pl
from jax.experimental.pallas import tpu as pltpu

F32 = jnp.float32
BF16 = jnp.bfloat16

N_HEADS = 4
HEAD_DIM = 64
GROUP = N_HEADS * HEAD_DIM
CHUNK = 64
CHUNK_SHIFT = 6
MLA_ROPE = 32
MLA_QK = HEAD_DIM + MLA_ROPE
ROPE_BASE = 10000.0
EPS = 1e-6
RET_DECAY_OFFSET = 5.0
NEG_BIG = -1e30
LOG2E = 1.4426950408889634
STICK_EXIT = 128.0

LANES = 128
VMEM_LIMIT_BYTES = 56 * 1024 * 1024

T_KEY = 512
TM_POST = 512
TQ_ATTN = 512
T_STICK = 256
T_RET = 256
RET_BLOCKS_PER_STEP = 4
TM_TAB = 1024

C_FOX = 0
C_CQ = 768
C_CKV = 1024
C_MISC = 1152
C_RET = 1280
C_RVG = 1792
C_SB = 2304
C_END = 3072
MISC_ROT = 32
MISC_FF = 64


def _rms(x, g):
    return x * lax.rsqrt(jnp.mean(x * x, axis=-1, keepdims=True) + EPS) * g


def _log_sigmoid(x):
    return jnp.minimum(x, 0.0) - jnp.log1p(jnp.exp(-jnp.abs(x)))


def _dot(a, b):
    return jnp.dot(a, b, preferred_element_type=F32)


def _head_select(lane_head, per_head):
    out = per_head[N_HEADS - 1]
    for h in range(N_HEADS - 2, -1, -1):
        out = jnp.where(lane_head == h, per_head[h], out)
    return out


def _layer_spec(a, layer):
    zeros = (0,) * (a.ndim - 1)
    return pl.BlockSpec((None,) + a.shape[1:], lambda i, j: (layer,) + zeros)


def _gain_spec(layer, group):
    return pl.BlockSpec((None, None, 1, GROUP), lambda i, j: (layer, group, 0, 0))


def _per_head(fn, x, tq):
    return jnp.concatenate([fn(h, x[h * tq:(h + 1) * tq]) for h in range(N_HEADS)], axis=0)


def _table_kernel(pos_ref, invf_ref, sign_ref, tab_ref):
    half_r, half_m = HEAD_DIM // 2, MLA_ROPE // 2
    ang = pos_ref[...].astype(F32) * invf_ref[...]
    lane = lax.broadcasted_iota(jnp.int32, (1, LANES), 1)
    is_r = lane < half_r
    is_m = (lane >= half_r) & (lane < half_r + half_m)

    def spread_r(x):
        x = jnp.where(is_r, x, 0.0)
        x = x + pltpu.roll(x, half_r, 1)
        return x + pltpu.roll(x, 2 * half_r, 1)

    def spread_m(x):
        x = pltpu.roll(jnp.where(is_m, x, 0.0), LANES - half_r, 1)
        x = x + pltpu.roll(x, half_m, 1)
        x = x + pltpu.roll(x, 2 * half_m, 1)
        return x + pltpu.roll(x, 4 * half_m, 1)

    c = jnp.cos(ang)
    s = jnp.sin(ang)
    tab_ref[...] = jnp.concatenate(
        [spread_r(c), spread_r(s), spread_m(c), spread_m(s) * sign_ref[...]], axis=-1)


def _rope_tables(positions):
    b, s = positions.shape
    half_r, half_m = HEAD_DIM // 2, MLA_ROPE // 2
    invf_r = ROPE_BASE ** (-jnp.arange(half_r, dtype=F32) / half_r)
    invf_m = ROPE_BASE ** (-jnp.arange(half_m, dtype=F32) / half_m)
    invf = jnp.concatenate([invf_r, invf_m, jnp.zeros((LANES - half_r - half_m,), F32)])[None, :]
    lane = np.arange(LANES)
    sign = jnp.asarray(np.where(lane % MLA_ROPE < half_m, -1.0, 1.0), F32)[None, :]
    tm = min(TM_TAB, s)
    return pl.pallas_call(
        _table_kernel,
        out_shape=jax.ShapeDtypeStruct((b, s, 4 * LANES), F32),
        grid=(b, s // tm),
        in_specs=[pl.BlockSpec((None, tm, 1), lambda i, j: (i, j, 0)),
                  pl.BlockSpec((1, LANES), lambda i, j: (0, 0)),
                  pl.BlockSpec((1, LANES), lambda i, j: (0, 0))],
        out_specs=pl.BlockSpec((None, tm, 4 * LANES), lambda i, j: (i, j, 0)),
        compiler_params=pltpu.CompilerParams(
            dimension_semantics=("parallel", "parallel")),
        name="rope_tables",
    )(positions[:, :, None], invf, sign)


def _proj_kernel(x_ref, tab_ref, g_ref, w_ref, bf_ref, gq_ref, wq_ref, gkv_ref, wkv_ref,
                 fq_o, fkt_o, fv_o, mq_o, mkt_o, mv_o,
                 rq_o, rk_o, rv_o, rg_o, sq_o, skt_o, sv_o, carry_ref, *, mla_scale):
    tm = x_ref.shape[0]
    h = _rms(x_ref[...], g_ref[...]).astype(BF16)

    def proj(a, b):
        return _dot(h, w_ref[:, a:b])

    tab = tab_ref[...]
    cos_r, sin_r = tab[:, 0:LANES], tab[:, LANES:2 * LANES]
    cos_m, sin_m = tab[:, 2 * LANES:3 * LANES], tab[:, 3 * LANES:]
    head64 = lax.broadcasted_iota(jnp.int32, (1, GROUP), 1) >> 6
    lane = lax.broadcasted_iota(jnp.int32, (1, LANES), 1)
    misc = proj(C_MISC, C_RET)

    def store_heads(o_ref, q):
        qb = q.astype(BF16)
        zero = jnp.zeros_like(qb)
        for hh in range(N_HEADS):
            o_ref[hh] = jnp.where(head64 == hh, qb, zero)

    @pl.when(pl.program_id(1) == 0)
    def _():
        carry_ref[...] = jnp.zeros_like(carry_ref)

    ls = _log_sigmoid(pltpu.roll(misc, LANES - MISC_FF, 1) + bf_ref[...])
    row = lax.broadcasted_iota(jnp.int32, (tm, tm), 0)
    col = lax.broadcasted_iota(jnp.int32, (tm, tm), 1)
    tri = jnp.where(col <= row, 1.0, 0.0).astype(BF16)
    ls_hi = ls.astype(BF16)
    ls_rest = ls - ls_hi.astype(F32)
    ls_mid = ls_rest.astype(BF16)
    ls_lo = (ls_rest - ls_mid.astype(F32)).astype(BF16)
    parts = _dot(tri, jnp.concatenate([ls_hi, ls_mid, ls_lo], axis=-1))
    cum = (parts[:, :LANES] + parts[:, LANES:2 * LANES] + parts[:, 2 * LANES:]) + carry_ref[...]
    carry_ref[...] = cum[tm - 1:tm, :]
    c_hi = (cum * LOG2E).astype(BF16).astype(F32)
    c_rest = cum * LOG2E - c_hi
    c_mid = c_rest.astype(BF16).astype(F32)
    c_lo = (c_rest - c_mid).astype(BF16).astype(F32)
    term_of_lane = lane & 3
    spread = jnp.zeros((tm, LANES), F32)
    for hh in range(N_HEADS):
        for t, term in enumerate((c_hi, c_mid, c_lo)):
            val = jnp.broadcast_to(term[:, hh:hh + 1], (tm, LANES))
            spread = jnp.where(((lane >> 3) == hh) & (term_of_lane == t), val, spread)
    used = (lane < 8 * N_HEADS) & (term_of_lane < 3)
    first = used & ((lane & 4) == 0)
    second = used & ((lane & 4) != 0)
    gate_q = jnp.where(first, spread, jnp.where(second, 1.0, 0.0))
    gate_k = jnp.where(second, -spread, jnp.where(first, 1.0, 0.0))

    p = proj(C_FOX, C_CQ)
    fq = p[:, 0:GROUP] * LOG2E
    for hh in range(N_HEADS):
        pair = hh // 2
        nope = jnp.where((lane >> 6) == hh % 2, fq[:, pair * LANES:(pair + 1) * LANES], 0.0)
        gate = jnp.where((lane >> 3) == hh, gate_q, 0.0)
        fq_o[hh] = jnp.concatenate([nope, gate], axis=-1).astype(BF16)
    for pair in range(2):
        kp = jnp.concatenate([p[:, GROUP + pair * LANES:GROUP + (pair + 1) * LANES], gate_k],
                             axis=-1)
        fkt_o[pair * GROUP:(pair + 1) * GROUP, :] = kp.T.astype(BF16)
    fv_o[...] = p[:, 2 * GROUP:].astype(BF16)

    hq = _rms(proj(C_CQ, C_CKV), gq_ref[...]).astype(BF16)
    qm = _dot(hq, wq_ref[...]) * mla_scale
    q_nope = qm[:, :GROUP].astype(BF16)
    q_rope = (qm[:, GROUP:GROUP + LANES] * cos_m + qm[:, GROUP + LANES:] * sin_m).astype(BF16)
    zero = jnp.zeros_like(q_rope)
    for hh in range(N_HEADS):
        pair = hh // 2
        nope = jnp.where((lane >> 6) == hh % 2, q_nope[:, pair * LANES:(pair + 1) * LANES], zero)
        rope = jnp.where((lane >> 5) == hh, q_rope, zero)
        mq_o[hh] = jnp.concatenate([nope, rope], axis=-1)
    hkv = _rms(proj(C_CKV, C_MISC), gkv_ref[...]).astype(BF16)
    kv = _dot(hkv, wkv_ref[...])
    in_rope = lane < MLA_ROPE
    k_rope = (misc * jnp.where(in_rope, cos_m, 0.0)
              + pltpu.roll(misc, LANES - MISC_ROT, 1) * jnp.where(in_rope, sin_m, 0.0))
    k_rope = (k_rope + pltpu.roll(k_rope, MLA_ROPE, 1)
              + pltpu.roll(k_rope, 2 * MLA_ROPE, 1) + pltpu.roll(k_rope, 3 * MLA_ROPE, 1))
    for pair in range(2):
        kp = jnp.concatenate([kv[:, pair * LANES:(pair + 1) * LANES], k_rope], axis=-1)
        mkt_o[pair * GROUP:(pair + 1) * GROUP, :] = kp.T.astype(BF16)
    mv_o[...] = kv[:, GROUP:].astype(BF16)

    pr = proj(C_RET, C_RVG)

    def rotary(first, second):
        return jnp.concatenate([first * cos_r - second * sin_r, second * cos_r + first * sin_r],
                               axis=-1)

    rq_o[...] = rotary(pr[:, 0:LANES], pr[:, LANES:GROUP])
    rk_o[...] = rotary(pr[:, GROUP:GROUP + LANES], pr[:, GROUP + LANES:2 * GROUP])
    pv = proj(C_RVG, C_SB)
    rv_o[...] = pv[:, :GROUP].astype(BF16)
    rg_o[...] = pv[:, GROUP:]

    ps = proj(C_SB, C_END)
    store_heads(sq_o, ps[:, 0:GROUP])
    skt = ps[:, GROUP:2 * GROUP].T.astype(BF16)
    for c in range(tm // T_STICK):
        skt_o[c] = skt[:, c * T_STICK:(c + 1) * T_STICK]
    sv_o[...] = ps[:, 2 * GROUP:].astype(BF16)


def _projection(x, tab, layer, params):
    b, s, d = x.shape
    tm = min(T_KEY, s)
    nk = s // tm
    row = lambda w: pl.BlockSpec((None, tm, w), lambda i, j: (i, j, 0))
    heads = pl.BlockSpec((None, N_HEADS, tm, GROUP), lambda i, j: (i, 0, j, 0))
    keyt = lambda r: pl.BlockSpec((None, None, r, tm), lambda i, j: (i, j, 0, 0))
    bf = lambda w: jax.ShapeDtypeStruct((b, s, w), BF16)
    f32 = lambda w: jax.ShapeDtypeStruct((b, s, w), F32)
    q4 = jax.ShapeDtypeStruct((b, N_HEADS, s, GROUP), BF16)
    kt = lambda r: jax.ShapeDtypeStruct((b, nk, r, tm), BF16)
    n_st = tm // T_STICK
    skt_shape = jax.ShapeDtypeStruct((b, nk * n_st, GROUP, T_STICK), BF16)
    skt_spec = pl.BlockSpec((None, n_st, GROUP, T_STICK), lambda i, j: (i, j, 0, 0))
    out_shape = [q4, kt(2 * GROUP), bf(GROUP),
                 q4, kt(2 * GROUP), bf(GROUP),
                 f32(GROUP), f32(GROUP), bf(GROUP), f32(GROUP),
                 q4, skt_shape, bf(GROUP)]
    out_specs = [heads, keyt(2 * GROUP), row(GROUP),
                 heads, keyt(2 * GROUP), row(GROUP),
                 row(GROUP), row(GROUP), row(GROUP), row(GROUP),
                 heads, skt_spec, row(GROUP)]
    return pl.pallas_call(
        functools.partial(_proj_kernel, mla_scale=MLA_QK ** -0.5 * LOG2E),
        out_shape=out_shape,
        grid=(b, nk),
        in_specs=[row(d), row(4 * LANES)] + [_layer_spec(a, layer) for a in params],
        out_specs=out_specs,
        scratch_shapes=[pltpu.VMEM((1, LANES), F32)],
        compiler_params=pltpu.CompilerParams(
            dimension_semantics=("parallel", "arbitrary"),
            vmem_limit_bytes=VMEM_LIMIT_BYTES),
        name="in_projection",
    )(x, tab, *params)


def _softmax_attn_kernel(q_ref, kt_ref, v_ref, gain_ref, o_ref, acc_ref, m_ref, l_ref,
                         *, chunk_causal, tq, tk):
    i = pl.program_id(1)
    groups = kt_ref.shape[1] // GROUP
    hpg = N_HEADS // groups

    def block(j, c0, nk, r0, masked, first):
        nr = tq - r0
        kt = kt_ref[j, :, c0:c0 + nk]
        if masked:
            qpos = i * tq + r0 + lax.broadcasted_iota(jnp.int32, (nr, nk), 0)
            kpos = j * tk + c0 + lax.broadcasted_iota(jnp.int32, (nr, nk), 1)
            if chunk_causal:
                visible = (kpos >> CHUNK_SHIFT) <= (qpos >> CHUNK_SHIFT)
            else:
                visible = kpos <= qpos
        s_parts = [_dot(q_ref[g * hpg:(g + 1) * hpg, r0:tq, :].reshape(hpg * nr, GROUP),
                        kt[g * GROUP:(g + 1) * GROUP]) for g in range(groups)]
        p_rows, alphas = [], []
        for h in range(N_HEADS):
            hr = slice(h * tq + r0, (h + 1) * tq)
            s = s_parts[h // hpg][(h % hpg) * nr:(h % hpg + 1) * nr]
            if masked:
                s = jnp.where(visible, s, NEG_BIG)
            m_new = jnp.broadcast_to(jnp.max(s, axis=-1, keepdims=True), (nr, LANES))
            l_new = None
            if not first:
                m_prev = m_ref[hr]
                m_new = jnp.maximum(m_prev, m_new)
                alpha = jnp.exp2(m_prev - m_new)
                alphas.append(jnp.concatenate([alpha, alpha], axis=-1))
                l_new = alpha * l_ref[hr]
            ps = []
            for c in range(nk // LANES):
                pc = jnp.exp2(s[:, c * LANES:(c + 1) * LANES] - m_new)
                l_new = pc if l_new is None else l_new + pc
                ps.append(pc.astype(BF16))
            l_ref[hr] = l_new
            m_ref[hr] = m_new
            p_rows.append(jnp.concatenate(ps, axis=-1))
        start = pl.multiple_of(j * tk + c0, nk)
        pv = _dot(jnp.concatenate(p_rows, axis=0), v_ref[pl.ds(start, nk), :])
        if first:
            acc_ref[...] = pv
        elif r0 == 0:
            acc_ref[...] = acc_ref[...] * jnp.concatenate(alphas, axis=0) + pv
        else:
            for h in range(N_HEADS):
                hr = slice(h * tq + r0, (h + 1) * tq)
                acc_ref[hr] = acc_ref[hr] * alphas[h] + pv[h * nr:(h + 1) * nr]

    n_full = (i * tq) // tk
    half = tk // 2
    block(n_full, 0, half, 0, True, True)
    block(n_full, half, half, tq // 2, True, False)

    def body(j, carry):
        block(j, 0, tk, 0, False, False)
        return carry

    lax.fori_loop(0, n_full, body, 0)

    inv_l = 1.0 / jnp.sum(l_ref[...], axis=-1, keepdims=True)
    o_all = acc_ref[...] * inv_l
    head_v = lax.broadcasted_iota(jnp.int32, (1, GROUP), 1) >> 6
    o = _head_select(head_v, [o_all[h * tq:(h + 1) * tq] for h in range(N_HEADS)])
    o_ref[...] = _rms(o, gain_ref[...]).astype(o_ref.dtype)


def _softmax_attention(q, kt, v, gains, layer, group, *, chunk_causal, name):
    b, _, s, _ = q.shape
    nk, krows, tk = kt.shape[1:]
    tq = min(TQ_ATTN, s)
    assert tq == tk and tk % (2 * CHUNK) == 0, (tq, tk)
    rows = N_HEADS * tq
    return pl.pallas_call(
        functools.partial(_softmax_attn_kernel, chunk_causal=chunk_causal, tq=tq, tk=tk),
        out_shape=jax.ShapeDtypeStruct((b, s, GROUP), BF16),
        grid=(b, s // tq),
        in_specs=[pl.BlockSpec((None, N_HEADS, tq, GROUP), lambda i, j: (i, 0, j, 0)),
                  pl.BlockSpec((None, nk, krows, tk), lambda i, j: (i, 0, 0, 0)),
                  pl.BlockSpec((None, s, GROUP), lambda i, j: (i, 0, 0)),
                  _gain_spec(layer, group)],
        out_specs=pl.BlockSpec((None, tq, GROUP), lambda i, j: (i, j, 0)),
        scratch_shapes=[pltpu.VMEM((rows, GROUP), F32),
                        pltpu.VMEM((rows, LANES), F32),
                        pltpu.VMEM((rows, LANES), F32)],
        compiler_params=pltpu.CompilerParams(
            dimension_semantics=("parallel", "parallel"),
            vmem_limit_bytes=VMEM_LIMIT_BYTES),
        name=name,
    )(q, kt, v, gains)


def _stick_kernel(q_ref, kt_ref, v_ref, gain_ref, o_ref, acc_ref, r_ref, *, t):
    i = pl.program_id(1)
    rows = N_HEADS * t
    q = q_ref[...].reshape(rows, GROUP)
    rj = lax.broadcasted_iota(jnp.int32, (t, t), 0)
    cs = lax.broadcasted_iota(jnp.int32, (t, t), 1)
    upper = jnp.where(rj > cs, 1.0, 0.0).astype(BF16)
    upper2 = jnp.concatenate([upper, upper], axis=0)
    visible = cs < rj

    def block(j, diagonal):
        z = _dot(q, kt_ref[j])
        leave = jnp.maximum(z, 0.0) + jnp.log(1.0 + jnp.exp(-jnp.abs(z)))
        if diagonal:
            leave = _per_head(lambda h, x: jnp.where(visible, x, 0.0), leave, t)
        hi = leave.astype(BF16)
        lo = (leave - hi.astype(F32)).astype(BF16)
        later = _dot(jnp.concatenate([hi, lo], axis=-1), upper2)
        row_sum = jnp.broadcast_to(jnp.sum(leave, axis=-1, keepdims=True), (rows, LANES))
        if diagonal:
            r_new = row_sum
        else:
            r_prev = r_ref[...]
            later = later + jnp.concatenate([r_prev] * (t // LANES), axis=-1)
            r_new = r_prev + row_sum
        w = jnp.exp(z - leave - later)
        if diagonal:
            w = _per_head(lambda h, x: jnp.where(visible, x, 0.0), w, t)
        r_ref[...] = r_new
        start = pl.multiple_of(j * t, t)
        pv = _dot(w.astype(BF16), v_ref[pl.ds(start, t), :])
        if diagonal:
            acc_ref[...] = pv
        else:
            acc_ref[...] += pv
        return jnp.min(r_new)

    def cond(carry):
        j, r_min = carry
        return jnp.logical_and(j >= 0, r_min < STICK_EXIT)

    def body(carry):
        j, _ = carry
        return j - 1, block(j, False)

    def finish():
        acc = acc_ref[...]
        head_v = lax.broadcasted_iota(jnp.int32, (1, GROUP), 1) >> 6
        o = _head_select(head_v, [acc[h * t:(h + 1) * t] for h in range(N_HEADS)])
        o_ref[...] = _rms(o, gain_ref[...]).astype(o_ref.dtype)

    @pl.when(i == 0)
    def _():
        block(0, True)
        finish()

    @pl.when(i > 0)
    def _():
        block(i, True)
        r_min = block(i - 1, False)
        lax.while_loop(cond, body, (i - 2, r_min))
        finish()


def _stick_attention(q, kt, v, gains, layer, group):
    b, _, s, _ = q.shape
    nk, krows, tk = kt.shape[1:]
    tq = tk
    rows = N_HEADS * tq
    return pl.pallas_call(
        functools.partial(_stick_kernel, t=tk),
        out_shape=jax.ShapeDtypeStruct((b, s, GROUP), BF16),
        grid=(b, s // tq),
        in_specs=[pl.BlockSpec((None, N_HEADS, tq, GROUP), lambda i, j: (i, 0, j, 0)),
                  pl.BlockSpec((None, nk, krows, tk), lambda i, j: (i, 0, 0, 0)),
                  pl.BlockSpec((None, s, GROUP), lambda i, j: (i, 0, 0)),
                  _gain_spec(layer, group)],
        out_specs=pl.BlockSpec((None, tq, GROUP), lambda i, j: (i, j, 0)),
        scratch_shapes=[pltpu.VMEM((rows, GROUP), F32),
                        pltpu.VMEM((rows, LANES), F32)],
        compiler_params=pltpu.CompilerParams(
            dimension_semantics=("parallel", "parallel"),
            vmem_limit_bytes=VMEM_LIMIT_BYTES),
        name="stick_attention",
    )(q, kt, v, gains)


def _retention_kernel(q_ref, k_ref, v_ref, g_ref, d_intra_ref, d_q_ref, d_kt_ref, d_state_ref,
                      gain_ref, o_ref, state_ref, *, t):
    @pl.when(pl.program_id(1) == 0)
    def _():
        state_ref[...] = jnp.zeros_like(state_ref)

    head = lax.broadcasted_iota(jnp.int32, (1, GROUP), 1) >> 6
    head_qk = (lax.broadcasted_iota(jnp.int32, (1, GROUP), 1) & (LANES - 1)) >> 5
    head_qk_r = (lax.broadcasted_iota(jnp.int32, (GROUP, 1), 0) & (LANES - 1)) >> 5
    same_head = head_qk_r == head

    state = state_ref[...]
    outs = []
    for sb in range(q_ref.shape[0] // t):
        rows = slice(sb * t, (sb + 1) * t)
        q = q_ref[rows]
        v = v_ref[rows]
        kt = k_ref[rows].T
        q_b = q.astype(BF16)
        zq = jnp.zeros_like(q_b)
        q4 = jnp.concatenate([jnp.where(head_qk == h, q_b, zq) for h in range(N_HEADS)], axis=0)
        w = (_dot(q4, kt.astype(BF16)) * d_intra_ref[...]).astype(BF16)
        wv = _dot(w, v)
        out = _head_select(head, [wv[h * t:(h + 1) * t] for h in range(N_HEADS)])
        outs.append(out + _dot((q * d_q_ref[...]).astype(BF16), state.astype(BF16)))
        kv = _dot((kt * d_kt_ref[...]).astype(BF16), v)
        state = state * d_state_ref[...] + jnp.where(same_head, kv, 0.0)
    state_ref[...] = state
    out = jnp.concatenate(outs, axis=0)

    head_r = lax.broadcasted_iota(jnp.int32, (GROUP, 1), 0) >> 6
    avg = jnp.where(head_r == head, 1.0 / HEAD_DIM, 0.0).astype(BF16)
    avg3 = jnp.concatenate([avg, avg, avg], axis=0)

    def head_mean(x):
        hi = x.astype(BF16)
        rest = x - hi.astype(F32)
        mid = rest.astype(BF16)
        lo = (rest - mid.astype(F32)).astype(BF16)
        return _dot(jnp.concatenate([hi, mid, lo], axis=-1), avg3)

    mu = head_mean(out)
    cen = out - mu
    var = head_mean(cen * cen)
    y = cen * lax.rsqrt(var + EPS) * gain_ref[...]
    g = g_ref[...]
    o_ref[...] = (y * (g * (1.0 / (1.0 + jnp.exp(-g))))).astype(o_ref.dtype)


def _retention_decay_tables(t):
    log_gamma = jnp.log1p(-jnp.power(2.0, -RET_DECAY_OFFSET - jnp.arange(N_HEADS, dtype=F32)))
    lg_v = jnp.repeat(log_gamma, HEAD_DIM)
    lg_qk = jnp.tile(jnp.repeat(log_gamma, HEAD_DIM // 2), 2)
    r = jnp.arange(t, dtype=F32)
    dist = jnp.abs(r[:, None] - r[None, :])
    reach = (jnp.arange(t)[None, :] // CHUNK) <= (jnp.arange(t)[:, None] // CHUNK)
    d_intra = jnp.where(reach[None], jnp.exp(log_gamma[:, None, None] * dist[None]), 0.0)
    d_q = jnp.exp(lg_qk[None, :] * (r[:, None] + 1.0))
    d_kt = jnp.exp(lg_qk[:, None] * (float(t - 1) - r)[None, :])
    d_state = jnp.exp(lg_v * float(t))[None, :]
    return d_intra.reshape(N_HEADS * t, t), d_q, d_kt, d_state


def _retention(q, k, v, g, tables, gains, layer, group):
    b, s, _ = q.shape
    t = tables[1].shape[0]
    tm = min(RET_BLOCKS_PER_STEP * t, s)
    row = pl.BlockSpec((None, tm, GROUP), lambda i, j: (i, j, 0))
    const = lambda a: pl.BlockSpec(a.shape, lambda i, j: (0, 0))
    return pl.pallas_call(
        functools.partial(_retention_kernel, t=t),
        out_shape=jax.ShapeDtypeStruct((b, s, GROUP), BF16),
        grid=(b, s // tm),
        in_specs=[row, row, row, row] + [const(a) for a in tables] + [_gain_spec(layer, group)],
        out_specs=row,
        scratch_shapes=[pltpu.VMEM((GROUP, GROUP), F32)],
        compiler_params=pltpu.CompilerParams(
            dimension_semantics=("parallel", "arbitrary"),
            vmem_limit_bytes=VMEM_LIMIT_BYTES),
        name="retention",
    )(q, k, v, g, *tables, gains)


def _post_kernel(x_ref, a_ref, b_ref, c_ref, d_ref, wo_ref, gpost_ref, gpre_ref,
                 wup_ref, wdn_ref, gfpost_ref, o_ref, *, f_chunk):
    mixed = jnp.concatenate([a_ref[...], b_ref[...], c_ref[...], d_ref[...]], axis=-1)
    x1 = x_ref[...] + _rms(_dot(mixed, wo_ref[...]), gpost_ref[...])
    h = _rms(x1, gpre_ref[...]).astype(BF16)
    y = None
    for c in range(wup_ref.shape[1] // f_chunk):
        u = jnp.maximum(_dot(h, wup_ref[:, c * f_chunk:(c + 1) * f_chunk]), 0.0)
        d = _dot((u * u).astype(BF16), wdn_ref[c * f_chunk:(c + 1) * f_chunk, :])
        y = d if y is None else y + d
    o_ref[...] = x1 + _rms(y, gfpost_ref[...])


def _post(x, a, bb, c, dd, layer, params):
    b, s, d = x.shape
    tm = min(TM_POST, s)
    row = lambda w: pl.BlockSpec((None, tm, w), lambda i, j: (i, j, 0))
    return pl.pallas_call(
        functools.partial(_post_kernel, f_chunk=1024),
        out_shape=jax.ShapeDtypeStruct((b, s, d), F32),
        grid=(b, s // tm),
        in_specs=([row(d), row(GROUP), row(GROUP), row(GROUP), row(GROUP)]
                  + [_layer_spec(p, layer) for p in params]),
        out_specs=row(d),
        compiler_params=pltpu.CompilerParams(
            dimension_semantics=("parallel", "parallel"),
            vmem_limit_bytes=VMEM_LIMIT_BYTES),
        name="out_proj_mlp",
    )(x, a, bb, c, dd, *params)


def _swap_halves(w, period):
    lead = w.shape[:-1]
    g = w.reshape(lead + (-1, 2, period // 2))
    return jnp.concatenate([g[..., 1:, :], g[..., :1, :]], axis=-2).reshape(w.shape)


def _halves_first(w, period):
    lead = w.shape[:-1]
    g = w.reshape(lead + (-1, 2, period // 2))
    return jnp.concatenate([g[..., 0, :].reshape(lead + (-1,)),
                            g[..., 1, :].reshape(lead + (-1,))], axis=-1)


def _projection_weights(w_in, b_forget, w_q_up, w_kv_up):
    depth, d, _ = w_in.shape
    sizes = [GROUP, GROUP, GROUP, N_HEADS, 256, 128, MLA_ROPE] + [GROUP] * 7
    offs = np.concatenate([[0], np.cumsum(sizes)])
    (fq, fk, fv, ff, cq, ckv, kr, rq, rk, rv, rg, sq, sk, sv) = [
        w_in[..., int(offs[n]):int(offs[n + 1])] for n in range(len(sizes))]
    scale = HEAD_DIM ** -0.5
    misc = jnp.concatenate(
        [kr, _swap_halves(kr, MLA_ROPE), ff,
         jnp.zeros((depth, d, LANES - 2 * MLA_ROPE - N_HEADS), F32)], axis=-1)
    w_cat = jnp.concatenate(
        [fq * scale, fk, fv, cq, ckv, misc, _halves_first(rq, HEAD_DIM),
         _halves_first(rk, HEAD_DIM) * scale, rv, rg, sq * scale, sk, sv], axis=-1).astype(BF16)
    bf_pad = jnp.concatenate(
        [b_forget, jnp.zeros((depth, LANES - N_HEADS), F32)], axis=-1)[:, None, :]
    wq4 = w_q_up.reshape(depth, -1, N_HEADS, MLA_QK)
    q_nope = wq4[..., :HEAD_DIM].reshape(depth, -1, GROUP)
    q_rope = wq4[..., HEAD_DIM:].reshape(depth, -1, N_HEADS * MLA_ROPE)
    w_q = jnp.concatenate(
        [q_nope, q_rope, _swap_halves(q_rope, MLA_ROPE)], axis=-1).astype(BF16)
    wkv4 = w_kv_up.reshape(depth, -1, N_HEADS, 2 * HEAD_DIM)
    w_kv = jnp.concatenate([wkv4[..., :HEAD_DIM].reshape(depth, -1, GROUP),
                            wkv4[..., HEAD_DIM:].reshape(depth, -1, GROUP)],
                           axis=-1).astype(BF16)
    return w_cat, bf_pad, w_q, w_kv


def kernel(x, positions, g_mix_pre, w_in, b_forget, g_q_lora, w_q_up, g_kv_lora, w_kv_up,
           g_mix_out, w_out, g_mix_post, g_ffn_pre, w_ffn_up, w_ffn_down, g_ffn_post):
    depth = w_in.shape[0]
    vec = lambda g: g[:, None, :]
    w_cat, bf_pad, w_q, w_kv = _projection_weights(w_in, b_forget, w_q_up, w_kv_up)
    proj_params = (vec(g_mix_pre), w_cat, bf_pad, vec(g_q_lora), w_q, vec(g_kv_lora), w_kv)
    post_params = (w_out.astype(BF16), vec(g_mix_post), vec(g_ffn_pre),
                   w_ffn_up.astype(BF16), w_ffn_down.astype(BF16), vec(g_ffn_post))
    gains = g_mix_out.reshape(depth, 4, 1, GROUP)
    tab = _rope_tables(positions)
    ret_tables = _retention_decay_tables(min(T_RET, x.shape[1]))
    for layer in range(depth):
        (fq, fkt, fv, mq, mkt, mv, rq, rk, rv, rg, sq, skt, sv) = _projection(
            x, tab, layer, proj_params)
        out_a = _softmax_attention(fq, fkt, fv, gains, layer, 0,
                                   chunk_causal=False, name="fox_attention")
        out_b = _softmax_attention(mq, mkt, mv, gains, layer, 1,
                                   chunk_causal=True, name="mla_attention")
        out_c = _retention(rq, rk, rv, rg, ret_tables, gains, layer, 2)
        out_d = _stick_attention(sq, skt, sv, gains, layer, 3)
        x = _post(x, out_a, out_b, out_c, out_d, layer, post_params)
    return x
```

```python
import functools

import numpy as np
import jax
import jax.numpy as jnp
from jax import lax
from jax.experimental import pallas as pl
from jax.experimental.pallas import tpu as pltpu

F32 = jnp.float32
BF16 = jnp.bfloat16

N_HEADS = 4
HEAD_DIM = 64
GROUP = N_HEADS * HEAD_DIM
CHUNK = 64
CHUNK_SHIFT = 6
MLA_ROPE = 32
MLA_QK = HEAD_DIM + MLA_ROPE
ROPE_BASE = 10000.0
EPS = 1e-6
RET_DECAY_OFFSET = 5.0
NEG_BIG = -1e30
LOG2E = 1.4426950408889634
STICK_EXIT = 128.0

LANES = 128
VMEM_LIMIT_BYTES = 56 * 1024 * 1024

T_KEY = 512
TM_POST = 512
TQ_ATTN = 512
T_STICK = 256
T_RET = 256
RET_BLOCKS_PER_STEP = 4
TM_TAB = 1024

C_FOX = 0
C_CQ = 768
C_CKV = 1024
C_MISC = 1152
C_RET = 1280
C_RVG = 1792
C_SB = 2304
C_END = 3072
MISC_ROT = 32
MISC_FF = 64


def _rms(x, g):
    return x * lax.rsqrt(jnp.mean(x * x, axis=-1, keepdims=True) + EPS) * g


def _log_sigmoid(x):
    return jnp.minimum(x, 0.0) - jnp.log1p(jnp.exp(-jnp.abs(x)))


def _dot(a, b):
    return jnp.dot(a, b, preferred_element_type=F32)


def _head_select(lane_head, per_head):
    out = per_head[N_HEADS - 1]
    for h in range(N_HEADS - 2, -1, -1):
        out = jnp.where(lane_head == h, per_head[h], out)
    return out


def _layer_spec(a, layer):
    zeros = (0,) * (a.ndim - 1)
    return pl.BlockSpec((None,) + a.shape[1:], lambda i, j: (layer,) + zeros)


def _gain_spec(layer, group):
    return pl.BlockSpec((None, None, 1, GROUP), lambda i, j: (layer, group, 0, 0))


def _per_head(fn, x, tq):
    return jnp.concatenate([fn(h, x[h * tq:(h + 1) * tq]) for h in range(N_HEADS)], axis=0)


def _table_kernel(pos_ref, invf_ref, sign_ref, tab_ref):
    half_r, half_m = HEAD_DIM // 2, MLA_ROPE // 2
    ang = pos_ref[...].astype(F32) * invf_ref[...]
    lane = lax.broadcasted_iota(jnp.int32, (1, LANES), 1)
    is_r = lane < half_r
    is_m = (lane >= half_r) & (lane < half_r + half_m)

    def spread_r(x):
        x = jnp.where(is_r, x, 0.0)
        x = x + pltpu.roll(x, half_r, 1)
        return x + pltpu.roll(x, 2 * half_r, 1)

    def spread_m(x):
        x = pltpu.roll(jnp.where(is_m, x, 0.0), LANES - half_r, 1)
        x = x + pltpu.roll(x, half_m, 1)
        x = x + pltpu.roll(x, 2 * half_m, 1)
        return x + pltpu.roll(x, 4 * half_m, 1)

    c = jnp.cos(ang)
    s = jnp.sin(ang)
    tab_ref[...] = jnp.concatenate(
        [spread_r(c), spread_r(s), spread_m(c), spread_m(s) * sign_ref[...]], axis=-1)


def _rope_tables(positions):
    b, s = positions.shape
    half_r, half_m = HEAD_DIM // 2, MLA_ROPE // 2
    invf_r = ROPE_BASE ** (-jnp.arange(half_r, dtype=F32) / half_r)
    invf_m = ROPE_BASE ** (-jnp.arange(half_m, dtype=F32) / half_m)
    invf = jnp.concatenate([invf_r, invf_m, jnp.zeros((LANES - half_r - half_m,), F32)])[None, :]
    lane = np.arange(LANES)
    sign = jnp.asarray(np.where(lane % MLA_ROPE < half_m, -1.0, 1.0), F32)[None, :]
    tm = min(TM_TAB, s)
    return pl.pallas_call(
        _table_kernel,
        out_shape=jax.ShapeDtypeStruct((b, s, 4 * LANES), F32),
        grid=(b, s // tm),
        in_specs=[pl.BlockSpec((None, tm, 1), lambda i, j: (i, j, 0)),
                  pl.BlockSpec((1, LANES), lambda i, j: (0, 0)),
                  pl.BlockSpec((1, LANES), lambda i, j: (0, 0))],
        out_specs=pl.BlockSpec((None, tm, 4 * LANES), lambda i, j: (i, j, 0)),
        compiler_params=pltpu.CompilerParams(
            dimension_semantics=("parallel", "parallel")),
        name="rope_tables",
    )(positions[:, :, None], invf, sign)


def _proj_kernel(x_ref, tab_ref, g_ref, w_ref, bf_ref, gq_ref, wq_ref, gkv_ref, wkv_ref,
                 fq_o, fkt_o, fv_o, mq_o, mkt_o, mv_o,
                 rq_o, rk_o, rv_o, rg_o, sq_o, skt_o, sv_o, carry_ref, *, mla_scale):
    tm = x_ref.shape[0]
    h = _rms(x_ref[...], g_ref[...]).astype(BF16)

    def proj(a, b):
        return _dot(h, w_ref[:, a:b])

    tab = tab_ref[...]
    cos_r, sin_r = tab[:, 0:LANES], tab[:, LANES:2 * LANES]
    cos_m, sin_m = tab[:, 2 * LANES:3 * LANES], tab[:, 3 * LANES:]
    head64 = lax.broadcasted_iota(jnp.int32, (1, GROUP), 1) >> 6
    lane = lax.broadcasted_iota(jnp.int32, (1, LANES), 1)
    misc = proj(C_MISC, C_RET)

    def store_heads(o_ref, q):
        qb = q.astype(BF16)
        zero = jnp.zeros_like(qb)
        for hh in range(N_HEADS):
            o_ref[hh] = jnp.where(head64 == hh, qb, zero)

    @pl.when(pl.program_id(1) == 0)
    def _():
        carry_ref[...] = jnp.zeros_like(carry_ref)

    ls = _log_sigmoid(pltpu.roll(misc, LANES - MISC_FF, 1) + bf_ref[...])
    row = lax.broadcasted_iota(jnp.int32, (tm, tm), 0)
    col = lax.broadcasted_iota(jnp.int32, (tm, tm), 1)
    tri = jnp.where(col <= row, 1.0, 0.0).astype(BF16)
    ls_hi = ls.astype(BF16)
    ls_rest = ls - ls_hi.astype(F32)
    ls_mid = ls_rest.astype(BF16)
    ls_lo = (ls_rest - ls_mid.astype(F32)).astype(BF16)
    parts = _dot(tri, jnp.concatenate([ls_hi, ls_mid, ls_lo], axis=-1))
    cum = (parts[:, :LANES] + parts[:, LANES:2 * LANES] + parts[:, 2 * LANES:]) + carry_ref[...]
    carry_ref[...] = cum[tm - 1:tm, :]
    c_hi = (cum * LOG2E).astype(BF16).astype(F32)
    c_rest = cum * LOG2E - c_hi
    c_mid = c_rest.astype(BF16).astype(F32)
    c_lo = (c_rest - c_mid).astype(BF16).astype(F32)
    term_of_lane = lane & 3
    spread = jnp.zeros((tm, LANES), F32)
    for hh in range(N_HEADS):
        for t, term in enumerate((c_hi, c_mid, c_lo)):
            val = jnp.broadcast_to(term[:, hh:hh + 1], (tm, LANES))
            spread = jnp.where(((lane >> 3) == hh) & (term_of_lane == t), val, spread)
    used = (lane < 8 * N_HEADS) & (term_of_lane < 3)
    first = used & ((lane & 4) == 0)
    second = used & ((lane & 4) != 0)
    gate_q = jnp.where(first, spread, jnp.where(second, 1.0, 0.0))
    gate_k = jnp.where(second, -spread, jnp.where(first, 1.0, 0.0))

    p = proj(C_FOX, C_CQ)
    fq = p[:, 0:GROUP] * LOG2E
    for hh in range(N_HEADS):
        pair = hh // 2
        nope = jnp.where((lane >> 6) == hh % 2, fq[:, pair * LANES:(pair + 1) * LANES], 0.0)
        gate = jnp.where((lane >> 3) == hh, gate_q, 0.0)
        fq_o[hh] = jnp.concatenate([nope, gate], axis=-1).astype(BF16)
    for pair in range(2):
        kp = jnp.concatenate([p[:, GROUP + pair * LANES:GROUP + (pair + 1) * LANES], gate_k],
                             axis=-1)
        fkt_o[pair * GROUP:(pair + 1) * GROUP, :] = kp.T.astype(BF16)
    fv_o[...] = p[:, 2 * GROUP:].astype(BF16)

    hq = _rms(proj(C_CQ, C_CKV), gq_ref[...]).astype(BF16)
    qm = _dot(hq, wq_ref[...]) * mla_scale
    q_nope = qm[:, :GROUP].astype(BF16)
    q_rope = (qm[:, GROUP:GROUP + LANES] * cos_m + qm[:, GROUP + LANES:] * sin_m).astype(BF16)
    zero = jnp.zeros_like(q_rope)
    for hh in range(N_HEADS):
        pair = hh // 2
        nope = jnp.where((lane >> 6) == hh % 2, q_nope[:, pair * LANES:(pair + 1) * LANES], zero)
        rope = jnp.where((lane >> 5) == hh, q_rope, zero)
        mq_o[hh] = jnp.concatenate([nope, rope], axis=-1)
    hkv = _rms(proj(C_CKV, C_MISC), gkv_ref[...]).astype(BF16)
    kv = _dot(hkv, wkv_ref[...])
    in_rope = lane < MLA_ROPE
    k_rope = (misc * jnp.where(in_rope, cos_m, 0.0)
              + pltpu.roll(misc, LANES - MISC_ROT, 1) * jnp.where(in_rope, sin_m, 0.0))
    k_rope = (k_rope + pltpu.roll(k_rope, MLA_ROPE, 1)
              + pltpu.roll(k_rope, 2 * MLA_ROPE, 1) + pltpu.roll(k_rope, 3 * MLA_ROPE, 1))
    for pair in range(2):
        kp = jnp.concatenate([kv[:, pair * LANES:(pair + 1) * LANES], k_rope], axis=-1)
        mkt_o[pair * GROUP:(pair + 1) * GROUP, :] = kp.T.astype(BF16)
    mv_o[...] = kv[:, GROUP:].astype(BF16)

    pr = proj(C_RET, C_RVG)

    def rotary(first, second):
        return jnp.concatenate([first * cos_r - second * sin_r, second * cos_r + first * sin_r],
                               axis=-1)

    rq_o[...] = rotary(pr[:, 0:LANES], pr[:, LANES:GROUP])
    rk_o[...] = rotary(pr[:, GROUP:GROUP + LANES], pr[:, GROUP + LANES:2 * GROUP])
    pv = proj(C_RVG, C_SB)
    rv_o[...] = pv[:, :GROUP].astype(BF16)
    rg_o[...] = pv[:, GROUP:]

    ps = proj(C_SB, C_END)
    store_heads(sq_o, ps[:, 0:GROUP])
    skt = ps[:, GROUP:2 * GROUP].T.astype(BF16)
    for c in range(tm // T_STICK):
        skt_o[c] = skt[:, c * T_STICK:(c + 1) * T_STICK]
    sv_o[...] = ps[:, 2 * GROUP:].astype(BF16)


def _projection(x, tab, layer, params):
    b, s, d = x.shape
    tm = min(T_KEY, s)
    nk = s // tm
    row = lambda w: pl.BlockSpec((None, tm, w), lambda i, j: (i, j, 0))
    heads = pl.BlockSpec((None, N_HEADS, tm, GROUP), lambda i, j: (i, 0, j, 0))
    keyt = lambda r: pl.BlockSpec((None, None, r, tm), lambda i, j: (i, j, 0, 0))
    bf = lambda w: jax.ShapeDtypeStruct((b, s, w), BF16)
    f32 = lambda w: jax.ShapeDtypeStruct((b, s, w), F32)
    q4 = jax.ShapeDtypeStruct((b, N_HEADS, s, GROUP), BF16)
    kt = lambda r: jax.ShapeDtypeStruct((b, nk, r, tm), BF16)
    n_st = tm // T_STICK
    skt_shape = jax.ShapeDtypeStruct((b, nk * n_st, GROUP, T_STICK), BF16)
    skt_spec = pl.BlockSpec((None, n_st, GROUP, T_STICK), lambda i, j: (i, j, 0, 0))
    out_shape = [q4, kt(2 * GROUP), bf(GROUP),
                 q4, kt(2 * GROUP), bf(GROUP),
                 f32(GROUP), f32(GROUP), bf(GROUP), f32(GROUP),
                 q4, skt_shape, bf(GROUP)]
    out_specs = [heads, keyt(2 * GROUP), row(GROUP),
                 heads, keyt(2 * GROUP), row(GROUP),
                 row(GROUP), row(GROUP), row(GROUP), row(GROUP),
                 heads, skt_spec, row(GROUP)]
    return pl.pallas_call(
        functools.partial(_proj_kernel, mla_scale=MLA_QK ** -0.5 * LOG2E),
        out_shape=out_shape,
        grid=(b, nk),
        in_specs=[row(d), row(4 * LANES)] + [_layer_spec(a, layer) for a in params],
        out_specs=out_specs,
        scratch_shapes=[pltpu.VMEM((1, LANES), F32)],
        compiler_params=pltpu.CompilerParams(
            dimension_semantics=("parallel", "arbitrary"),
            vmem_limit_bytes=VMEM_LIMIT_BYTES),
        name="in_projection",
    )(x, tab, *params)


def _softmax_attn_kernel(q_ref, kt_ref, v_ref, gain_ref, o_ref, acc_ref, m_ref, l_ref,
                         *, chunk_causal, tq, tk):
    i = pl.program_id(1)
    groups = kt_ref.shape[1] // GROUP
    hpg = N_HEADS // groups

    def block(j, c0, nk, r0, masked, first):
        nr = tq - r0
        kt = kt_ref[j, :, c0:c0 + nk]
        if masked:
            qpos = i * tq + r0 + lax.broadcasted_iota(jnp.int32, (nr, nk), 0)
            kpos = j * tk + c0 + lax.broadcasted_iota(jnp.int32, (nr, nk), 1)
            if chunk_causal:
                visible = (kpos >> CHUNK_SHIFT) <= (qpos >> CHUNK_SHIFT)
            else:
                visible = kpos <= qpos
        s_parts = [_dot(q_ref[g * hpg:(g + 1) * hpg, r0:tq, :].reshape(hpg * nr, GROUP),
                        kt[g * GROUP:(g + 1) * GROUP]) for g in range(groups)]
        p_rows, alphas = [], []
        for h in range(N_HEADS):
            hr = slice(h * tq + r0, (h + 1) * tq)
            s = s_parts[h // hpg][(h % hpg) * nr:(h % hpg + 1) * nr]
            if masked:
                s = jnp.where(visible, s, NEG_BIG)
            m_new = jnp.broadcast_to(jnp.max(s, axis=-1, keepdims=True), (nr, LANES))
            l_new = None
            if not first:
                m_prev = m_ref[hr]
                m_new = jnp.maximum(m_prev, m_new)
                alpha = jnp.exp2(m_prev - m_new)
                alphas.append(jnp.concatenate([alpha, alpha], axis=-1))
                l_new = alpha * l_ref[hr]
            ps = []
            for c in range(nk // LANES):
                pc = jnp.exp2(s[:, c * LANES:(c + 1) * LANES] - m_new)
                l_new = pc if l_new is None else l_new + pc
                ps.append(pc.astype(BF16))
            l_ref[hr] = l_new
            m_ref[hr] = m_new
            p_rows.append(jnp.concatenate(ps, axis=-1))
        start = pl.multiple_of(j * tk + c0, nk)
        pv = _dot(jnp.concatenate(p_rows, axis=0), v_ref[pl.ds(start, nk), :])
        if first:
            acc_ref[...] = pv
        elif r0 == 0:
            acc_ref[...] = acc_ref[...] * jnp.concatenate(alphas, axis=0) + pv
        else:
            for h in range(N_HEADS):
                hr = slice(h * tq + r0, (h + 1) * tq)
                acc_ref[hr] = acc_ref[hr] * alphas[h] + pv[h * nr:(h + 1) * nr]

    n_full = (i * tq) // tk
    half = tk // 2
    block(n_full, 0, half, 0, True, True)
    block(n_full, half, half, tq // 2, True, False)

    def body(j, carry):
        block(j, 0, tk, 0, False, False)
        return carry

    lax.fori_loop(0, n_full, body, 0)

    inv_l = 1.0 / jnp.sum(l_ref[...], axis=-1, keepdims=True)
    o_all = acc_ref[...] * inv_l
    head_v = lax.broadcasted_iota(jnp.int32, (1, GROUP), 1) >> 6
    o = _head_select(head_v, [o_all[h * tq:(h + 1) * tq] for h in range(N_HEADS)])
    o_ref[...] = _rms(o, gain_ref[...]).astype(o_ref.dtype)


def _softmax_attention(q, kt, v, gains, layer, group, *, chunk_causal, name):
    b, _, s, _ = q.shape
    nk, krows, tk = kt.shape[1:]
    tq = min(TQ_ATTN, s)
    assert tq == tk and tk % (2 * CHUNK) == 0, (tq, tk)
    rows = N_HEADS * tq
    return pl.pallas_call(
        functools.partial(_softmax_attn_kernel, chunk_causal=chunk_causal, tq=tq, tk=tk),
        out_shape=jax.ShapeDtypeStruct((b, s, GROUP), BF16),
        grid=(b, s // tq),
        in_specs=[pl.BlockSpec((None, N_HEADS, tq, GROUP), lambda i, j: (i, 0, j, 0)),
                  pl.BlockSpec((None, nk, krows, tk), lambda i, j: (i, 0, 0, 0)),
                  pl.BlockSpec((None, s, GROUP), lambda i, j: (i, 0, 0)),
                  _gain_spec(layer, group)],
        out_specs=pl.BlockSpec((None, tq, GROUP), lambda i, j: (i, j, 0)),
        scratch_shapes=[pltpu.VMEM((rows, GROUP), F32),
                        pltpu.VMEM((rows, LANES), F32),
                        pltpu.VMEM((rows, LANES), F32)],
        compiler_params=pltpu.CompilerParams(
            dimension_semantics=("parallel", "parallel"),
            vmem_limit_bytes=VMEM_LIMIT_BYTES),
        name=name,
    )(q, kt, v, gains)


def _stick_kernel(q_ref, kt_ref, v_ref, gain_ref, o_ref, acc_ref, r_ref, *, t):
    i = pl.program_id(1)
    rows = N_HEADS * t
    q = q_ref[...].reshape(rows, GROUP)
    rj = lax.broadcasted_iota(jnp.int32, (t, t), 0)
    cs = lax.broadcasted_iota(jnp.int32, (t, t), 1)
    upper = jnp.where(rj > cs, 1.0, 0.0).astype(BF16)
    upper2 = jnp.concatenate([upper, upper], axis=0)
    visible = cs < rj

    def block(j, diagonal):
        z = _dot(q, kt_ref[j])
        leave = jnp.maximum(z, 0.0) + jnp.log(1.0 + jnp.exp(-jnp.abs(z)))
        if diagonal:
            leave = _per_head(lambda h, x: jnp.where(visible, x, 0.0), leave, t)
        hi = leave.astype(BF16)
        lo = (leave - hi.astype(F32)).astype(BF16)
        later = _dot(jnp.concatenate([hi, lo], axis=-1), upper2)
        row_sum = jnp.broadcast_to(jnp.sum(leave, axis=-1, keepdims=True), (rows, LANES))
        if diagonal:
            r_new = row_sum
        else:
            r_prev = r_ref[...]
            later = later + jnp.concatenate([r_prev] * (t // LANES), axis=-1)
            r_new = r_prev + row_sum
        w = jnp.exp(z - leave - later)
        if diagonal:
            w = _per_head(lambda h, x: jnp.where(visible, x, 0.0), w, t)
        r_ref[...] = r_new
        start = pl.multiple_of(j * t, t)
        pv = _dot(w.astype(BF16), v_ref[pl.ds(start, t), :])
        if diagonal:
            acc_ref[...] = pv
        else:
            acc_ref[...] += pv
        return jnp.min(r_new)

    def cond(carry):
        j, r_min = carry
        return jnp.logical_and(j >= 0, r_min < STICK_EXIT)

    def body(carry):
        j, _ = carry
        return j - 1, block(j, False)

    def finish():
        acc = acc_ref[...]
        head_v = lax.broadcasted_iota(jnp.int32, (1, GROUP), 1) >> 6
        o = _head_select(head_v, [acc[h * t:(h + 1) * t] for h in range(N_HEADS)])
        o_ref[...] = _rms(o, gain_ref[...]).astype(o_ref.dtype)

    @pl.when(i == 0)
    def _():
        block(0, True)
        finish()

    @pl.when(i > 0)
    def _():
        block(i, True)
        r_min = block(i - 1, False)
        lax.while_loop(cond, body, (i - 2, r_min))
        finish()


def _stick_attention(q, kt, v, gains, layer, group):
    b, _, s, _ = q.shape
    nk, krows, tk = kt.shape[1:]
    tq = tk
    rows = N_HEADS * tq
    return pl.pallas_call(
        functools.partial(_stick_kernel, t=tk),
        out_shape=jax.ShapeDtypeStruct((b, s, GROUP), BF16),
        grid=(b, s // tq),
        in_specs=[pl.BlockSpec((None, N_HEADS, tq, GROUP), lambda i, j: (i, 0, j, 0)),
                  pl.BlockSpec((None, nk, krows, tk), lambda i, j: (i, 0, 0, 0)),
                  pl.BlockSpec((None, s, GROUP), lambda i, j: (i, 0, 0)),
                  _gain_spec(layer, group)],
        out_specs=pl.BlockSpec((None, tq, GROUP), lambda i, j: (i, j, 0)),
        scratch_shapes=[pltpu.VMEM((rows, GROUP), F32),
                        pltpu.VMEM((rows, LANES), F32)],
        compiler_params=pltpu.CompilerParams(
            dimension_semantics=("parallel", "parallel"),
            vmem_limit_bytes=VMEM_LIMIT_BYTES),
        name="stick_attention",
    )(q, kt, v, gains)


def _retention_kernel(q_ref, k_ref, v_ref, g_ref, d_intra_ref, d_q_ref, d_kt_ref, d_state_ref,
                      gain_ref, o_ref, state_ref, *, t):
    @pl.when(pl.program_id(1) == 0)
    def _():
        state_ref[...] = jnp.zeros_like(state_ref)

    head = lax.broadcasted_iota(jnp.int32, (1, GROUP), 1) >> 6
    head_qk = (lax.broadcasted_iota(jnp.int32, (1, GROUP), 1) & (LANES - 1)) >> 5
    head_qk_r = (lax.broadcasted_iota(jnp.int32, (GROUP, 1), 0) & (LANES - 1)) >> 5
    same_head = head_qk_r == head

    state = state_ref[...]
    outs = []
    for sb in range(q_ref.shape[0] // t):
        rows = slice(sb * t, (sb + 1) * t)
        q = q_ref[rows]
        v = v_ref[rows]
        kt = k_ref[rows].T
        q_b = q.astype(BF16)
        zq = jnp.zeros_like(q_b)
        q4 = jnp.concatenate([jnp.where(head_qk == h, q_b, zq) for h in range(N_HEADS)], axis=0)
        w = (_dot(q4, kt.astype(BF16)) * d_intra_ref[...]).astype(BF16)
        wv = _dot(w, v)
        out = _head_select(head, [wv[h * t:(h + 1) * t] for h in range(N_HEADS)])
        outs.append(out + _dot((q * d_q_ref[...]).astype(BF16), state.astype(BF16)))
        kv = _dot((kt * d_kt_ref[...]).astype(BF16), v)
        state = state * d_state_ref[...] + jnp.where(same_head, kv, 0.0)
    state_ref[...] = state
    out = jnp.concatenate(outs, axis=0)

    head_r = lax.broadcasted_iota(jnp.int32, (GROUP, 1), 0) >> 6
    avg = jnp.where(head_r == head, 1.0 / HEAD_DIM, 0.0).astype(BF16)
    avg3 = jnp.concatenate([avg, avg, avg], axis=0)

    def head_mean(x):
        hi = x.astype(BF16)
        rest = x - hi.astype(F32)
        mid = rest.astype(BF16)
        lo = (rest - mid.astype(F32)).astype(BF16)
        return _dot(jnp.concatenate([hi, mid, lo], axis=-1), avg3)

    mu = head_mean(out)
    cen = out - mu
    var = head_mean(cen * cen)
    y = cen * lax.rsqrt(var + EPS) * gain_ref[...]
    g = g_ref[...]
    o_ref[...] = (y * (g * (1.0 / (1.0 + jnp.exp(-g))))).astype(o_ref.dtype)


def _retention_decay_tables(t):
    log_gamma = jnp.log1p(-jnp.power(2.0, -RET_DECAY_OFFSET - jnp.arange(N_HEADS, dtype=F32)))
    lg_v = jnp.repeat(log_gamma, HEAD_DIM)
    lg_qk = jnp.tile(jnp.repeat(log_gamma, HEAD_DIM // 2), 2)
    r = jnp.arange(t, dtype=F32)
    dist = jnp.abs(r[:, None] - r[None, :])
    reach = (jnp.arange(t)[None, :] // CHUNK) <= (jnp.arange(t)[:, None] // CHUNK)
    d_intra = jnp.where(reach[None], jnp.exp(log_gamma[:, None, None] * dist[None]), 0.0)
    d_q = jnp.exp(lg_qk[None, :] * (r[:, None] + 1.0))
    d_kt = jnp.exp(lg_qk[:, None] * (float(t - 1) - r)[None, :])
    d_state = jnp.exp(lg_v * float(t))[None, :]
    return d_intra.reshape(N_HEADS * t, t), d_q, d_kt, d_state


def _retention(q, k, v, g, tables, gains, layer, group):
    b, s, _ = q.shape
    t = tables[1].shape[0]
    tm = min(RET_BLOCKS_PER_STEP * t, s)
    row = pl.BlockSpec((None, tm, GROUP), lambda i, j: (i, j, 0))
    const = lambda a: pl.BlockSpec(a.shape, lambda i, j: (0, 0))
    return pl.pallas_call(
        functools.partial(_retention_kernel, t=t),
        out_shape=jax.ShapeDtypeStruct((b, s, GROUP), BF16),
        grid=(b, s // tm),
        in_specs=[row, row, row, row] + [const(a) for a in tables] + [_gain_spec(layer, group)],
        out_specs=row,
        scratch_shapes=[pltpu.VMEM((GROUP, GROUP), F32)],
        compiler_params=pltpu.CompilerParams(
            dimension_semantics=("parallel", "arbitrary"),
            vmem_limit_bytes=VMEM_LIMIT_BYTES),
        name="retention",
    )(q, k, v, g, *tables, gains)


def _post_kernel(x_ref, a_ref, b_ref, c_ref, d_ref, wo_ref, gpost_ref, gpre_ref,
                 wup_ref, wdn_ref, gfpost_ref, o_ref, *, f_chunk):
    mixed = jnp.concatenate([a_ref[...], b_ref[...], c_ref[...], d_ref[...]], axis=-1)
    x1 = x_ref[...] + _rms(_dot(mixed, wo_ref[...]), gpost_ref[...])
    h = _rms(x1, gpre_ref[...]).astype(BF16)
    y = None
    for c in range(wup_ref.shape[1] // f_chunk):
        u = jnp.maximum(_dot(h, wup_ref[:, c * f_chunk:(c + 1) * f_chunk]), 0.0)
        d = _dot((u * u).astype(BF16), wdn_ref[c * f_chunk:(c + 1) * f_chunk, :])
        y = d if y is None else y + d
    o_ref[...] = x1 + _rms(y, gfpost_ref[...])


def _post(x, a, bb, c, dd, layer, params):
    b, s, d = x.shape
    tm = min(TM_POST, s)
    row = lambda w: pl.BlockSpec((None, tm, w), lambda i, j: (i, j, 0))
    return pl.pallas_call(
        functools.partial(_post_kernel, f_chunk=1024),
        out_shape=jax.ShapeDtypeStruct((b, s, d), F32),
        grid=(b, s // tm),
        in_specs=([row(d), row(GROUP), row(GROUP), row(GROUP), row(GROUP)]
                  + [_layer_spec(p, layer) for p in params]),
        out_specs=row(d),
        compiler_params=pltpu.CompilerParams(
            dimension_semantics=("parallel", "parallel"),
            vmem_limit_bytes=VMEM_LIMIT_BYTES),
        name="out_proj_mlp",
    )(x, a, bb, c, dd, *params)


def _swap_halves(w, period):
    half = period // 2
    starts = range(0, w.shape[-1], period)
    return jnp.concatenate(
        [w[..., s0 + o:s0 + o + half] for s0 in starts for o in (half, 0)], axis=-1)


def _halves_first(w, period):
    half = period // 2
    starts = range(0, w.shape[-1], period)
    return jnp.concatenate(
        [w[..., s0 + o:s0 + o + half] for o in (0, half) for s0 in starts], axis=-1)


def _projection_weights(w_in, b_forget, w_q_up, w_kv_up):
    depth, d, _ = w_in.shape
    sizes = [GROUP, GROUP, GROUP, N_HEADS, 256, 128, MLA_ROPE] + [GROUP] * 7
    offs = np.concatenate([[0], np.cumsum(sizes)])
    (fq, fk, fv, ff, cq, ckv, kr, rq, rk, rv, rg, sq, sk, sv) = [
        w_in[..., int(offs[n]):int(offs[n + 1])] for n in range(len(sizes))]
    scale = HEAD_DIM ** -0.5
    misc = jnp.concatenate(
        [kr, _swap_halves(kr, MLA_ROPE), ff,
         jnp.zeros((depth, d, LANES - 2 * MLA_ROPE - N_HEADS), F32)], axis=-1)
    w_cat = jnp.concatenate(
        [fq * scale, fk, fv, cq, ckv, misc, _halves_first(rq, HEAD_DIM),
         _halves_first(rk, HEAD_DIM) * scale, rv, rg, sq * scale, sk, sv], axis=-1).astype(BF16)
    bf_pad = jnp.concatenate(
        [b_forget, jnp.zeros((depth, LANES - N_HEADS), F32)], axis=-1)[:, None, :]
    wq4 = w_q_up.reshape(depth, -1, N_HEADS, MLA_QK)
    q_nope = wq4[..., :HEAD_DIM].reshape(depth, -1, GROUP)
    q_rope = wq4[..., HEAD_DIM:].reshape(depth, -1, N_HEADS * MLA_ROPE)
    w_q = jnp.concatenate(
        [q_nope, q_rope, _swap_halves(q_rope, MLA_ROPE)], axis=-1).astype(BF16)
    wkv4 = w_kv_up.reshape(depth, -1, N_HEADS, 2 * HEAD_DIM)
    w_kv = jnp.concatenate([wkv4[..., :HEAD_DIM].reshape(depth, -1, GROUP),
                            wkv4[..., HEAD_DIM:].reshape(depth, -1, GROUP)],
                           axis=-1).astype(BF16)
    return w_cat, bf_pad, w_q, w_kv


def kernel(x, positions, g_mix_pre, w_in, b_forget, g_q_lora, w_q_up, g_kv_lora, w_kv_up,
           g_mix_out, w_out, g_mix_post, g_ffn_pre, w_ffn_up, w_ffn_down, g_ffn_post):
    depth = w_in.shape[0]
    vec = lambda g: g[:, None, :]
    w_cat, bf_pad, w_q, w_kv = _projection_weights(w_in, b_forget, w_q_up, w_kv_up)
    proj_params = (vec(g_mix_pre), w_cat, bf_pad, vec(g_q_lora), w_q, vec(g_kv_lora), w_kv)
    post_params = (w_out.astype(BF16), vec(g_mix_post), vec(g_ffn_pre),
                   w_ffn_up.astype(BF16), w_ffn_down.astype(BF16), vec(g_ffn_post))
    gains = g_mix_out.reshape(depth, 4, 1, GROUP)
    tab = _rope_tables(positions)
    ret_tables = _retention_decay_tables(min(T_RET, x.shape[1]))
    for layer in range(depth):
        (fq, fkt, fv, mq, mkt, mv, rq, rk, rv, rg, sq, skt, sv) = _projection(
            x, tab, layer, proj_params)
        out_a = _softmax_attention(fq, fkt, fv, gains, layer, 0,
                                   chunk_causal=False, name="fox_attention")
        out_b = _softmax_attention(mq, mkt, mv, gains, layer, 1,
                                   chunk_causal=True, name="mla_attention")
        out_c = _retention(rq, rk, rv, rg, ret_tables, gains, layer, 2)
        out_d = _stick_attention(sq, skt, sv, gains, layer, 3)
        x = _post(x, out_a, out_b, out_c, out_d, layer, post_params)
    return x
```

```python
import functools

import numpy as np
import jax
import jax.numpy as jnp
from jax import lax
from jax.experimental import pallas as pl
from jax.experimental.pallas import tpu as pltpu

F32 = jnp.float32
BF16 = jnp.bfloat16

N_HEADS = 4
HEAD_DIM = 64
GROUP = N_HEADS * HEAD_DIM
CHUNK = 64
CHUNK_SHIFT = 6
MLA_ROPE = 32
MLA_QK = HEAD_DIM + MLA_ROPE
ROPE_BASE = 10000.0
EPS = 1e-6
RET_DECAY_OFFSET = 5.0
NEG_BIG = -1e30
LOG2E = 1.4426950408889634
STICK_EXIT = 128.0

LANES = 128
VMEM_LIMIT_BYTES = 56 * 1024 * 1024

T_KEY = 512
PROJ_CHAIN_ROWS = 256
TM_POST = 512
POST_CHAINS = 2
TQ_ATTN = 512
T_STICK = 256
T_RET = 256
RET_BLOCKS_PER_STEP = 4
TM_TAB = 1024

C_FOX = 0
C_CQ = 768
C_CKV = 1024
C_MISC = 1152
C_RET = 1280
C_RVG = 1792
C_SB = 2304
C_END = 3072
MISC_ROT = 32
MISC_FF = 64


def _rms(x, g):
    return x * lax.rsqrt(jnp.mean(x * x, axis=-1, keepdims=True) + EPS) * g


def _log_sigmoid(x):
    return jnp.minimum(x, 0.0) - jnp.log1p(jnp.exp(-jnp.abs(x)))


def _dot(a, b):
    return jnp.dot(a, b, preferred_element_type=F32)


def _head_select(lane_head, per_head):
    out = per_head[N_HEADS - 1]
    for h in range(N_HEADS - 2, -1, -1):
        out = jnp.where(lane_head == h, per_head[h], out)
    return out


def _layer_spec(a, layer):
    zeros = (0,) * (a.ndim - 1)
    return pl.BlockSpec((None,) + a.shape[1:], lambda i, j: (layer,) + zeros)


def _gain_spec(layer, group):
    return pl.BlockSpec((None, None, 1, GROUP), lambda i, j: (layer, group, 0, 0))


def _per_head(fn, x, tq):
    return jnp.concatenate([fn(h, x[h * tq:(h + 1) * tq]) for h in range(N_HEADS)], axis=0)


def _table_kernel(pos_ref, invf_ref, sign_ref, tab_ref):
    half_r, half_m = HEAD_DIM // 2, MLA_ROPE // 2
    ang = pos_ref[...].astype(F32) * invf_ref[...]
    lane = lax.broadcasted_iota(jnp.int32, (1, LANES), 1)
    is_r = lane < half_r
    is_m = (lane >= half_r) & (lane < half_r + half_m)

    def spread_r(x):
        x = jnp.where(is_r, x, 0.0)
        x = x + pltpu.roll(x, half_r, 1)
        return x + pltpu.roll(x, 2 * half_r, 1)

    def spread_m(x):
        x = pltpu.roll(jnp.where(is_m, x, 0.0), LANES - half_r, 1)
        x = x + pltpu.roll(x, half_m, 1)
        x = x + pltpu.roll(x, 2 * half_m, 1)
        return x + pltpu.roll(x, 4 * half_m, 1)

    c = jnp.cos(ang)
    s = jnp.sin(ang)
    tab_ref[...] = jnp.concatenate(
        [spread_r(c), spread_r(s), spread_m(c), spread_m(s) * sign_ref[...]], axis=-1)


def _rope_tables(positions):
    b, s = positions.shape
    half_r, half_m = HEAD_DIM // 2, MLA_ROPE // 2
    invf_r = ROPE_BASE ** (-jnp.arange(half_r, dtype=F32) / half_r)
    invf_m = ROPE_BASE ** (-jnp.arange(half_m, dtype=F32) / half_m)
    invf = jnp.concatenate([invf_r, invf_m, jnp.zeros((LANES - half_r - half_m,), F32)])[None, :]
    lane = np.arange(LANES)
    sign = jnp.asarray(np.where(lane % MLA_ROPE < half_m, -1.0, 1.0), F32)[None, :]
    tm = min(TM_TAB, s)
    return pl.pallas_call(
        _table_kernel,
        out_shape=jax.ShapeDtypeStruct((b, s, 4 * LANES), F32),
        grid=(b, s // tm),
        in_specs=[pl.BlockSpec((None, tm, 1), lambda i, j: (i, j, 0)),
                  pl.BlockSpec((1, LANES), lambda i, j: (0, 0)),
                  pl.BlockSpec((1, LANES), lambda i, j: (0, 0))],
        out_specs=pl.BlockSpec((None, tm, 4 * LANES), lambda i, j: (i, j, 0)),
        compiler_params=pltpu.CompilerParams(
            dimension_semantics=("parallel", "parallel")),
        name="rope_tables",
    )(positions[:, :, None], invf, sign)


def _proj_kernel(x_ref, tab_ref, g_ref, w_ref, bf_ref, gq_ref, wq_ref, gkv_ref, wkv_ref,
                 fq_o, fkt_o, fv_o, mq_o, mkt_o, mv_o,
                 rq_o, rk_o, rv_o, rg_o, sq_o, skt_o, sv_o, carry_ref, *, mla_scale, chain_rows):
    n_chain = x_ref.shape[0] // chain_rows
    hs = [_rms(x_ref[k * chain_rows:(k + 1) * chain_rows], g_ref[...]).astype(BF16)
          for k in range(n_chain)]

    @pl.when(pl.program_id(1) == 0)
    def _():
        carry_ref[...] = jnp.zeros_like(carry_ref)

    for k in range(n_chain):
        rows = pl.ds(k * chain_rows, chain_rows)
        n_st = chain_rows // T_STICK
        _proj_chain(hs[k], tab_ref.at[rows], w_ref, bf_ref, gq_ref, wq_ref, gkv_ref, wkv_ref,
                    fq_o.at[:, rows], fkt_o.at[:, rows], fv_o.at[rows],
                    mq_o.at[:, rows], mkt_o.at[:, rows], mv_o.at[rows],
                    rq_o.at[rows], rk_o.at[rows], rv_o.at[rows], rg_o.at[rows],
                    sq_o.at[:, rows], skt_o.at[pl.ds(k * n_st, n_st)], sv_o.at[rows],
                    carry_ref, mla_scale=mla_scale)


def _proj_chain(h, tab_ref, w_ref, bf_ref, gq_ref, wq_ref, gkv_ref, wkv_ref,
                fq_o, fkt_o, fv_o, mq_o, mkt_o, mv_o,
                rq_o, rk_o, rv_o, rg_o, sq_o, skt_o, sv_o, carry_ref, *, mla_scale):
    tm = tab_ref.shape[0]

    def proj(a, b):
        return _dot(h, w_ref[:, a:b])

    tab = tab_ref[...]
    cos_r, sin_r = tab[:, 0:LANES], tab[:, LANES:2 * LANES]
    cos_m, sin_m = tab[:, 2 * LANES:3 * LANES], tab[:, 3 * LANES:]
    head64 = lax.broadcasted_iota(jnp.int32, (1, GROUP), 1) >> 6
    lane = lax.broadcasted_iota(jnp.int32, (1, LANES), 1)
    misc = proj(C_MISC, C_RET)

    def store_heads(o_ref, q):
        qb = q.astype(BF16)
        zero = jnp.zeros_like(qb)
        for hh in range(N_HEADS):
            o_ref[hh] = jnp.where(head64 == hh, qb, zero)

    ls = _log_sigmoid(pltpu.roll(misc, LANES - MISC_FF, 1) + bf_ref[...])
    row = lax.broadcasted_iota(jnp.int32, (tm, tm), 0)
    col = lax.broadcasted_iota(jnp.int32, (tm, tm), 1)
    tri = jnp.where(col <= row, 1.0, 0.0).astype(BF16)
    ls_hi = ls.astype(BF16)
    ls_rest = ls - ls_hi.astype(F32)
    ls_mid = ls_rest.astype(BF16)
    ls_lo = (ls_rest - ls_mid.astype(F32)).astype(BF16)
    parts = _dot(tri, jnp.concatenate([ls_hi, ls_mid, ls_lo], axis=-1))
    cum = (parts[:, :LANES] + parts[:, LANES:2 * LANES] + parts[:, 2 * LANES:]) + carry_ref[...]
    carry_ref[...] = cum[tm - 1:tm, :]
    c_hi = (cum * LOG2E).astype(BF16).astype(F32)
    c_rest = cum * LOG2E - c_hi
    c_mid = c_rest.astype(BF16).astype(F32)
    c_lo = (c_rest - c_mid).astype(BF16).astype(F32)
    term_of_lane = lane & 3
    spread = jnp.zeros((tm, LANES), F32)
    for hh in range(N_HEADS):
        for t, term in enumerate((c_hi, c_mid, c_lo)):
            val = jnp.broadcast_to(term[:, hh:hh + 1], (tm, LANES))
            spread = jnp.where(((lane >> 3) == hh) & (term_of_lane == t), val, spread)
    used = (lane < 8 * N_HEADS) & (term_of_lane < 3)
    first = used & ((lane & 4) == 0)
    second = used & ((lane & 4) != 0)
    gate_q = jnp.where(first, spread, jnp.where(second, 1.0, 0.0))
    gate_k = jnp.where(second, -spread, jnp.where(first, 1.0, 0.0))

    p = proj(C_FOX, C_CQ)
    fq = p[:, 0:GROUP] * LOG2E
    for hh in range(N_HEADS):
        pair = hh // 2
        nope = jnp.where((lane >> 6) == hh % 2, fq[:, pair * LANES:(pair + 1) * LANES], 0.0)
        gate = jnp.where((lane >> 3) == hh, gate_q, 0.0)
        fq_o[hh] = jnp.concatenate([nope, gate], axis=-1).astype(BF16)
    for pair in range(2):
        kp = jnp.concatenate([p[:, GROUP + pair * LANES:GROUP + (pair + 1) * LANES], gate_k],
                             axis=-1)
        fkt_o[pair * GROUP:(pair + 1) * GROUP, :] = kp.T.astype(BF16)
    fv_o[...] = p[:, 2 * GROUP:].astype(BF16)

    hq = _rms(proj(C_CQ, C_CKV), gq_ref[...]).astype(BF16)
    qm = _dot(hq, wq_ref[...]) * mla_scale
    q_nope = qm[:, :GROUP].astype(BF16)
    q_rope = (qm[:, GROUP:GROUP + LANES] * cos_m + qm[:, GROUP + LANES:] * sin_m).astype(BF16)
    zero = jnp.zeros_like(q_rope)
    for hh in range(N_HEADS):
        pair = hh // 2
        nope = jnp.where((lane >> 6) == hh % 2, q_nope[:, pair * LANES:(pair + 1) * LANES], zero)
        rope = jnp.where((lane >> 5) == hh, q_rope, zero)
        mq_o[hh] = jnp.concatenate([nope, rope], axis=-1)
    hkv = _rms(proj(C_CKV, C_MISC), gkv_ref[...]).astype(BF16)
    kv = _dot(hkv, wkv_ref[...])
    in_rope = lane < MLA_ROPE
    k_rope = (misc * jnp.where(in_rope, cos_m, 0.0)
              + pltpu.roll(misc, LANES - MISC_ROT, 1) * jnp.where(in_rope, sin_m, 0.0))
    k_rope = (k_rope + pltpu.roll(k_rope, MLA_ROPE, 1)
              + pltpu.roll(k_rope, 2 * MLA_ROPE, 1) + pltpu.roll(k_rope, 3 * MLA_ROPE, 1))
    for pair in range(2):
        kp = jnp.concatenate([kv[:, pair * LANES:(pair + 1) * LANES], k_rope], axis=-1)
        mkt_o[pair * GROUP:(pair + 1) * GROUP, :] = kp.T.astype(BF16)
    mv_o[...] = kv[:, GROUP:].astype(BF16)

    pr = proj(C_RET, C_RVG)

    def rotary(first, second):
        return jnp.concatenate([first * cos_r - second * sin_r, second * cos_r + first * sin_r],
                               axis=-1)

    rq_o[...] = rotary(pr[:, 0:LANES], pr[:, LANES:GROUP])
    rk_o[...] = rotary(pr[:, GROUP:GROUP + LANES], pr[:, GROUP + LANES:2 * GROUP])
    pv = proj(C_RVG, C_SB)
    rv_o[...] = pv[:, :GROUP].astype(BF16)
    rg_o[...] = pv[:, GROUP:]

    ps = proj(C_SB, C_END)
    store_heads(sq_o, ps[:, 0:GROUP])
    skt = ps[:, GROUP:2 * GROUP].T.astype(BF16)
    for c in range(tm // T_STICK):
        skt_o[c] = skt[:, c * T_STICK:(c + 1) * T_STICK]
    sv_o[...] = ps[:, 2 * GROUP:].astype(BF16)


def _projection(x, tab, layer, params):
    b, s, d = x.shape
    tm = min(T_KEY, s)
    nk = s // tm
    row = lambda w: pl.BlockSpec((None, tm, w), lambda i, j: (i, j, 0))
    heads = pl.BlockSpec((None, N_HEADS, tm, GROUP), lambda i, j: (i, 0, j, 0))
    keyt = lambda r: pl.BlockSpec((None, None, r, tm), lambda i, j: (i, j, 0, 0))
    bf = lambda w: jax.ShapeDtypeStruct((b, s, w), BF16)
    f32 = lambda w: jax.ShapeDtypeStruct((b, s, w), F32)
    q4 = jax.ShapeDtypeStruct((b, N_HEADS, s, GROUP), BF16)
    kt = lambda r: jax.ShapeDtypeStruct((b, nk, r, tm), BF16)
    n_st = tm // T_STICK
    skt_shape = jax.ShapeDtypeStruct((b, nk * n_st, GROUP, T_STICK), BF16)
    skt_spec = pl.BlockSpec((None, n_st, GROUP, T_STICK), lambda i, j: (i, j, 0, 0))
    out_shape = [q4, kt(2 * GROUP), bf(GROUP),
                 q4, kt(2 * GROUP), bf(GROUP),
                 f32(GROUP), f32(GROUP), bf(GROUP), f32(GROUP),
                 q4, skt_shape, bf(GROUP)]
    out_specs = [heads, keyt(2 * GROUP), row(GROUP),
                 heads, keyt(2 * GROUP), row(GROUP),
                 row(GROUP), row(GROUP), row(GROUP), row(GROUP),
                 heads, skt_spec, row(GROUP)]
    return pl.pallas_call(
        functools.partial(_proj_kernel, mla_scale=MLA_QK ** -0.5 * LOG2E,
                          chain_rows=min(PROJ_CHAIN_ROWS, tm)),
        out_shape=out_shape,
        grid=(b, nk),
        in_specs=[row(d), row(4 * LANES)] + [_layer_spec(a, layer) for a in params],
        out_specs=out_specs,
        scratch_shapes=[pltpu.VMEM((1, LANES), F32)],
        compiler_params=pltpu.CompilerParams(
            dimension_semantics=("parallel", "arbitrary"),
            vmem_limit_bytes=VMEM_LIMIT_BYTES),
        name="in_projection",
    )(x, tab, *params)


def _softmax_attn_kernel(q_ref, kt_ref, v_ref, gain_ref, o_ref, acc_ref, m_ref, l_ref,
                         *, chunk_causal, tq, tk):
    i = pl.program_id(1)
    groups = kt_ref.shape[1] // GROUP
    hpg = N_HEADS // groups

    def block(j, c0, nk, r0, masked, first):
        nr = tq - r0
        kt = kt_ref[j, :, c0:c0 + nk]
        if masked:
            qpos = i * tq + r0 + lax.broadcasted_iota(jnp.int32, (nr, nk), 0)
            kpos = j * tk + c0 + lax.broadcasted_iota(jnp.int32, (nr, nk), 1)
            if chunk_causal:
                visible = (kpos >> CHUNK_SHIFT) <= (qpos >> CHUNK_SHIFT)
            else:
                visible = kpos <= qpos
        s_parts = [_dot(q_ref[g * hpg:(g + 1) * hpg, r0:tq, :].reshape(hpg * nr, GROUP),
                        kt[g * GROUP:(g + 1) * GROUP]) for g in range(groups)]
        p_rows, alphas = [], []
        for h in range(N_HEADS):
            hr = slice(h * tq + r0, (h + 1) * tq)
            s = s_parts[h // hpg][(h % hpg) * nr:(h % hpg + 1) * nr]
            if masked:
                s = jnp.where(visible, s, NEG_BIG)
            m_new = jnp.broadcast_to(jnp.max(s, axis=-1, keepdims=True), (nr, LANES))
            l_new = None
            if not first:
                m_prev = m_ref[hr]
                m_new = jnp.maximum(m_prev, m_new)
                alpha = jnp.exp2(m_prev - m_new)
                alphas.append(jnp.concatenate([alpha, alpha], axis=-1))
                l_new = alpha * l_ref[hr]
            ps = []
            for c in range(nk // LANES):
                pc = jnp.exp2(s[:, c * LANES:(c + 1) * LANES] - m_new)
                l_new = pc if l_new is None else l_new + pc
                ps.append(pc.astype(BF16))
            l_ref[hr] = l_new
            m_ref[hr] = m_new
            p_rows.append(jnp.concatenate(ps, axis=-1))
        start = pl.multiple_of(j * tk + c0, nk)
        pv = _dot(jnp.concatenate(p_rows, axis=0), v_ref[pl.ds(start, nk), :])
        if first:
            acc_ref[...] = pv
        elif r0 == 0:
            acc_ref[...] = acc_ref[...] * jnp.concatenate(alphas, axis=0) + pv
        else:
            for h in range(N_HEADS):
                hr = slice(h * tq + r0, (h + 1) * tq)
                acc_ref[hr] = acc_ref[hr] * alphas[h] + pv[h * nr:(h + 1) * nr]

    n_full = (i * tq) // tk
    half = tk // 2
    block(n_full, 0, half, 0, True, True)
    block(n_full, half, half, tq // 2, True, False)

    def body(j, carry):
        block(j, 0, tk, 0, False, False)
        return carry

    lax.fori_loop(0, n_full, body, 0)

    inv_l = 1.0 / jnp.sum(l_ref[...], axis=-1, keepdims=True)
    o_all = acc_ref[...] * inv_l
    head_v = lax.broadcasted_iota(jnp.int32, (1, GROUP), 1) >> 6
    o = _head_select(head_v, [o_all[h * tq:(h + 1) * tq] for h in range(N_HEADS)])
    o_ref[...] = _rms(o, gain_ref[...]).astype(o_ref.dtype)


def _softmax_attention(q, kt, v, gains, layer, group, *, chunk_causal, name):
    b, _, s, _ = q.shape
    nk, krows, tk = kt.shape[1:]
    tq = min(TQ_ATTN, s)
    assert tq == tk and tk % (2 * CHUNK) == 0, (tq, tk)
    rows = N_HEADS * tq
    return pl.pallas_call(
        functools.partial(_softmax_attn_kernel, chunk_causal=chunk_causal, tq=tq, tk=tk),
        out_shape=jax.ShapeDtypeStruct((b, s, GROUP), BF16),
        grid=(b, s // tq),
        in_specs=[pl.BlockSpec((None, N_HEADS, tq, GROUP), lambda i, j: (i, 0, j, 0)),
                  pl.BlockSpec((None, nk, krows, tk), lambda i, j: (i, 0, 0, 0)),
                  pl.BlockSpec((None, s, GROUP), lambda i, j: (i, 0, 0)),
                  _gain_spec(layer, group)],
        out_specs=pl.BlockSpec((None, tq, GROUP), lambda i, j: (i, j, 0)),
        scratch_shapes=[pltpu.VMEM((rows, GROUP), F32),
                        pltpu.VMEM((rows, LANES), F32),
                        pltpu.VMEM((rows, LANES), F32)],
        compiler_params=pltpu.CompilerParams(
            dimension_semantics=("parallel", "parallel"),
            vmem_limit_bytes=VMEM_LIMIT_BYTES),
        name=name,
    )(q, kt, v, gains)


def _stick_kernel(q_ref, kt_ref, v_ref, gain_ref, o_ref, acc_ref, r_ref, *, t):
    i = pl.program_id(1)
    rows = N_HEADS * t
    q = q_ref[...].reshape(rows, GROUP)
    rj = lax.broadcasted_iota(jnp.int32, (t, t), 0)
    cs = lax.broadcasted_iota(jnp.int32, (t, t), 1)
    upper = jnp.where(rj > cs, 1.0, 0.0).astype(BF16)
    upper2 = jnp.concatenate([upper, upper], axis=0)
    visible = cs < rj

    def block(j, diagonal):
        z = _dot(q, kt_ref[j])
        leave = jnp.maximum(z, 0.0) + jnp.log(1.0 + jnp.exp(-jnp.abs(z)))
        if diagonal:
            leave = _per_head(lambda h, x: jnp.where(visible, x, 0.0), leave, t)
        hi = leave.astype(BF16)
        lo = (leave - hi.astype(F32)).astype(BF16)
        later = _dot(jnp.concatenate([hi, lo], axis=-1), upper2)
        row_sum = jnp.broadcast_to(jnp.sum(leave, axis=-1, keepdims=True), (rows, LANES))
        if diagonal:
            r_new = row_sum
        else:
            r_prev = r_ref[...]
            later = later + jnp.concatenate([r_prev] * (t // LANES), axis=-1)
            r_new = r_prev + row_sum
        w = jnp.exp(z - leave - later)
        if diagonal:
            w = _per_head(lambda h, x: jnp.where(visible, x, 0.0), w, t)
        r_ref[...] = r_new
        start = pl.multiple_of(j * t, t)
        pv = _dot(w.astype(BF16), v_ref[pl.ds(start, t), :])
        if diagonal:
            acc_ref[...] = pv
        else:
            acc_ref[...] += pv
        return jnp.min(r_new)

    def cond(carry):
        j, r_min = carry
        return jnp.logical_and(j >= 0, r_min < STICK_EXIT)

    def body(carry):
        j, _ = carry
        return j - 1, block(j, False)

    def finish():
        acc = acc_ref[...]
        head_v = lax.broadcasted_iota(jnp.int32, (1, GROUP), 1) >> 6
        o = _head_select(head_v, [acc[h * t:(h + 1) * t] for h in range(N_HEADS)])
        o_ref[...] = _rms(o, gain_ref[...]).astype(o_ref.dtype)

    @pl.when(i == 0)
    def _():
        block(0, True)
        finish()

    @pl.when(i > 0)
    def _():
        block(i, True)
        r_min = block(i - 1, False)
        lax.while_loop(cond, body, (i - 2, r_min))
        finish()


def _stick_attention(q, kt, v, gains, layer, group):
    b, _, s, _ = q.shape
    nk, krows, tk = kt.shape[1:]
    tq = tk
    rows = N_HEADS * tq
    return pl.pallas_call(
        functools.partial(_stick_kernel, t=tk),
        out_shape=jax.ShapeDtypeStruct((b, s, GROUP), BF16),
        grid=(b, s // tq),
        in_specs=[pl.BlockSpec((None, N_HEADS, tq, GROUP), lambda i, j: (i, 0, j, 0)),
                  pl.BlockSpec((None, nk, krows, tk), lambda i, j: (i, 0, 0, 0)),
                  pl.BlockSpec((None, s, GROUP), lambda i, j: (i, 0, 0)),
                  _gain_spec(layer, group)],
        out_specs=pl.BlockSpec((None, tq, GROUP), lambda i, j: (i, j, 0)),
        scratch_shapes=[pltpu.VMEM((rows, GROUP), F32),
                        pltpu.VMEM((rows, LANES), F32)],
        compiler_params=pltpu.CompilerParams(
            dimension_semantics=("parallel", "parallel"),
            vmem_limit_bytes=VMEM_LIMIT_BYTES),
        name="stick_attention",
    )(q, kt, v, gains)


def _retention_kernel(q_ref, k_ref, v_ref, g_ref, d_intra_ref, d_q_ref, d_kt_ref, d_state_ref,
                      gain_ref, o_ref, state_ref, *, t):
    @pl.when(pl.program_id(1) == 0)
    def _():
        state_ref[...] = jnp.zeros_like(state_ref)

    head = lax.broadcasted_iota(jnp.int32, (1, GROUP), 1) >> 6
    head_qk = (lax.broadcasted_iota(jnp.int32, (1, GROUP), 1) & (LANES - 1)) >> 5
    head_qk_r = (lax.broadcasted_iota(jnp.int32, (GROUP, 1), 0) & (LANES - 1)) >> 5
    same_head = head_qk_r == head

    state = state_ref[...]
    outs = []
    for sb in range(q_ref.shape[0] // t):
        rows = slice(sb * t, (sb + 1) * t)
        q = q_ref[rows]
        v = v_ref[rows]
        kt = k_ref[rows].T
        q_b = q.astype(BF16)
        zq = jnp.zeros_like(q_b)
        q4 = jnp.concatenate([jnp.where(head_qk == h, q_b, zq) for h in range(N_HEADS)], axis=0)
        w = (_dot(q4, kt.astype(BF16)) * d_intra_ref[...]).astype(BF16)
        wv = _dot(w, v)
        out = _head_select(head, [wv[h * t:(h + 1) * t] for h in range(N_HEADS)])
        outs.append(out + _dot((q * d_q_ref[...]).astype(BF16), state.astype(BF16)))
        kv = _dot((kt * d_kt_ref[...]).astype(BF16), v)
        state = state * d_state_ref[...] + jnp.where(same_head, kv, 0.0)
    state_ref[...] = state
    out = jnp.concatenate(outs, axis=0)

    head_r = lax.broadcasted_iota(jnp.int32, (GROUP, 1), 0) >> 6
    avg = jnp.where(head_r == head, 1.0 / HEAD_DIM, 0.0).astype(BF16)
    avg3 = jnp.concatenate([avg, avg, avg], axis=0)

    def head_mean(x):
        hi = x.astype(BF16)
        rest = x - hi.astype(F32)
        mid = rest.astype(BF16)
        lo = (rest - mid.astype(F32)).astype(BF16)
        return _dot(jnp.concatenate([hi, mid, lo], axis=-1), avg3)

    mu = head_mean(out)
    cen = out - mu
    var = head_mean(cen * cen)
    y = cen * lax.rsqrt(var + EPS) * gain_ref[...]
    g = g_ref[...]
    o_ref[...] = (y * (g * (1.0 / (1.0 + jnp.exp(-g))))).astype(o_ref.dtype)


def _retention_decay_tables(t):
    log_gamma = jnp.log1p(-jnp.power(2.0, -RET_DECAY_OFFSET - jnp.arange(N_HEADS, dtype=F32)))
    lg_v = jnp.repeat(log_gamma, HEAD_DIM)
    lg_qk = jnp.tile(jnp.repeat(log_gamma, HEAD_DIM // 2), 2)
    r = jnp.arange(t, dtype=F32)
    dist = jnp.abs(r[:, None] - r[None, :])
    reach = (jnp.arange(t)[None, :] // CHUNK) <= (jnp.arange(t)[:, None] // CHUNK)
    d_intra = jnp.where(reach[None], jnp.exp(log_gamma[:, None, None] * dist[None]), 0.0)
    d_q = jnp.exp(lg_qk[None, :] * (r[:, None] + 1.0))
    d_kt = jnp.exp(lg_qk[:, None] * (float(t - 1) - r)[None, :])
    d_state = jnp.exp(lg_v * float(t))[None, :]
    return d_intra.reshape(N_HEADS * t, t), d_q, d_kt, d_state


def _retention(q, k, v, g, tables, gains, layer, group):
    b, s, _ = q.shape
    t = tables[1].shape[0]
    tm = min(RET_BLOCKS_PER_STEP * t, s)
    row = pl.BlockSpec((None, tm, GROUP), lambda i, j: (i, j, 0))
    const = lambda a: pl.BlockSpec(a.shape, lambda i, j: (0, 0))
    return pl.pallas_call(
        functools.partial(_retention_kernel, t=t),
        out_shape=jax.ShapeDtypeStruct((b, s, GROUP), BF16),
        grid=(b, s // tm),
        in_specs=[row, row, row, row] + [const(a) for a in tables] + [_gain_spec(layer, group)],
        out_specs=row,
        scratch_shapes=[pltpu.VMEM((GROUP, GROUP), F32)],
        compiler_params=pltpu.CompilerParams(
            dimension_semantics=("parallel", "arbitrary"),
            vmem_limit_bytes=VMEM_LIMIT_BYTES),
        name="retention",
    )(q, k, v, g, *tables, gains)


def _post_kernel(x_ref, a_ref, b_ref, c_ref, d_ref, wo_ref, gpost_ref, gpre_ref,
                 wup_ref, wdn_ref, gfpost_ref, o_ref, *, f_chunk, chain_rows):
    n_chunks = wup_ref.shape[1] // f_chunk
    n_chain = x_ref.shape[0] // chain_rows

    def prologue(k):
        rows = slice(k * chain_rows, (k + 1) * chain_rows)
        mixed = jnp.concatenate([a_ref[rows], b_ref[rows], c_ref[rows], d_ref[rows]], axis=-1)
        x1 = x_ref[rows] + _rms(_dot(mixed, wo_ref[...]), gpost_ref[...])
        return x1, _rms(x1, gpre_ref[...]).astype(BF16)

    def chunk(h, c):
        u = jnp.maximum(_dot(h, wup_ref[:, c * f_chunk:(c + 1) * f_chunk]), 0.0)
        return _dot((u * u).astype(BF16), wdn_ref[c * f_chunk:(c + 1) * f_chunk, :])

    state = {}
    for step in range(n_chain + 1):
        if step < n_chain:
            state[step] = prologue(step)
        if step >= 1:
            k = step - 1
            x1, h = state.pop(k)
            y = chunk(h, 0)
            for c in range(1, n_chunks):
                y = y + chunk(h, c)
            rows = slice(k * chain_rows, (k + 1) * chain_rows)
            o_ref[rows] = x1 + _rms(y, gfpost_ref[...])


def _post(x, a, bb, c, dd, layer, params):
    b, s, d = x.shape
    tm = min(POST_CHAINS * TM_POST, s)
    row = lambda w: pl.BlockSpec((None, tm, w), lambda i, j: (i, j, 0))
    resident = lambda p: pl.BlockSpec((None,) + p.shape[1:],
                                      lambda i, j: (layer,) + (0,) * (p.ndim - 1),
                                      pipeline_mode=pl.Buffered(1))
    return pl.pallas_call(
        functools.partial(_post_kernel, f_chunk=1024, chain_rows=min(TM_POST, tm)),
        out_shape=jax.ShapeDtypeStruct((b, s, d), F32),
        grid=(b, s // tm),
        in_specs=([row(d), row(GROUP), row(GROUP), row(GROUP), row(GROUP)]
                  + [resident(p) for p in params]),
        out_specs=row(d),
        compiler_params=pltpu.CompilerParams(
            dimension_semantics=("parallel", "parallel"),
            vmem_limit_bytes=VMEM_LIMIT_BYTES),
        name="out_proj_mlp",
    )(x, a, bb, c, dd, *params)


def _swap_halves(w, period):
    half = period // 2
    starts = range(0, w.shape[-1], period)
    return jnp.concatenate(
        [w[..., s0 + o:s0 + o + half] for s0 in starts for o in (half, 0)], axis=-1)


def _halves_first(w, period):
    half = period // 2
    starts = range(0, w.shape[-1], period)
    return jnp.concatenate(
        [w[..., s0 + o:s0 + o + half] for o in (0, half) for s0 in starts], axis=-1)


def _projection_weights(w_in, b_forget, w_q_up, w_kv_up):
    depth, d, _ = w_in.shape
    sizes = [GROUP, GROUP, GROUP, N_HEADS, 256, 128, MLA_ROPE] + [GROUP] * 7
    offs = np.concatenate([[0], np.cumsum(sizes)])
    (fq, fk, fv, ff, cq, ckv, kr, rq, rk, rv, rg, sq, sk, sv) = [
        w_in[..., int(offs[n]):int(offs[n + 1])] for n in range(len(sizes))]
    scale = HEAD_DIM ** -0.5
    misc = jnp.concatenate(
        [kr, _swap_halves(kr, MLA_ROPE), ff,
         jnp.zeros((depth, d, LANES - 2 * MLA_ROPE - N_HEADS), F32)], axis=-1)
    w_cat = jnp.concatenate(
        [fq * scale, fk, fv, cq, ckv, misc, _halves_first(rq, HEAD_DIM),
         _halves_first(rk, HEAD_DIM) * scale, rv, rg, sq * scale, sk, sv], axis=-1).astype(BF16)
    bf_pad = jnp.concatenate(
        [b_forget, jnp.zeros((depth, LANES - N_HEADS), F32)], axis=-1)[:, None, :]
    wq4 = w_q_up.reshape(depth, -1, N_HEADS, MLA_QK)
    q_nope = wq4[..., :HEAD_DIM].reshape(depth, -1, GROUP)
    q_rope = wq4[..., HEAD_DIM:].reshape(depth, -1, N_HEADS * MLA_ROPE)
    w_q = jnp.concatenate(
        [q_nope, q_rope, _swap_halves(q_rope, MLA_ROPE)], axis=-1).astype(BF16)
    wkv4 = w_kv_up.reshape(depth, -1, N_HEADS, 2 * HEAD_DIM)
    w_kv = jnp.concatenate([wkv4[..., :HEAD_DIM].reshape(depth, -1, GROUP),
                            wkv4[..., HEAD_DIM:].reshape(depth, -1, GROUP)],
                           axis=-1).astype(BF16)
    return w_cat, bf_pad, w_q, w_kv


def kernel(x, positions, g_mix_pre, w_in, b_forget, g_q_lora, w_q_up, g_kv_lora, w_kv_up,
           g_mix_out, w_out, g_mix_post, g_ffn_pre, w_ffn_up, w_ffn_down, g_ffn_post):
    depth = w_in.shape[0]
    vec = lambda g: g[:, None, :]
    w_cat, bf_pad, w_q, w_kv = _projection_weights(w_in, b_forget, w_q_up, w_kv_up)
    proj_params = (vec(g_mix_pre), w_cat, bf_pad, vec(g_q_lora), w_q, vec(g_kv_lora), w_kv)
    post_params = (w_out.astype(BF16), vec(g_mix_post), vec(g_ffn_pre),
                   w_ffn_up.astype(BF16), w_ffn_down.astype(BF16), vec(g_ffn_post))
    gains = g_mix_out.reshape(depth, 4, 1, GROUP)
    tab = _rope_tables(positions)
    ret_tables = _retention_decay_tables(min(T_RET, x.shape[1]))
    for layer in range(depth):
        (fq, fkt, fv, mq, mkt, mv, rq, rk, rv, rg, sq, skt, sv) = _projection(
            x, tab, layer, proj_params)
        out_a = _softmax_attention(fq, fkt, fv, gains, layer, 0,
                                   chunk_causal=False, name="fox_attention")
        out_b = _softmax_attention(mq, mkt, mv, gains, layer, 1,
                                   chunk_causal=True, name="mla_attention")
        out_c = _retention(rq, rk, rv, rg, ret_tables, gains, layer, 2)
        out_d = _stick_attention(sq, skt, sv, gains, layer, 3)
        x = _post(x, out_a, out_b, out_c, out_d, layer, post_params)
    return x
```

```python
import functools

import numpy as np
import jax
import jax.numpy as jnp
from jax import lax
from jax.experimental import pallas as pl
from jax.experimental.pallas import tpu as pltpu

F32 = jnp.float32
BF16 = jnp.bfloat16

N_HEADS = 4
HEAD_DIM = 64
GROUP = N_HEADS * HEAD_DIM
CHUNK = 64
CHUNK_SHIFT = 6
MLA_ROPE = 32
MLA_QK = HEAD_DIM + MLA_ROPE
ROPE_BASE = 10000.0
EPS = 1e-6
RET_DECAY_OFFSET = 5.0
NEG_BIG = -1e30
LOG2E = 1.4426950408889634
STICK_EXIT = 128.0

LANES = 128
VMEM_LIMIT_BYTES = 56 * 1024 * 1024

T_KEY = 512
TM_POST = 512
POST_CHAINS = 2
TQ_ATTN = 512
T_STICK = 256
T_RET = 256
RET_BLOCKS_PER_STEP = 4
TM_TAB = 1024

C_FOX = 0
C_CQ = 768
C_CKV = 1024
C_MISC = 1152
C_RET = 1280
C_RVG = 1792
C_SB = 2304
C_END = 3072
MISC_ROT = 32
MISC_FF = 64


def _rms(x, g):
    return x * lax.rsqrt(jnp.mean(x * x, axis=-1, keepdims=True) + EPS) * g


def _log_sigmoid(x):
    return jnp.minimum(x, 0.0) - jnp.log1p(jnp.exp(-jnp.abs(x)))


def _dot(a, b):
    return jnp.dot(a, b, preferred_element_type=F32)


def _head_select(lane_head, per_head):
    out = per_head[N_HEADS - 1]
    for h in range(N_HEADS - 2, -1, -1):
        out = jnp.where(lane_head == h, per_head[h], out)
    return out


def _layer_spec(a, layer):
    zeros = (0,) * (a.ndim - 1)
    return pl.BlockSpec((None,) + a.shape[1:], lambda i, j: (layer,) + zeros)


def _gain_spec(layer, group):
    return pl.BlockSpec((None, None, 1, GROUP), lambda i, j: (layer, group, 0, 0))


def _per_head(fn, x, tq):
    return jnp.concatenate([fn(h, x[h * tq:(h + 1) * tq]) for h in range(N_HEADS)], axis=0)


def _table_kernel(pos_ref, invf_ref, sign_ref, tab_ref):
    half_r, half_m = HEAD_DIM // 2, MLA_ROPE // 2
    ang = pos_ref[...].astype(F32) * invf_ref[...]
    lane = lax.broadcasted_iota(jnp.int32, (1, LANES), 1)
    is_r = lane < half_r
    is_m = (lane >= half_r) & (lane < half_r + half_m)

    def spread_r(x):
        x = jnp.where(is_r, x, 0.0)
        x = x + pltpu.roll(x, half_r, 1)
        return x + pltpu.roll(x, 2 * half_r, 1)

    def spread_m(x):
        x = pltpu.roll(jnp.where(is_m, x, 0.0), LANES - half_r, 1)
        x = x + pltpu.roll(x, half_m, 1)
        x = x + pltpu.roll(x, 2 * half_m, 1)
        return x + pltpu.roll(x, 4 * half_m, 1)

    c = jnp.cos(ang)
    s = jnp.sin(ang)
    tab_ref[...] = jnp.concatenate(
        [spread_r(c), spread_r(s), spread_m(c), spread_m(s) * sign_ref[...]], axis=-1)


def _rope_tables(positions):
    b, s = positions.shape
    half_r, half_m = HEAD_DIM // 2, MLA_ROPE // 2
    invf_r = ROPE_BASE ** (-jnp.arange(half_r, dtype=F32) / half_r)
    invf_m = ROPE_BASE ** (-jnp.arange(half_m, dtype=F32) / half_m)
    invf = jnp.concatenate([invf_r, invf_m, jnp.zeros((LANES - half_r - half_m,), F32)])[None, :]
    lane = np.arange(LANES)
    sign = jnp.asarray(np.where(lane % MLA_ROPE < half_m, -1.0, 1.0), F32)[None, :]
    tm = min(TM_TAB, s)
    return pl.pallas_call(
        _table_kernel,
        out_shape=jax.ShapeDtypeStruct((b, s, 4 * LANES), F32),
        grid=(b, s // tm),
        in_specs=[pl.BlockSpec((None, tm, 1), lambda i, j: (i, j, 0)),
                  pl.BlockSpec((1, LANES), lambda i, j: (0, 0)),
                  pl.BlockSpec((1, LANES), lambda i, j: (0, 0))],
        out_specs=pl.BlockSpec((None, tm, 4 * LANES), lambda i, j: (i, j, 0)),
        compiler_params=pltpu.CompilerParams(
            dimension_semantics=("parallel", "parallel")),
        name="rope_tables",
    )(positions[:, :, None], invf, sign)


def _proj_kernel(x_ref, tab_ref, g_ref, w_ref, bf_ref, gq_ref, wq_ref, gkv_ref, wkv_ref,
                 fq_o, fkt_o, fv_o, mq_o, mkt_o, mv_o,
                 rq_o, rk_o, rv_o, rg_o, sq_o, skt_o, sv_o, carry_ref, *, mla_scale):
    tm = x_ref.shape[0]
    h = _rms(x_ref[...], g_ref[...]).astype(BF16)

    def proj(a, b):
        return _dot(h, w_ref[:, a:b])

    tab = tab_ref[...]
    cos_r, sin_r = tab[:, 0:LANES], tab[:, LANES:2 * LANES]
    cos_m, sin_m = tab[:, 2 * LANES:3 * LANES], tab[:, 3 * LANES:]
    head64 = lax.broadcasted_iota(jnp.int32, (1, GROUP), 1) >> 6
    lane = lax.broadcasted_iota(jnp.int32, (1, LANES), 1)
    misc = proj(C_MISC, C_RET)

    def store_heads(o_ref, q):
        qb = q.astype(BF16)
        zero = jnp.zeros_like(qb)
        for hh in range(N_HEADS):
            o_ref[hh] = jnp.where(head64 == hh, qb, zero)

    @pl.when(pl.program_id(1) == 0)
    def _():
        carry_ref[...] = jnp.zeros_like(carry_ref)

    ls = _log_sigmoid(pltpu.roll(misc, LANES - MISC_FF, 1) + bf_ref[...])
    row = lax.broadcasted_iota(jnp.int32, (tm, tm), 0)
    col = lax.broadcasted_iota(jnp.int32, (tm, tm), 1)
    tri = jnp.where(col <= row, 1.0, 0.0).astype(BF16)
    ls_hi = ls.astype(BF16)
    ls_rest = ls - ls_hi.astype(F32)
    ls_mid = ls_rest.astype(BF16)
    ls_lo = (ls_rest - ls_mid.astype(F32)).astype(BF16)
    parts = _dot(tri, jnp.concatenate([ls_hi, ls_mid, ls_lo], axis=-1))
    cum = (parts[:, :LANES] + parts[:, LANES:2 * LANES] + parts[:, 2 * LANES:]) + carry_ref[...]
    carry_ref[...] = cum[tm - 1:tm, :]
    c_hi = (cum * LOG2E).astype(BF16).astype(F32)
    c_rest = cum * LOG2E - c_hi
    c_mid = c_rest.astype(BF16).astype(F32)
    c_lo = (c_rest - c_mid).astype(BF16).astype(F32)
    term_of_lane = lane & 3
    spread = jnp.zeros((tm, LANES), F32)
    for hh in range(N_HEADS):
        for t, term in enumerate((c_hi, c_mid, c_lo)):
            val = jnp.broadcast_to(term[:, hh:hh + 1], (tm, LANES))
            spread = jnp.where(((lane >> 3) == hh) & (term_of_lane == t), val, spread)
    used = (lane < 8 * N_HEADS) & (term_of_lane < 3)
    first = used & ((lane & 4) == 0)
    second = used & ((lane & 4) != 0)
    gate_q = jnp.where(first, spread, jnp.where(second, 1.0, 0.0))
    gate_k = jnp.where(second, -spread, jnp.where(first, 1.0, 0.0))

    p = proj(C_FOX, C_CQ)
    fq = p[:, 0:GROUP] * LOG2E
    for hh in range(N_HEADS):
        pair = hh // 2
        nope = jnp.where((lane >> 6) == hh % 2, fq[:, pair * LANES:(pair + 1) * LANES], 0.0)
        gate = jnp.where((lane >> 3) == hh, gate_q, 0.0)
        fq_o[hh] = jnp.concatenate([nope, gate], axis=-1).astype(BF16)
    for pair in range(2):
        kp = jnp.concatenate([p[:, GROUP + pair * LANES:GROUP + (pair + 1) * LANES], gate_k],
                             axis=-1)
        fkt_o[pair * GROUP:(pair + 1) * GROUP, :] = kp.T.astype(BF16)
    fv_o[...] = p[:, 2 * GROUP:].astype(BF16)

    hq = _rms(proj(C_CQ, C_CKV), gq_ref[...]).astype(BF16)
    qm = _dot(hq, wq_ref[...]) * mla_scale
    q_nope = qm[:, :GROUP].astype(BF16)
    q_rope = (qm[:, GROUP:GROUP + LANES] * cos_m + qm[:, GROUP + LANES:] * sin_m).astype(BF16)
    zero = jnp.zeros_like(q_rope)
    for hh in range(N_HEADS):
        pair = hh // 2
        nope = jnp.where((lane >> 6) == hh % 2, q_nope[:, pair * LANES:(pair + 1) * LANES], zero)
        rope = jnp.where((lane >> 5) == hh, q_rope, zero)
        mq_o[hh] = jnp.concatenate([nope, rope], axis=-1)
    hkv = _rms(proj(C_CKV, C_MISC), gkv_ref[...]).astype(BF16)
    kv = _dot(hkv, wkv_ref[...])
    in_rope = lane < MLA_ROPE
    k_rope = (misc * jnp.where(in_rope, cos_m, 0.0)
              + pltpu.roll(misc, LANES - MISC_ROT, 1) * jnp.where(in_rope, sin_m, 0.0))
    k_rope = (k_rope + pltpu.roll(k_rope, MLA_ROPE, 1)
              + pltpu.roll(k_rope, 2 * MLA_ROPE, 1) + pltpu.roll(k_rope, 3 * MLA_ROPE, 1))
    for pair in range(2):
        kp = jnp.concatenate([kv[:, pair * LANES:(pair + 1) * LANES], k_rope], axis=-1)
        mkt_o[pair * GROUP:(pair + 1) * GROUP, :] = kp.T.astype(BF16)
    mv_o[...] = kv[:, GROUP:].astype(BF16)

    pr = proj(C_RET, C_RVG)

    def rotary(first, second):
        return jnp.concatenate([first * cos_r - second * sin_r, second * cos_r + first * sin_r],
                               axis=-1)

    rq_o[...] = rotary(pr[:, 0:LANES], pr[:, LANES:GROUP])
    rk_o[...] = rotary(pr[:, GROUP:GROUP + LANES], pr[:, GROUP + LANES:2 * GROUP])
    pv = proj(C_RVG, C_SB)
    rv_o[...] = pv[:, :GROUP].astype(BF16)
    rg_o[...] = pv[:, GROUP:]

    ps = proj(C_SB, C_END)
    store_heads(sq_o, ps[:, 0:GROUP])
    skt = ps[:, GROUP:2 * GROUP].T.astype(BF16)
    for c in range(tm // T_STICK):
        skt_o[c] = skt[:, c * T_STICK:(c + 1) * T_STICK]
    sv_o[...] = ps[:, 2 * GROUP:].astype(BF16)


def _projection(x, tab, layer, params):
    b, s, d = x.shape
    tm = min(T_KEY, s)
    nk = s // tm
    row = lambda w: pl.BlockSpec((None, tm, w), lambda i, j: (i, j, 0))
    heads = pl.BlockSpec((None, N_HEADS, tm, GROUP), lambda i, j: (i, 0, j, 0))
    keyt = lambda r: pl.BlockSpec((None, None, r, tm), lambda i, j: (i, j, 0, 0))
    bf = lambda w: jax.ShapeDtypeStruct((b, s, w), BF16)
    f32 = lambda w: jax.ShapeDtypeStruct((b, s, w), F32)
    q4 = jax.ShapeDtypeStruct((b, N_HEADS, s, GROUP), BF16)
    kt = lambda r: jax.ShapeDtypeStruct((b, nk, r, tm), BF16)
    n_st = tm // T_STICK
    skt_shape = jax.ShapeDtypeStruct((b, nk * n_st, GROUP, T_STICK), BF16)
    skt_spec = pl.BlockSpec((None, n_st, GROUP, T_STICK), lambda i, j: (i, j, 0, 0))
    out_shape = [q4, kt(2 * GROUP), bf(GROUP),
                 q4, kt(2 * GROUP), bf(GROUP),
                 f32(GROUP), f32(GROUP), bf(GROUP), f32(GROUP),
                 q4, skt_shape, bf(GROUP)]
    out_specs = [heads, keyt(2 * GROUP), row(GROUP),
                 heads, keyt(2 * GROUP), row(GROUP),
                 row(GROUP), row(GROUP), row(GROUP), row(GROUP),
                 heads, skt_spec, row(GROUP)]
    return pl.pallas_call(
        functools.partial(_proj_kernel, mla_scale=MLA_QK ** -0.5 * LOG2E),
        out_shape=out_shape,
        grid=(b, nk),
        in_specs=[row(d), row(4 * LANES)] + [_layer_spec(a, layer) for a in params],
        out_specs=out_specs,
        scratch_shapes=[pltpu.VMEM((1, LANES), F32)],
        compiler_params=pltpu.CompilerParams(
            dimension_semantics=("parallel", "arbitrary"),
            vmem_limit_bytes=VMEM_LIMIT_BYTES),
        name="in_projection",
    )(x, tab, *params)


def _softmax_attn_kernel(q_ref, kt_ref, v_ref, gain_ref, o_ref, acc_ref, m_ref, l_ref,
                         *, chunk_causal, tq, tk):
    i = pl.program_id(1)
    groups = kt_ref.shape[1] // GROUP
    hpg = N_HEADS // groups

    def block(j, c0, nk, r0, masked, first):
        nr = tq - r0
        kt = kt_ref[j, :, c0:c0 + nk]
        if masked:
            qpos = i * tq + r0 + lax.broadcasted_iota(jnp.int32, (nr, nk), 0)
            kpos = j * tk + c0 + lax.broadcasted_iota(jnp.int32, (nr, nk), 1)
            if chunk_causal:
                visible = (kpos >> CHUNK_SHIFT) <= (qpos >> CHUNK_SHIFT)
            else:
                visible = kpos <= qpos
        s_parts = [_dot(q_ref[g * hpg:(g + 1) * hpg, r0:tq, :].reshape(hpg * nr, GROUP),
                        kt[g * GROUP:(g + 1) * GROUP]) for g in range(groups)]
        p_rows, alphas = [], []
        for h in range(N_HEADS):
            hr = slice(h * tq + r0, (h + 1) * tq)
            s = s_parts[h // hpg][(h % hpg) * nr:(h % hpg + 1) * nr]
            if masked:
                s = jnp.where(visible, s, NEG_BIG)
            m_new = jnp.broadcast_to(jnp.max(s, axis=-1, keepdims=True), (nr, LANES))
            l_new = None
            if not first:
                m_prev = m_ref[hr]
                m_new = jnp.maximum(m_prev, m_new)
                alpha = jnp.exp2(m_prev - m_new)
                alphas.append(jnp.concatenate([alpha, alpha], axis=-1))
                l_new = alpha * l_ref[hr]
            ps = []
            for c in range(nk // LANES):
                pc = jnp.exp2(s[:, c * LANES:(c + 1) * LANES] - m_new)
                l_new = pc if l_new is None else l_new + pc
                ps.append(pc.astype(BF16))
            l_ref[hr] = l_new
            m_ref[hr] = m_new
            p_rows.append(jnp.concatenate(ps, axis=-1))
        start = pl.multiple_of(j * tk + c0, nk)
        pv = _dot(jnp.concatenate(p_rows, axis=0), v_ref[pl.ds(start, nk), :])
        if first:
            acc_ref[...] = pv
        elif r0 == 0:
            acc_ref[...] = acc_ref[...] * jnp.concatenate(alphas, axis=0) + pv
        else:
            for h in range(N_HEADS):
                hr = slice(h * tq + r0, (h + 1) * tq)
                acc_ref[hr] = acc_ref[hr] * alphas[h] + pv[h * nr:(h + 1) * nr]

    n_full = (i * tq) // tk
    half = tk // 2
    block(n_full, 0, half, 0, True, True)
    block(n_full, half, half, tq // 2, True, False)

    def body(j, carry):
        block(j, 0, tk, 0, False, False)
        return carry

    lax.fori_loop(0, n_full, body, 0)

    inv_l = 1.0 / jnp.sum(l_ref[...], axis=-1, keepdims=True)
    o_all = acc_ref[...] * inv_l
    head_v = lax.broadcasted_iota(jnp.int32, (1, GROUP), 1) >> 6
    o = _head_select(head_v, [o_all[h * tq:(h + 1) * tq] for h in range(N_HEADS)])
    o_ref[...] = _rms(o, gain_ref[...]).astype(o_ref.dtype)


def _softmax_attention(q, kt, v, gains, layer, group, *, chunk_causal, name):
    b, _, s, _ = q.shape
    nk, krows, tk = kt.shape[1:]
    tq = min(TQ_ATTN, s)
    assert tq == tk and tk % (2 * CHUNK) == 0, (tq, tk)
    rows = N_HEADS * tq
    return pl.pallas_call(
        functools.partial(_softmax_attn_kernel, chunk_causal=chunk_causal, tq=tq, tk=tk),
        out_shape=jax.ShapeDtypeStruct((b, s, GROUP), BF16),
        grid=(b, s // tq),
        in_specs=[pl.BlockSpec((None, N_HEADS, tq, GROUP), lambda i, j: (i, 0, j, 0)),
                  pl.BlockSpec((None, nk, krows, tk), lambda i, j: (i, 0, 0, 0)),
                  pl.BlockSpec((None, s, GROUP), lambda i, j: (i, 0, 0)),
                  _gain_spec(layer, group)],
        out_specs=pl.BlockSpec((None, tq, GROUP), lambda i, j: (i, j, 0)),
        scratch_shapes=[pltpu.VMEM((rows, GROUP), F32),
                        pltpu.VMEM((rows, LANES), F32),
                        pltpu.VMEM((rows, LANES), F32)],
        compiler_params=pltpu.CompilerParams(
            dimension_semantics=("parallel", "parallel"),
            vmem_limit_bytes=VMEM_LIMIT_BYTES),
        name=name,
    )(q, kt, v, gains)


def _stick_kernel(q_ref, kt_ref, v_ref, gain_ref, o_ref, acc_ref, r_ref, *, t):
    i = pl.program_id(1)
    rows = N_HEADS * t
    q = q_ref[...].reshape(rows, GROUP)
    rj = lax.broadcasted_iota(jnp.int32, (t, t), 0)
    cs = lax.broadcasted_iota(jnp.int32, (t, t), 1)
    upper = jnp.where(rj > cs, 1.0, 0.0).astype(BF16)
    upper2 = jnp.concatenate([upper, upper], axis=0)
    visible = cs < rj

    def block(j, diagonal):
        z = _dot(q, kt_ref[j])
        leave = jnp.maximum(z, 0.0) + jnp.log(1.0 + jnp.exp(-jnp.abs(z)))
        if diagonal:
            leave = _per_head(lambda h, x: jnp.where(visible, x, 0.0), leave, t)
        hi = leave.astype(BF16)
        lo = (leave - hi.astype(F32)).astype(BF16)
        later = _dot(jnp.concatenate([hi, lo], axis=-1), upper2)
        row_sum = jnp.broadcast_to(jnp.sum(leave, axis=-1, keepdims=True), (rows, LANES))
        if diagonal:
            r_new = row_sum
        else:
            r_prev = r_ref[...]
            later = later + jnp.concatenate([r_prev] * (t // LANES), axis=-1)
            r_new = r_prev + row_sum
        w = jnp.exp(z - leave - later)
        if diagonal:
            w = _per_head(lambda h, x: jnp.where(visible, x, 0.0), w, t)
        r_ref[...] = r_new
        start = pl.multiple_of(j * t, t)
        pv = _dot(w.astype(BF16), v_ref[pl.ds(start, t), :])
        if diagonal:
            acc_ref[...] = pv
        else:
            acc_ref[...] += pv
        return jnp.min(r_new)

    def cond(carry):
        j, r_min = carry
        return jnp.logical_and(j >= 0, r_min < STICK_EXIT)

    def body(carry):
        j, _ = carry
        return j - 1, block(j, False)

    def finish():
        acc = acc_ref[...]
        head_v = lax.broadcasted_iota(jnp.int32, (1, GROUP), 1) >> 6
        o = _head_select(head_v, [acc[h * t:(h + 1) * t] for h in range(N_HEADS)])
        o_ref[...] = _rms(o, gain_ref[...]).astype(o_ref.dtype)

    @pl.when(i == 0)
    def _():
        block(0, True)
        finish()

    @pl.when(i > 0)
    def _():
        block(i, True)
        r_min = block(i - 1, False)
        lax.while_loop(cond, body, (i - 2, r_min))
        finish()


def _stick_attention(q, kt, v, gains, layer, group):
    b, _, s, _ = q.shape
    nk, krows, tk = kt.shape[1:]
    tq = tk
    rows = N_HEADS * tq
    return pl.pallas_call(
        functools.partial(_stick_kernel, t=tk),
        out_shape=jax.ShapeDtypeStruct((b, s, GROUP), BF16),
        grid=(b, s // tq),
        in_specs=[pl.BlockSpec((None, N_HEADS, tq, GROUP), lambda i, j: (i, 0, j, 0)),
                  pl.BlockSpec((None, nk, krows, tk), lambda i, j: (i, 0, 0, 0)),
                  pl.BlockSpec((None, s, GROUP), lambda i, j: (i, 0, 0)),
                  _gain_spec(layer, group)],
        out_specs=pl.BlockSpec((None, tq, GROUP), lambda i, j: (i, j, 0)),
        scratch_shapes=[pltpu.VMEM((rows, GROUP), F32),
                        pltpu.VMEM((rows, LANES), F32)],
        compiler_params=pltpu.CompilerParams(
            dimension_semantics=("parallel", "parallel"),
            vmem_limit_bytes=VMEM_LIMIT_BYTES),
        name="stick_attention",
    )(q, kt, v, gains)


def _retention_kernel(q_ref, k_ref, v_ref, g_ref, d_intra_ref, d_q_ref, d_kt_ref, d_state_ref,
                      gain_ref, o_ref, state_ref, *, t):
    @pl.when(pl.program_id(1) == 0)
    def _():
        state_ref[...] = jnp.zeros_like(state_ref)

    head = lax.broadcasted_iota(jnp.int32, (1, GROUP), 1) >> 6
    head_qk = (lax.broadcasted_iota(jnp.int32, (1, GROUP), 1) & (LANES - 1)) >> 5
    head_qk_r = (lax.broadcasted_iota(jnp.int32, (GROUP, 1), 0) & (LANES - 1)) >> 5
    same_head = head_qk_r == head

    state = state_ref[...]
    outs = []
    for sb in range(q_ref.shape[0] // t):
        rows = slice(sb * t, (sb + 1) * t)
        q = q_ref[rows]
        v = v_ref[rows]
        kt = k_ref[rows].T
        q_b = q.astype(BF16)
        zq = jnp.zeros_like(q_b)
        q4 = jnp.concatenate([jnp.where(head_qk == h, q_b, zq) for h in range(N_HEADS)], axis=0)
        w = (_dot(q4, kt.astype(BF16)) * d_intra_ref[...]).astype(BF16)
        wv = _dot(w, v)
        out = _head_select(head, [wv[h * t:(h + 1) * t] for h in range(N_HEADS)])
        outs.append(out + _dot((q * d_q_ref[...]).astype(BF16), state.astype(BF16)))
        kv = _dot((kt * d_kt_ref[...]).astype(BF16), v)
        state = state * d_state_ref[...] + jnp.where(same_head, kv, 0.0)
    state_ref[...] = state
    out = jnp.concatenate(outs, axis=0)

    head_r = lax.broadcasted_iota(jnp.int32, (GROUP, 1), 0) >> 6
    avg = jnp.where(head_r == head, 1.0 / HEAD_DIM, 0.0).astype(BF16)
    avg3 = jnp.concatenate([avg, avg, avg], axis=0)

    def head_mean(x):
        hi = x.astype(BF16)
        rest = x - hi.astype(F32)
        mid = rest.astype(BF16)
        lo = (rest - mid.astype(F32)).astype(BF16)
        return _dot(jnp.concatenate([hi, mid, lo], axis=-1), avg3)

    mu = head_mean(out)
    cen = out - mu
    var = head_mean(cen * cen)
    y = cen * lax.rsqrt(var + EPS) * gain_ref[...]
    g = g_ref[...]
    o_ref[...] = (y * (g * (1.0 / (1.0 + jnp.exp(-g))))).astype(o_ref.dtype)


def _retention_decay_tables(t):
    log_gamma = jnp.log1p(-jnp.power(2.0, -RET_DECAY_OFFSET - jnp.arange(N_HEADS, dtype=F32)))
    lg_v = jnp.repeat(log_gamma, HEAD_DIM)
    lg_qk = jnp.tile(jnp.repeat(log_gamma, HEAD_DIM // 2), 2)
    r = jnp.arange(t, dtype=F32)
    dist = jnp.abs(r[:, None] - r[None, :])
    reach = (jnp.arange(t)[None, :] // CHUNK) <= (jnp.arange(t)[:, None] // CHUNK)
    d_intra = jnp.where(reach[None], jnp.exp(log_gamma[:, None, None] * dist[None]), 0.0)
    d_q = jnp.exp(lg_qk[None, :] * (r[:, None] + 1.0))
    d_kt = jnp.exp(lg_qk[:, None] * (float(t - 1) - r)[None, :])
    d_state = jnp.exp(lg_v * float(t))[None, :]
    return d_intra.reshape(N_HEADS * t, t), d_q, d_kt, d_state


def _retention(q, k, v, g, tables, gains, layer, group):
    b, s, _ = q.shape
    t = tables[1].shape[0]
    tm = min(RET_BLOCKS_PER_STEP * t, s)
    row = pl.BlockSpec((None, tm, GROUP), lambda i, j: (i, j, 0))
    const = lambda a: pl.BlockSpec(a.shape, lambda i, j: (0, 0))
    return pl.pallas_call(
        functools.partial(_retention_kernel, t=t),
        out_shape=jax.ShapeDtypeStruct((b, s, GROUP), BF16),
        grid=(b, s // tm),
        in_specs=[row, row, row, row] + [const(a) for a in tables] + [_gain_spec(layer, group)],
        out_specs=row,
        scratch_shapes=[pltpu.VMEM((GROUP, GROUP), F32)],
        compiler_params=pltpu.CompilerParams(
            dimension_semantics=("parallel", "arbitrary"),
            vmem_limit_bytes=VMEM_LIMIT_BYTES),
        name="retention",
    )(q, k, v, g, *tables, gains)


def _post_kernel(x_ref, a_ref, b_ref, c_ref, d_ref, wo_ref, gpost_ref, gpre_ref,
                 wup_ref, wdn_ref, gfpost_ref, o_ref, *, f_chunk, chain_rows):
    n_chunks = wup_ref.shape[1] // f_chunk
    n_chain = x_ref.shape[0] // chain_rows

    def prologue(k):
        rows = slice(k * chain_rows, (k + 1) * chain_rows)
        mixed = jnp.concatenate([a_ref[rows], b_ref[rows], c_ref[rows], d_ref[rows]], axis=-1)
        x1 = x_ref[rows] + _rms(_dot(mixed, wo_ref[...]), gpost_ref[...])
        return x1, _rms(x1, gpre_ref[...]).astype(BF16)

    def chunk(h, c):
        u = jnp.maximum(_dot(h, wup_ref[:, c * f_chunk:(c + 1) * f_chunk]), 0.0)
        return _dot((u * u).astype(BF16), wdn_ref[c * f_chunk:(c + 1) * f_chunk, :])

    state = {}
    for step in range(n_chain + 1):
        if step < n_chain:
            state[step] = prologue(step)
        if step >= 1:
            k = step - 1
            x1, h = state.pop(k)
            y = chunk(h, 0)
            for c in range(1, n_chunks):
                y = y + chunk(h, c)
            rows = slice(k * chain_rows, (k + 1) * chain_rows)
            o_ref[rows] = x1 + _rms(y, gfpost_ref[...])


def _post(x, a, bb, c, dd, layer, params):
    b, s, d = x.shape
    tm = min(POST_CHAINS * TM_POST, s)
    row = lambda w: pl.BlockSpec((None, tm, w), lambda i, j: (i, j, 0))
    resident = lambda p: pl.BlockSpec((None,) + p.shape[1:],
                                      lambda i, j: (layer,) + (0,) * (p.ndim - 1),
                                      pipeline_mode=pl.Buffered(1))
    return pl.pallas_call(
        functools.partial(_post_kernel, f_chunk=1024, chain_rows=min(TM_POST, tm)),
        out_shape=jax.ShapeDtypeStruct((b, s, d), F32),
        grid=(b, s // tm),
        in_specs=([row(d), row(GROUP), row(GROUP), row(GROUP), row(GROUP)]
                  + [resident(p) for p in params]),
        out_specs=row(d),
        compiler_params=pltpu.CompilerParams(
            dimension_semantics=("parallel", "parallel"),
            vmem_limit_bytes=VMEM_LIMIT_BYTES),
        name="out_proj_mlp",
    )(x, a, bb, c, dd, *params)


def _swap_halves(w, period):
    half = period // 2
    starts = range(0, w.shape[-1], period)
    return jnp.concatenate(
        [w[..., s0 + o:s0 + o + half] for s0 in starts for o in (half, 0)], axis=-1)


def _halves_first(w, period):
    half = period // 2
    starts = range(0, w.shape[-1], period)
    return jnp.concatenate(
        [w[..., s0 + o:s0 + o + half] for o in (0, half) for s0 in starts], axis=-1)


def _projection_weights(w_in, b_forget, w_q_up, w_kv_up):
    depth, d, _ = w_in.shape
    sizes = [GROUP, GROUP, GROUP, N_HEADS, 256, 128, MLA_ROPE] + [GROUP] * 7
    offs = np.concatenate([[0], np.cumsum(sizes)])
    (fq, fk, fv, ff, cq, ckv, kr, rq, rk, rv, rg, sq, sk, sv) = [
        w_in[..., int(offs[n]):int(offs[n + 1])] for n in range(len(sizes))]
    scale = HEAD_DIM ** -0.5
    misc = jnp.concatenate(
        [kr, _swap_halves(kr, MLA_ROPE), ff,
         jnp.zeros((depth, d, LANES - 2 * MLA_ROPE - N_HEADS), F32)], axis=-1)
    w_cat = jnp.concatenate(
        [fq * scale, fk, fv, cq, ckv, misc, _halves_first(rq, HEAD_DIM),
         _halves_first(rk, HEAD_DIM) * scale, rv, rg, sq * scale, sk, sv], axis=-1).astype(BF16)
    bf_pad = jnp.concatenate(
        [b_forget, jnp.zeros((depth, LANES - N_HEADS), F32)], axis=-1)[:, None, :]
    wq4 = w_q_up.reshape(depth, -1, N_HEADS, MLA_QK)
    q_nope = wq4[..., :HEAD_DIM].reshape(depth, -1, GROUP)
    q_rope = wq4[..., HEAD_DIM:].reshape(depth, -1, N_HEADS * MLA_ROPE)
    w_q = jnp.concatenate(
        [q_nope, q_rope, _swap_halves(q_rope, MLA_ROPE)], axis=-1).astype(BF16)
    wkv4 = w_kv_up.reshape(depth, -1, N_HEADS, 2 * HEAD_DIM)
    w_kv = jnp.concatenate([wkv4[..., :HEAD_DIM].reshape(depth, -1, GROUP),
                            wkv4[..., HEAD_DIM:].reshape(depth, -1, GROUP)],
                           axis=-1).astype(BF16)
    return w_cat, bf_pad, w_q, w_kv


def kernel(x, positions, g_mix_pre, w_in, b_forget, g_q_lora, w_q_up, g_kv_lora, w_kv_up,
           g_mix_out, w_out, g_mix_post, g_ffn_pre, w_ffn_up, w_ffn_down, g_ffn_post):
    depth = w_in.shape[0]
    vec = lambda g: g[:, None, :]
    w_cat, bf_pad, w_q, w_kv = _projection_weights(w_in, b_forget, w_q_up, w_kv_up)
    proj_params = (vec(g_mix_pre), w_cat, bf_pad, vec(g_q_lora), w_q, vec(g_kv_lora), w_kv)
    post_params = (w_out.astype(BF16), vec(g_mix_post), vec(g_ffn_pre),
                   w_ffn_up.astype(BF16), w_ffn_down.astype(BF16), vec(g_ffn_post))
    gains = g_mix_out.reshape(depth, 4, 1, GROUP)
    tab = _rope_tables(positions)
    ret_tables = _retention_decay_tables(min(T_RET, x.shape[1]))
    for layer in range(depth):
        (fq, fkt, fv, mq, mkt, mv, rq, rk, rv, rg, sq, skt, sv) = _projection(
            x, tab, layer, proj_params)
        out_a = _softmax_attention(fq, fkt, fv, gains, layer, 0,
                                   chunk_causal=False, name="fox_attention")
        out_b = _softmax_attention(mq, mkt, mv, gains, layer, 1,
                                   chunk_causal=True, name="mla_attention")
        out_c = _retention(rq, rk, rv, rg, ret_tables, gains, layer, 2)
        out_d = _stick_attention(sq, skt, sv, gains, layer, 3)
        x = _post(x, out_a, out_b, out_c, out_d, layer, post_params)
    return x
```

```python
import functools

import numpy as np
import jax
import jax.numpy as jnp
from jax import lax
from jax.experimental import pallas as pl
from jax.experimental.pallas import tpu as pltpu

F32 = jnp.float32
BF16 = jnp.bfloat16

N_HEADS = 4
HEAD_DIM = 64
GROUP = N_HEADS * HEAD_DIM
CHUNK = 64
CHUNK_SHIFT = 6
MLA_ROPE = 32
MLA_QK = HEAD_DIM + MLA_ROPE
ROPE_BASE = 10000.0
EPS = 1e-6
RET_DECAY_OFFSET = 5.0
NEG_BIG = -1e30
LOG2E = 1.4426950408889634
STICK_EXIT = 128.0

LANES = 128
VMEM_LIMIT_BYTES = 56 * 1024 * 1024

T_KEY = 512
TM_POST = 512
POST_CHAINS = 2
TQ_ATTN = 512
T_STICK = 256
T_RET = 256
RET_BLOCKS_PER_STEP = 4
TM_TAB = 1024

C_FOX = 0
C_CQ = 768
C_CKV = 1024
C_MISC = 1152
C_RET = 1280
C_RVG = 1792
C_SB = 2304
C_END = 3072
MISC_ROT = 32
MISC_FF = 64


def _rms(x, g):
    return x * lax.rsqrt(jnp.mean(x * x, axis=-1, keepdims=True) + EPS) * g


def _log_sigmoid(x):
    return jnp.minimum(x, 0.0) - jnp.log1p(jnp.exp(-jnp.abs(x)))


def _dot(a, b):
    return jnp.dot(a, b, preferred_element_type=F32)


def _head_select(lane_head, per_head):
    out = per_head[N_HEADS - 1]
    for h in range(N_HEADS - 2, -1, -1):
        out = jnp.where(lane_head == h, per_head[h], out)
    return out


def _layer_spec(a, layer):
    zeros = (0,) * (a.ndim - 1)
    return pl.BlockSpec((None,) + a.shape[1:], lambda i, j: (layer,) + zeros)


def _gain_spec(layer, group):
    return pl.BlockSpec((None, None, 1, GROUP), lambda i, j: (layer, group, 0, 0))


def _per_head(fn, x, tq):
    return jnp.concatenate([fn(h, x[h * tq:(h + 1) * tq]) for h in range(N_HEADS)], axis=0)


def _table_kernel(pos_ref, invf_ref, sign_ref, tab_ref):
    half_r, half_m = HEAD_DIM // 2, MLA_ROPE // 2
    ang = pos_ref[...].astype(F32) * invf_ref[...]
    lane = lax.broadcasted_iota(jnp.int32, (1, LANES), 1)
    is_r = lane < half_r
    is_m = (lane >= half_r) & (lane < half_r + half_m)

    def spread_r(x):
        x = jnp.where(is_r, x, 0.0)
        x = x + pltpu.roll(x, half_r, 1)
        return x + pltpu.roll(x, 2 * half_r, 1)

    def spread_m(x):
        x = pltpu.roll(jnp.where(is_m, x, 0.0), LANES - half_r, 1)
        x = x + pltpu.roll(x, half_m, 1)
        x = x + pltpu.roll(x, 2 * half_m, 1)
        return x + pltpu.roll(x, 4 * half_m, 1)

    c = jnp.cos(ang)
    s = jnp.sin(ang)
    tab_ref[...] = jnp.concatenate(
        [spread_r(c), spread_r(s), spread_m(c), spread_m(s) * sign_ref[...]], axis=-1)


def _rope_tables(positions):
    b, s = positions.shape
    half_r, half_m = HEAD_DIM // 2, MLA_ROPE // 2
    invf_r = ROPE_BASE ** (-jnp.arange(half_r, dtype=F32) / half_r)
    invf_m = ROPE_BASE ** (-jnp.arange(half_m, dtype=F32) / half_m)
    invf = jnp.concatenate([invf_r, invf_m, jnp.zeros((LANES - half_r - half_m,), F32)])[None, :]
    lane = np.arange(LANES)
    sign = jnp.asarray(np.where(lane % MLA_ROPE < half_m, -1.0, 1.0), F32)[None, :]
    tm = min(TM_TAB, s)
    return pl.pallas_call(
        _table_kernel,
        out_shape=jax.ShapeDtypeStruct((b, s, 4 * LANES), F32),
        grid=(b, s // tm),
        in_specs=[pl.BlockSpec((None, tm, 1), lambda i, j: (i, j, 0)),
                  pl.BlockSpec((1, LANES), lambda i, j: (0, 0)),
                  pl.BlockSpec((1, LANES), lambda i, j: (0, 0))],
        out_specs=pl.BlockSpec((None, tm, 4 * LANES), lambda i, j: (i, j, 0)),
        compiler_params=pltpu.CompilerParams(
            dimension_semantics=("parallel", "parallel")),
        name="rope_tables",
    )(positions[:, :, None], invf, sign)


def _proj_kernel(x_ref, tab_ref, g_ref, w_ref, bf_ref, gq_ref, wq_ref, gkv_ref, wkv_ref,
                 fq_o, fkt_o, fv_o, mq_o, mkt_o, mv_o,
                 rq_o, rk_o, rv_o, rg_o, sq_o, skt_o, sv_o, carry_ref, *, mla_scale):
    tm = x_ref.shape[0]
    h = _rms(x_ref[...], g_ref[...]).astype(BF16)

    def proj(a, b):
        return _dot(h, w_ref[:, a:b])

    tab = tab_ref[...]
    cos_r, sin_r = tab[:, 0:LANES], tab[:, LANES:2 * LANES]
    cos_m, sin_m = tab[:, 2 * LANES:3 * LANES], tab[:, 3 * LANES:]
    head64 = lax.broadcasted_iota(jnp.int32, (1, GROUP), 1) >> 6
    lane = lax.broadcasted_iota(jnp.int32, (1, LANES), 1)
    misc = proj(C_MISC, C_RET)

    def store_heads(o_ref, q):
        qb = q.astype(BF16)
        zero = jnp.zeros_like(qb)
        for hh in range(N_HEADS):
            o_ref[hh] = jnp.where(head64 == hh, qb, zero)

    @pl.when(pl.program_id(1) == 0)
    def _():
        carry_ref[...] = jnp.zeros_like(carry_ref)

    ls = _log_sigmoid(pltpu.roll(misc, LANES - MISC_FF, 1) + bf_ref[...])
    row = lax.broadcasted_iota(jnp.int32, (tm, tm), 0)
    col = lax.broadcasted_iota(jnp.int32, (tm, tm), 1)
    tri = jnp.where(col <= row, 1.0, 0.0).astype(BF16)
    ls_hi = ls.astype(BF16)
    ls_rest = ls - ls_hi.astype(F32)
    ls_mid = ls_rest.astype(BF16)
    ls_lo = (ls_rest - ls_mid.astype(F32)).astype(BF16)
    parts = _dot(tri, jnp.concatenate([ls_hi, ls_mid, ls_lo], axis=-1))
    cum = (parts[:, :LANES] + parts[:, LANES:2 * LANES] + parts[:, 2 * LANES:]) + carry_ref[...]
    carry_ref[...] = cum[tm - 1:tm, :]
    c_hi = (cum * LOG2E).astype(BF16).astype(F32)
    c_rest = cum * LOG2E - c_hi
    c_mid = c_rest.astype(BF16).astype(F32)
    c_lo = (c_rest - c_mid).astype(BF16).astype(F32)
    term_of_lane = lane & 3
    spread = jnp.zeros((tm, LANES), F32)
    for hh in range(N_HEADS):
        for t, term in enumerate((c_hi, c_mid, c_lo)):
            val = jnp.broadcast_to(term[:, hh:hh + 1], (tm, LANES))
            spread = jnp.where(((lane >> 3) == hh) & (term_of_lane == t), val, spread)
    used = (lane < 8 * N_HEADS) & (term_of_lane < 3)
    first = used & ((lane & 4) == 0)
    second = used & ((lane & 4) != 0)
    gate_q = jnp.where(first, spread, jnp.where(second, 1.0, 0.0))
    gate_k = jnp.where(second, -spread, jnp.where(first, 1.0, 0.0))

    hq = _rms(proj(C_CQ, C_CKV), gq_ref[...]).astype(BF16)
    qm = _dot(hq, wq_ref[...]) * mla_scale
    q_nope = qm[:, :GROUP].astype(BF16)
    q_rope = (qm[:, GROUP:GROUP + LANES] * cos_m + qm[:, GROUP + LANES:] * sin_m).astype(BF16)
    zero = jnp.zeros_like(q_rope)
    for hh in range(N_HEADS):
        pair = hh // 2
        nope = jnp.where((lane >> 6) == hh % 2, q_nope[:, pair * LANES:(pair + 1) * LANES], zero)
        rope = jnp.where((lane >> 5) == hh, q_rope, zero)
        mq_o[hh] = jnp.concatenate([nope, rope], axis=-1)
    hkv = _rms(proj(C_CKV, C_MISC), gkv_ref[...]).astype(BF16)
    kv = _dot(hkv, wkv_ref[...])
    in_rope = lane < MLA_ROPE
    k_rope = (misc * jnp.where(in_rope, cos_m, 0.0)
              + pltpu.roll(misc, LANES - MISC_ROT, 1) * jnp.where(in_rope, sin_m, 0.0))
    k_rope = (k_rope + pltpu.roll(k_rope, MLA_ROPE, 1)
              + pltpu.roll(k_rope, 2 * MLA_ROPE, 1) + pltpu.roll(k_rope, 3 * MLA_ROPE, 1))
    for pair in range(2):
        kp = jnp.concatenate([kv[:, pair * LANES:(pair + 1) * LANES], k_rope], axis=-1)
        mkt_o[pair * GROUP:(pair + 1) * GROUP, :] = kp.T.astype(BF16)
    mv_o[...] = kv[:, GROUP:].astype(BF16)

    pr = proj(C_RET, C_RVG)

    def rotary(first, second):
        return jnp.concatenate([first * cos_r - second * sin_r, second * cos_r + first * sin_r],
                               axis=-1)

    rq_o[...] = rotary(pr[:, 0:LANES], pr[:, LANES:GROUP])
    rk_o[...] = rotary(pr[:, GROUP:GROUP + LANES], pr[:, GROUP + LANES:2 * GROUP])
    pv = proj(C_RVG, C_SB)
    rv_o[...] = pv[:, :GROUP].astype(BF16)
    rg_o[...] = pv[:, GROUP:]

    ps = proj(C_SB, C_END)
    store_heads(sq_o, ps[:, 0:GROUP])
    skt = ps[:, GROUP:2 * GROUP].T.astype(BF16)
    for c in range(tm // T_STICK):
        skt_o[c] = skt[:, c * T_STICK:(c + 1) * T_STICK]
    sv_o[...] = ps[:, 2 * GROUP:].astype(BF16)

    p = proj(C_FOX, C_CQ)
    fq = p[:, 0:GROUP] * LOG2E
    for hh in range(N_HEADS):
        pair = hh // 2
        nope = jnp.where((lane >> 6) == hh % 2, fq[:, pair * LANES:(pair + 1) * LANES], 0.0)
        gate = jnp.where((lane >> 3) == hh, gate_q, 0.0)
        fq_o[hh] = jnp.concatenate([nope, gate], axis=-1).astype(BF16)
    for pair in range(2):
        kp = jnp.concatenate([p[:, GROUP + pair * LANES:GROUP + (pair + 1) * LANES], gate_k],
                             axis=-1)
        fkt_o[pair * GROUP:(pair + 1) * GROUP, :] = kp.T.astype(BF16)
    fv_o[...] = p[:, 2 * GROUP:].astype(BF16)


def _projection(x, tab, layer, params):
    b, s, d = x.shape
    tm = min(T_KEY, s)
    nk = s // tm
    row = lambda w: pl.BlockSpec((None, tm, w), lambda i, j: (i, j, 0))
    heads = pl.BlockSpec((None, N_HEADS, tm, GROUP), lambda i, j: (i, 0, j, 0))
    keyt = lambda r: pl.BlockSpec((None, None, r, tm), lambda i, j: (i, j, 0, 0))
    bf = lambda w: jax.ShapeDtypeStruct((b, s, w), BF16)
    f32 = lambda w: jax.ShapeDtypeStruct((b, s, w), F32)
    q4 = jax.ShapeDtypeStruct((b, N_HEADS, s, GROUP), BF16)
    kt = lambda r: jax.ShapeDtypeStruct((b, nk, r, tm), BF16)
    n_st = tm // T_STICK
    skt_shape = jax.ShapeDtypeStruct((b, nk * n_st, GROUP, T_STICK), BF16)
    skt_spec = pl.BlockSpec((None, n_st, GROUP, T_STICK), lambda i, j: (i, j, 0, 0))
    out_shape = [q4, kt(2 * GROUP), bf(GROUP),
                 q4, kt(2 * GROUP), bf(GROUP),
                 f32(GROUP), f32(GROUP), bf(GROUP), f32(GROUP),
                 q4, skt_shape, bf(GROUP)]
    out_specs = [heads, keyt(2 * GROUP), row(GROUP),
                 heads, keyt(2 * GROUP), row(GROUP),
                 row(GROUP), row(GROUP), row(GROUP), row(GROUP),
                 heads, skt_spec, row(GROUP)]
    return pl.pallas_call(
        functools.partial(_proj_kernel, mla_scale=MLA_QK ** -0.5 * LOG2E),
        out_shape=out_shape,
        grid=(b, nk),
        in_specs=[row(d), row(4 * LANES)] + [_layer_spec(a, layer) for a in params],
        out_specs=out_specs,
        scratch_shapes=[pltpu.VMEM((1, LANES), F32)],
        compiler_params=pltpu.CompilerParams(
            dimension_semantics=("parallel", "arbitrary"),
            vmem_limit_bytes=VMEM_LIMIT_BYTES),
        name="in_projection",
    )(x, tab, *params)


def _softmax_attn_kernel(q_ref, kt_ref, v_ref, gain_ref, o_ref, acc_ref, m_ref, l_ref,
                         *, chunk_causal, tq, tk):
    i = pl.program_id(1)
    groups = kt_ref.shape[1] // GROUP
    hpg = N_HEADS // groups

    def block(j, c0, nk, r0, masked, first):
        nr = tq - r0
        kt = kt_ref[j, :, c0:c0 + nk]
        if masked:
            qpos = i * tq + r0 + lax.broadcasted_iota(jnp.int32, (nr, nk), 0)
            kpos = j * tk + c0 + lax.broadcasted_iota(jnp.int32, (nr, nk), 1)
            if chunk_causal:
                visible = (kpos >> CHUNK_SHIFT) <= (qpos >> CHUNK_SHIFT)
            else:
                visible = kpos <= qpos
        s_parts = [_dot(q_ref[g * hpg:(g + 1) * hpg, r0:tq, :].reshape(hpg * nr, GROUP),
                        kt[g * GROUP:(g + 1) * GROUP]) for g in range(groups)]
        p_rows, alphas = [], []
        for h in range(N_HEADS):
            hr = slice(h * tq + r0, (h + 1) * tq)
            s = s_parts[h // hpg][(h % hpg) * nr:(h % hpg + 1) * nr]
            if masked:
                s = jnp.where(visible, s, NEG_BIG)
            m_new = jnp.broadcast_to(jnp.max(s, axis=-1, keepdims=True), (nr, LANES))
            l_new = None
            if not first:
                m_prev = m_ref[hr]
                m_new = jnp.maximum(m_prev, m_new)
                alpha = jnp.exp2(m_prev - m_new)
                alphas.append(jnp.concatenate([alpha, alpha], axis=-1))
                l_new = alpha * l_ref[hr]
            ps = []
            for c in range(nk // LANES):
                pc = jnp.exp2(s[:, c * LANES:(c + 1) * LANES] - m_new)
                l_new = pc if l_new is None else l_new + pc
                ps.append(pc.astype(BF16))
            l_ref[hr] = l_new
            m_ref[hr] = m_new
            p_rows.append(jnp.concatenate(ps, axis=-1))
        start = pl.multiple_of(j * tk + c0, nk)
        pv = _dot(jnp.concatenate(p_rows, axis=0), v_ref[pl.ds(start, nk), :])
        if first:
            acc_ref[...] = pv
        elif r0 == 0:
            acc_ref[...] = acc_ref[...] * jnp.concatenate(alphas, axis=0) + pv
        else:
            for h in range(N_HEADS):
                hr = slice(h * tq + r0, (h + 1) * tq)
                acc_ref[hr] = acc_ref[hr] * alphas[h] + pv[h * nr:(h + 1) * nr]

    n_full = (i * tq) // tk
    half = tk // 2
    block(n_full, 0, half, 0, True, True)
    block(n_full, half, half, tq // 2, True, False)

    def body(j, carry):
        block(j, 0, tk, 0, False, False)
        return carry

    lax.fori_loop(0, n_full, body, 0)

    inv_l = 1.0 / jnp.sum(l_ref[...], axis=-1, keepdims=True)
    o_all = acc_ref[...] * inv_l
    head_v = lax.broadcasted_iota(jnp.int32, (1, GROUP), 1) >> 6
    o = _head_select(head_v, [o_all[h * tq:(h + 1) * tq] for h in range(N_HEADS)])
    o_ref[...] = _rms(o, gain_ref[...]).astype(o_ref.dtype)


def _softmax_attention(q, kt, v, gains, layer, group, *, chunk_causal, name):
    b, _, s, _ = q.shape
    nk, krows, tk = kt.shape[1:]
    tq = min(TQ_ATTN, s)
    assert tq == tk and tk % (2 * CHUNK) == 0, (tq, tk)
    rows = N_HEADS * tq
    return pl.pallas_call(
        functools.partial(_softmax_attn_kernel, chunk_causal=chunk_causal, tq=tq, tk=tk),
        out_shape=jax.ShapeDtypeStruct((b, s, GROUP), BF16),
        grid=(b, s // tq),
        in_specs=[pl.BlockSpec((None, N_HEADS, tq, GROUP), lambda i, j: (i, 0, j, 0)),
                  pl.BlockSpec((None, nk, krows, tk), lambda i, j: (i, 0, 0, 0)),
                  pl.BlockSpec((None, s, GROUP), lambda i, j: (i, 0, 0)),
                  _gain_spec(layer, group)],
        out_specs=pl.BlockSpec((None, tq, GROUP), lambda i, j: (i, j, 0)),
        scratch_shapes=[pltpu.VMEM((rows, GROUP), F32),
                        pltpu.VMEM((rows, LANES), F32),
                        pltpu.VMEM((rows, LANES), F32)],
        compiler_params=pltpu.CompilerParams(
            dimension_semantics=("parallel", "parallel"),
            vmem_limit_bytes=VMEM_LIMIT_BYTES),
        name=name,
    )(q, kt, v, gains)


def _stick_kernel(q_ref, kt_ref, v_ref, gain_ref, o_ref, acc_ref, r_ref, *, t):
    i = pl.program_id(1)
    rows = N_HEADS * t
    q = q_ref[...].reshape(rows, GROUP)
    rj = lax.broadcasted_iota(jnp.int32, (t, t), 0)
    cs = lax.broadcasted_iota(jnp.int32, (t, t), 1)
    upper = jnp.where(rj > cs, 1.0, 0.0).astype(BF16)
    upper2 = jnp.concatenate([upper, upper], axis=0)
    visible = cs < rj

    def block(j, diagonal):
        z = _dot(q, kt_ref[j])
        leave = jnp.maximum(z, 0.0) + jnp.log(1.0 + jnp.exp(-jnp.abs(z)))
        if diagonal:
            leave = _per_head(lambda h, x: jnp.where(visible, x, 0.0), leave, t)
        hi = leave.astype(BF16)
        lo = (leave - hi.astype(F32)).astype(BF16)
        later = _dot(jnp.concatenate([hi, lo], axis=-1), upper2)
        row_sum = jnp.broadcast_to(jnp.sum(leave, axis=-1, keepdims=True), (rows, LANES))
        if diagonal:
            r_new = row_sum
        else:
            r_prev = r_ref[...]
            later = later + jnp.concatenate([r_prev] * (t // LANES), axis=-1)
            r_new = r_prev + row_sum
        w = jnp.exp(z - leave - later)
        if diagonal:
            w = _per_head(lambda h, x: jnp.where(visible, x, 0.0), w, t)
        r_ref[...] = r_new
        start = pl.multiple_of(j * t, t)
        pv = _dot(w.astype(BF16), v_ref[pl.ds(start, t), :])
        if diagonal:
            acc_ref[...] = pv
        else:
            acc_ref[...] += pv
        return jnp.min(r_new)

    def cond(carry):
        j, r_min = carry
        return jnp.logical_and(j >= 0, r_min < STICK_EXIT)

    def body(carry):
        j, _ = carry
        return j - 1, block(j, False)

    def finish():
        acc = acc_ref[...]
        head_v = lax.broadcasted_iota(jnp.int32, (1, GROUP), 1) >> 6
        o = _head_select(head_v, [acc[h * t:(h + 1) * t] for h in range(N_HEADS)])
        o_ref[...] = _rms(o, gain_ref[...]).astype(o_ref.dtype)

    @pl.when(i == 0)
    def _():
        block(0, True)
        finish()

    @pl.when(i > 0)
    def _():
        block(i, True)
        r_min = block(i - 1, False)
        lax.while_loop(cond, body, (i - 2, r_min))
        finish()


def _stick_attention(q, kt, v, gains, layer, group):
    b, _, s, _ = q.shape
    nk, krows, tk = kt.shape[1:]
    tq = tk
    rows = N_HEADS * tq
    return pl.pallas_call(
        functools.partial(_stick_kernel, t=tk),
        out_shape=jax.ShapeDtypeStruct((b, s, GROUP), BF16),
        grid=(b, s // tq),
        in_specs=[pl.BlockSpec((None, N_HEADS, tq, GROUP), lambda i, j: (i, 0, j, 0)),
                  pl.BlockSpec((None, nk, krows, tk), lambda i, j: (i, 0, 0, 0)),
                  pl.BlockSpec((None, s, GROUP), lambda i, j: (i, 0, 0)),
                  _gain_spec(layer, group)],
        out_specs=pl.BlockSpec((None, tq, GROUP), lambda i, j: (i, j, 0)),
        scratch_shapes=[pltpu.VMEM((rows, GROUP), F32),
                        pltpu.VMEM((rows, LANES), F32)],
        compiler_params=pltpu.CompilerParams(
            dimension_semantics=("parallel", "parallel"),
            vmem_limit_bytes=VMEM_LIMIT_BYTES),
        name="stick_attention",
    )(q, kt, v, gains)


def _retention_kernel(q_ref, k_ref, v_ref, g_ref, d_intra_ref, d_q_ref, d_kt_ref, d_state_ref,
                      gain_ref, o_ref, state_ref, *, t):
    @pl.when(pl.program_id(1) == 0)
    def _():
        state_ref[...] = jnp.zeros_like(state_ref)

    head = lax.broadcasted_iota(jnp.int32, (1, GROUP), 1) >> 6
    head_qk = (lax.broadcasted_iota(jnp.int32, (1, GROUP), 1) & (LANES - 1)) >> 5
    head_qk_r = (lax.broadcasted_iota(jnp.int32, (GROUP, 1), 0) & (LANES - 1)) >> 5
    same_head = head_qk_r == head

    state = state_ref[...]
    outs = []
    for sb in range(q_ref.shape[0] // t):
        rows = slice(sb * t, (sb + 1) * t)
        q = q_ref[rows]
        v = v_ref[rows]
        kt = k_ref[rows].T
        q_b = q.astype(BF16)
        zq = jnp.zeros_like(q_b)
        q4 = jnp.concatenate([jnp.where(head_qk == h, q_b, zq) for h in range(N_HEADS)], axis=0)
        w = (_dot(q4, kt.astype(BF16)) * d_intra_ref[...]).astype(BF16)
        wv = _dot(w, v)
        out = _head_select(head, [wv[h * t:(h + 1) * t] for h in range(N_HEADS)])
        outs.append(out + _dot((q * d_q_ref[...]).astype(BF16), state.astype(BF16)))
        kv = _dot((kt * d_kt_ref[...]).astype(BF16), v)
        state = state * d_state_ref[...] + jnp.where(same_head, kv, 0.0)
    state_ref[...] = state
    out = jnp.concatenate(outs, axis=0)

    head_r = lax.broadcasted_iota(jnp.int32, (GROUP, 1), 0) >> 6
    avg = jnp.where(head_r == head, 1.0 / HEAD_DIM, 0.0).astype(BF16)
    avg3 = jnp.concatenate([avg, avg, avg], axis=0)

    def head_mean(x):
        hi = x.astype(BF16)
        rest = x - hi.astype(F32)
        mid = rest.astype(BF16)
        lo = (rest - mid.astype(F32)).astype(BF16)
        return _dot(jnp.concatenate([hi, mid, lo], axis=-1), avg3)

    mu = head_mean(out)
    cen = out - mu
    var = head_mean(cen * cen)
    y = cen * lax.rsqrt(var + EPS) * gain_ref[...]
    g = g_ref[...]
    o_ref[...] = (y * (g * (1.0 / (1.0 + jnp.exp(-g))))).astype(o_ref.dtype)


def _retention_decay_tables(t):
    log_gamma = jnp.log1p(-jnp.power(2.0, -RET_DECAY_OFFSET - jnp.arange(N_HEADS, dtype=F32)))
    lg_v = jnp.repeat(log_gamma, HEAD_DIM)
    lg_qk = jnp.tile(jnp.repeat(log_gamma, HEAD_DIM // 2), 2)
    r = jnp.arange(t, dtype=F32)
    dist = jnp.abs(r[:, None] - r[None, :])
    reach = (jnp.arange(t)[None, :] // CHUNK) <= (jnp.arange(t)[:, None] // CHUNK)
    d_intra = jnp.where(reach[None], jnp.exp(log_gamma[:, None, None] * dist[None]), 0.0)
    d_q = jnp.exp(lg_qk[None, :] * (r[:, None] + 1.0))
    d_kt = jnp.exp(lg_qk[:, None] * (float(t - 1) - r)[None, :])
    d_state = jnp.exp(lg_v * float(t))[None, :]
    return d_intra.reshape(N_HEADS * t, t), d_q, d_kt, d_state


def _retention(q, k, v, g, tables, gains, layer, group):
    b, s, _ = q.shape
    t = tables[1].shape[0]
    tm = min(RET_BLOCKS_PER_STEP * t, s)
    row = pl.BlockSpec((None, tm, GROUP), lambda i, j: (i, j, 0))
    const = lambda a: pl.BlockSpec(a.shape, lambda i, j: (0, 0))
    return pl.pallas_call(
        functools.partial(_retention_kernel, t=t),
        out_shape=jax.ShapeDtypeStruct((b, s, GROUP), BF16),
        grid=(b, s // tm),
        in_specs=[row, row, row, row] + [const(a) for a in tables] + [_gain_spec(layer, group)],
        out_specs=row,
        scratch_shapes=[pltpu.VMEM((GROUP, GROUP), F32)],
        compiler_params=pltpu.CompilerParams(
            dimension_semantics=("parallel", "arbitrary"),
            vmem_limit_bytes=VMEM_LIMIT_BYTES),
        name="retention",
    )(q, k, v, g, *tables, gains)


def _post_kernel(x_ref, a_ref, b_ref, c_ref, d_ref, wo_ref, gpost_ref, gpre_ref,
                 wup_ref, wdn_ref, gfpost_ref, o_ref, *, f_chunk, chain_rows):
    n_chunks = wup_ref.shape[1] // f_chunk
    n_chain = x_ref.shape[0] // chain_rows

    def prologue(k):
        rows = slice(k * chain_rows, (k + 1) * chain_rows)
        mixed = jnp.concatenate([a_ref[rows], b_ref[rows], c_ref[rows], d_ref[rows]], axis=-1)
        x1 = x_ref[rows] + _rms(_dot(mixed, wo_ref[...]), gpost_ref[...])
        return x1, _rms(x1, gpre_ref[...]).astype(BF16)

    def chunk(h, c):
        u = jnp.maximum(_dot(h, wup_ref[:, c * f_chunk:(c + 1) * f_chunk]), 0.0)
        return _dot((u * u).astype(BF16), wdn_ref[c * f_chunk:(c + 1) * f_chunk, :])

    state = {}
    for step in range(n_chain + 1):
        if step < n_chain:
            state[step] = prologue(step)
        if step >= 1:
            k = step - 1
            x1, h = state.pop(k)
            y = chunk(h, 0)
            for c in range(1, n_chunks):
                y = y + chunk(h, c)
            rows = slice(k * chain_rows, (k + 1) * chain_rows)
            o_ref[rows] = x1 + _rms(y, gfpost_ref[...])


def _post(x, a, bb, c, dd, layer, params):
    b, s, d = x.shape
    tm = min(POST_CHAINS * TM_POST, s)
    row = lambda w: pl.BlockSpec((None, tm, w), lambda i, j: (i, j, 0))
    resident = lambda p: pl.BlockSpec((None,) + p.shape[1:],
                                      lambda i, j: (layer,) + (0,) * (p.ndim - 1),
                                      pipeline_mode=pl.Buffered(1))
    return pl.pallas_call(
        functools.partial(_post_kernel, f_chunk=1024, chain_rows=min(TM_POST, tm)),
        out_shape=jax.ShapeDtypeStruct((b, s, d), F32),
        grid=(b, s // tm),
        in_specs=([row(d), row(GROUP), row(GROUP), row(GROUP), row(GROUP)]
                  + [resident(p) for p in params]),
        out_specs=row(d),
        compiler_params=pltpu.CompilerParams(
            dimension_semantics=("parallel", "parallel"),
            vmem_limit_bytes=VMEM_LIMIT_BYTES),
        name="out_proj_mlp",
    )(x, a, bb, c, dd, *params)


def _swap_halves(w, period):
    half = period // 2
    starts = range(0, w.shape[-1], period)
    return jnp.concatenate(
        [w[..., s0 + o:s0 + o + half] for s0 in starts for o in (half, 0)], axis=-1)


def _halves_first(w, period):
    half = period // 2
    starts = range(0, w.shape[-1], period)
    return jnp.concatenate(
        [w[..., s0 + o:s0 + o + half] for o in (0, half) for s0 in starts], axis=-1)


def _projection_weights(w_in, b_forget, w_q_up, w_kv_up):
    depth, d, _ = w_in.shape
    sizes = [GROUP, GROUP, GROUP, N_HEADS, 256, 128, MLA_ROPE] + [GROUP] * 7
    offs = np.concatenate([[0], np.cumsum(sizes)])
    (fq, fk, fv, ff, cq, ckv, kr, rq, rk, rv, rg, sq, sk, sv) = [
        w_in[..., int(offs[n]):int(offs[n + 1])] for n in range(len(sizes))]
    scale = HEAD_DIM ** -0.5
    misc = jnp.concatenate(
        [kr, _swap_halves(kr, MLA_ROPE), ff,
         jnp.zeros((depth, d, LANES - 2 * MLA_ROPE - N_HEADS), F32)], axis=-1)
    w_cat = jnp.concatenate(
        [fq * scale, fk, fv, cq, ckv, misc, _halves_first(rq, HEAD_DIM),
         _halves_first(rk, HEAD_DIM) * scale, rv, rg, sq * scale, sk, sv], axis=-1).astype(BF16)
    bf_pad = jnp.concatenate(
        [b_forget, jnp.zeros((depth, LANES - N_HEADS), F32)], axis=-1)[:, None, :]
    wq4 = w_q_up.reshape(depth, -1, N_HEADS, MLA_QK)
    q_nope = wq4[..., :HEAD_DIM].reshape(depth, -1, GROUP)
    q_rope = wq4[..., HEAD_DIM:].reshape(depth, -1, N_HEADS * MLA_ROPE)
    w_q = jnp.concatenate(
        [q_nope, q_rope, _swap_halves(q_rope, MLA_ROPE)], axis=-1).astype(BF16)
    wkv4 = w_kv_up.reshape(depth, -1, N_HEADS, 2 * HEAD_DIM)
    w_kv = jnp.concatenate([wkv4[..., :HEAD_DIM].reshape(depth, -1, GROUP),
                            wkv4[..., HEAD_DIM:].reshape(depth, -1, GROUP)],
                           axis=-1).astype(BF16)
    return w_cat, bf_pad, w_q, w_kv


def kernel(x, positions, g_mix_pre, w_in, b_forget, g_q_lora, w_q_up, g_kv_lora, w_kv_up,
           g_mix_out, w_out, g_mix_post, g_ffn_pre, w_ffn_up, w_ffn_down, g_ffn_post):
    depth = w_in.shape[0]
    vec = lambda g: g[:, None, :]
    w_cat, bf_pad, w_q, w_kv = _projection_weights(w_in, b_forget, w_q_up, w_kv_up)
    proj_params = (vec(g_mix_pre), w_cat, bf_pad, vec(g_q_lora), w_q, vec(g_kv_lora), w_kv)
    post_params = (w_out.astype(BF16), vec(g_mix_post), vec(g_ffn_pre),
                   w_ffn_up.astype(BF16), w_ffn_down.astype(BF16), vec(g_ffn_post))
    gains = g_mix_out.reshape(depth, 4, 1, GROUP)
    tab = _rope_tables(positions)
    ret_tables = _retention_decay_tables(min(T_RET, x.shape[1]))
    for layer in range(depth):
        (fq, fkt, fv, mq, mkt, mv, rq, rk, rv, rg, sq, skt, sv) = _projection(
            x, tab, layer, proj_params)
        out_a = _softmax_attention(fq, fkt, fv, gains, layer, 0,
                                   chunk_causal=False, name="fox_attention")
        out_b = _softmax_attention(mq, mkt, mv, gains, layer, 1,
                                   chunk_causal=True, name="mla_attention")
        out_c = _retention(rq, rk, rv, rg, ret_tables, gains, layer, 2)
        out_d = _stick_attention(sq, skt, sv, gains, layer, 3)
        x = _post(x, out_a, out_b, out_c, out_d, layer, post_params)
    return x
```

```python
import functools

import numpy as np
import jax
import jax.numpy as jnp
from jax import lax
from jax.experimental import pallas as pl
from jax.experimental.pallas import tpu as pltpu

F32 = jnp.float32
BF16 = jnp.bfloat16

N_HEADS = 4
HEAD_DIM = 64
GROUP = N_HEADS * HEAD_DIM
CHUNK = 64
CHUNK_SHIFT = 6
MLA_ROPE = 32
MLA_QK = HEAD_DIM + MLA_ROPE
ROPE_BASE = 10000.0
EPS = 1e-6
RET_DECAY_OFFSET = 5.0
NEG_BIG = -1e30
LOG2E = 1.4426950408889634
STICK_EXIT = 128.0

LANES = 128
VMEM_LIMIT_BYTES = 56 * 1024 * 1024

T_KEY = 512
TM_POST = 512
POST_CHAINS = 2
TQ_ATTN = 512
T_STICK = 256
T_RET = 256
RET_BLOCKS_PER_STEP = 4
TM_TAB = 1024

C_FOX = 0
C_CQ = 768
C_CKV = 1024
C_MISC = 1152
C_RET = 1280
C_RVG = 1792
C_SB = 2304
C_END = 3072
MISC_ROT = 32
MISC_FF = 64


def _rms(x, g):
    return x * lax.rsqrt(jnp.mean(x * x, axis=-1, keepdims=True) + EPS) * g


def _log_sigmoid(x):
    return jnp.minimum(x, 0.0) - jnp.log1p(jnp.exp(-jnp.abs(x)))


def _dot(a, b):
    return jnp.dot(a, b, preferred_element_type=F32)


def _head_select(lane_head, per_head):
    out = per_head[N_HEADS - 1]
    for h in range(N_HEADS - 2, -1, -1):
        out = jnp.where(lane_head == h, per_head[h], out)
    return out


def _layer_spec(a, layer):
    zeros = (0,) * (a.ndim - 1)
    return pl.BlockSpec((None,) + a.shape[1:], lambda i, j: (layer,) + zeros)


def _gain_spec(layer, group):
    return pl.BlockSpec((None, None, 1, GROUP), lambda i, j: (layer, group, 0, 0))


def _per_head(fn, x, tq):
    return jnp.concatenate([fn(h, x[h * tq:(h + 1) * tq]) for h in range(N_HEADS)], axis=0)


def _table_kernel(pos_ref, invf_ref, sign_ref, tab_ref):
    half_r, half_m = HEAD_DIM // 2, MLA_ROPE // 2
    ang = pos_ref[...].astype(F32) * invf_ref[...]
    lane = lax.broadcasted_iota(jnp.int32, (1, LANES), 1)
    is_r = lane < half_r
    is_m = (lane >= half_r) & (lane < half_r + half_m)

    def spread_r(x):
        x = jnp.where(is_r, x, 0.0)
        x = x + pltpu.roll(x, half_r, 1)
        return x + pltpu.roll(x, 2 * half_r, 1)

    def spread_m(x):
        x = pltpu.roll(jnp.where(is_m, x, 0.0), LANES - half_r, 1)
        x = x + pltpu.roll(x, half_m, 1)
        x = x + pltpu.roll(x, 2 * half_m, 1)
        return x + pltpu.roll(x, 4 * half_m, 1)

    c = jnp.cos(ang)
    s = jnp.sin(ang)
    tab_ref[...] = jnp.concatenate(
        [spread_r(c), spread_r(s), spread_m(c), spread_m(s) * sign_ref[...]], axis=-1)


def _rope_tables(positions):
    b, s = positions.shape
    half_r, half_m = HEAD_DIM // 2, MLA_ROPE // 2
    invf_r = ROPE_BASE ** (-jnp.arange(half_r, dtype=F32) / half_r)
    invf_m = ROPE_BASE ** (-jnp.arange(half_m, dtype=F32) / half_m)
    invf = jnp.concatenate([invf_r, invf_m, jnp.zeros((LANES - half_r - half_m,), F32)])[None, :]
    lane = np.arange(LANES)
    sign = jnp.asarray(np.where(lane % MLA_ROPE < half_m, -1.0, 1.0), F32)[None, :]
    tm = min(TM_TAB, s)
    return pl.pallas_call(
        _table_kernel,
        out_shape=jax.ShapeDtypeStruct((b, s, 4 * LANES), F32),
        grid=(b, s // tm),
        in_specs=[pl.BlockSpec((None, tm, 1), lambda i, j: (i, j, 0)),
                  pl.BlockSpec((1, LANES), lambda i, j: (0, 0)),
                  pl.BlockSpec((1, LANES), lambda i, j: (0, 0))],
        out_specs=pl.BlockSpec((None, tm, 4 * LANES), lambda i, j: (i, j, 0)),
        compiler_params=pltpu.CompilerParams(
            dimension_semantics=("parallel", "parallel")),
        name="rope_tables",
    )(positions[:, :, None], invf, sign)


def _proj_kernel(x_ref, tab_ref, g_ref, w_ref, bf_ref, gq_ref, wq_ref, gkv_ref, wkv_ref,
                 fq_o, fkt_o, fv_o, mq_o, mkt_o, mv_o,
                 rq_o, rk_o, rv_o, rg_o, sq_o, skt_o, sv_o, carry_ref, *, mla_scale):
    tm = x_ref.shape[0]
    h = _rms(x_ref[...], g_ref[...]).astype(BF16)

    def proj(a, b):
        return _dot(h, w_ref[:, a:b])

    tab = tab_ref[...]
    cos_r, sin_r = tab[:, 0:LANES], tab[:, LANES:2 * LANES]
    cos_m, sin_m = tab[:, 2 * LANES:3 * LANES], tab[:, 3 * LANES:]
    head64 = lax.broadcasted_iota(jnp.int32, (1, GROUP), 1) >> 6
    lane = lax.broadcasted_iota(jnp.int32, (1, LANES), 1)
    misc = proj(C_MISC, C_RET)

    def store_heads(o_ref, q):
        qb = q.astype(BF16)
        zero = jnp.zeros_like(qb)
        for hh in range(N_HEADS):
            o_ref[hh] = jnp.where(head64 == hh, qb, zero)

    @pl.when(pl.program_id(1) == 0)
    def _():
        carry_ref[...] = jnp.zeros_like(carry_ref)

    ls = _log_sigmoid(pltpu.roll(misc, LANES - MISC_FF, 1) + bf_ref[...])
    row = lax.broadcasted_iota(jnp.int32, (tm, tm), 0)
    col = lax.broadcasted_iota(jnp.int32, (tm, tm), 1)
    tri = jnp.where(col <= row, 1.0, 0.0).astype(BF16)
    ls_hi = ls.astype(BF16)
    ls_rest = ls - ls_hi.astype(F32)
    ls_mid = ls_rest.astype(BF16)
    ls_lo = (ls_rest - ls_mid.astype(F32)).astype(BF16)
    parts = _dot(tri, jnp.concatenate([ls_hi, ls_mid, ls_lo], axis=-1))
    cum = (parts[:, :LANES] + parts[:, LANES:2 * LANES] + parts[:, 2 * LANES:]) + carry_ref[...]
    carry_ref[...] = cum[tm - 1:tm, :]
    c_hi = (cum * LOG2E).astype(BF16).astype(F32)
    c_rest = cum * LOG2E - c_hi
    c_mid = c_rest.astype(BF16).astype(F32)
    c_lo = (c_rest - c_mid).astype(BF16).astype(F32)
    term_of_lane = lane & 3
    spread = jnp.zeros((tm, LANES), F32)
    for hh in range(N_HEADS):
        for t, term in enumerate((c_hi, c_mid, c_lo)):
            val = jnp.broadcast_to(term[:, hh:hh + 1], (tm, LANES))
            spread = jnp.where(((lane >> 3) == hh) & (term_of_lane == t), val, spread)
    used = (lane < 8 * N_HEADS) & (term_of_lane < 3)
    first = used & ((lane & 4) == 0)
    second = used & ((lane & 4) != 0)
    gate_q = jnp.where(first, spread, jnp.where(second, 1.0, 0.0))
    gate_k = jnp.where(second, -spread, jnp.where(first, 1.0, 0.0))

    hq = _rms(proj(C_CQ, C_CKV), gq_ref[...]).astype(BF16)
    qm = _dot(hq, wq_ref[...]) * mla_scale
    q_nope = qm[:, :GROUP].astype(BF16)
    q_rope = (qm[:, GROUP:GROUP + LANES] * cos_m + qm[:, GROUP + LANES:] * sin_m).astype(BF16)
    zero = jnp.zeros_like(q_rope)
    for hh in range(N_HEADS):
        pair = hh // 2
        nope = jnp.where((lane >> 6) == hh % 2, q_nope[:, pair * LANES:(pair + 1) * LANES], zero)
        rope = jnp.where((lane >> 5) == hh, q_rope, zero)
        mq_o[hh] = jnp.concatenate([nope, rope], axis=-1)
    hkv = _rms(proj(C_CKV, C_MISC), gkv_ref[...]).astype(BF16)
    kv = _dot(hkv, wkv_ref[...])
    in_rope = lane < MLA_ROPE
    k_rope = (misc * jnp.where(in_rope, cos_m, 0.0)
              + pltpu.roll(misc, LANES - MISC_ROT, 1) * jnp.where(in_rope, sin_m, 0.0))
    k_rope = (k_rope + pltpu.roll(k_rope, MLA_ROPE, 1)
              + pltpu.roll(k_rope, 2 * MLA_ROPE, 1) + pltpu.roll(k_rope, 3 * MLA_ROPE, 1))
    for pair in range(2):
        kp = jnp.concatenate([kv[:, pair * LANES:(pair + 1) * LANES], k_rope], axis=-1)
        mkt_o[pair * GROUP:(pair + 1) * GROUP, :] = kp.T.astype(BF16)
    mv_o[...] = kv[:, GROUP:].astype(BF16)

    pr = proj(C_RET, C_RVG)

    def rotary(first, second):
        return jnp.concatenate([first * cos_r - second * sin_r, second * cos_r + first * sin_r],
                               axis=-1)

    rq_o[...] = rotary(pr[:, 0:LANES], pr[:, LANES:GROUP])
    rk_o[...] = rotary(pr[:, GROUP:GROUP + LANES], pr[:, GROUP + LANES:2 * GROUP])
    pv = proj(C_RVG, C_SB)
    rv_o[...] = pv[:, :GROUP].astype(BF16)
    rg_o[...] = pv[:, GROUP:]

    ps = proj(C_SB, C_END)
    store_heads(sq_o, ps[:, 0:GROUP])
    skt = ps[:, GROUP:2 * GROUP].T.astype(BF16)
    for c in range(tm // T_STICK):
        skt_o[c] = skt[:, c * T_STICK:(c + 1) * T_STICK]
    sv_o[...] = ps[:, 2 * GROUP:].astype(BF16)

    p = proj(C_FOX, C_CQ)
    fq = p[:, 0:GROUP] * LOG2E
    for hh in range(N_HEADS):
        pair = hh // 2
        nope = jnp.where((lane >> 6) == hh % 2, fq[:, pair * LANES:(pair + 1) * LANES], 0.0)
        gate = jnp.where((lane >> 3) == hh, gate_q, 0.0)
        fq_o[hh] = jnp.concatenate([nope, gate], axis=-1).astype(BF16)
    for pair in range(2):
        kp = jnp.concatenate([p[:, GROUP + pair * LANES:GROUP + (pair + 1) * LANES], gate_k],
                             axis=-1)
        fkt_o[pair * GROUP:(pair + 1) * GROUP, :] = kp.T.astype(BF16)
    fv_o[...] = p[:, 2 * GROUP:].astype(BF16)


def _projection(x, tab, layer, params):
    b, s, d = x.shape
    tm = min(T_KEY, s)
    nk = s // tm
    row = lambda w: pl.BlockSpec((None, tm, w), lambda i, j: (i, j, 0))
    heads = pl.BlockSpec((None, N_HEADS, tm, GROUP), lambda i, j: (i, 0, j, 0))
    keyt = lambda r: pl.BlockSpec((None, None, r, tm), lambda i, j: (i, j, 0, 0))
    bf = lambda w: jax.ShapeDtypeStruct((b, s, w), BF16)
    f32 = lambda w: jax.ShapeDtypeStruct((b, s, w), F32)
    q4 = jax.ShapeDtypeStruct((b, N_HEADS, s, GROUP), BF16)
    kt = lambda r: jax.ShapeDtypeStruct((b, nk, r, tm), BF16)
    n_st = tm // T_STICK
    skt_shape = jax.ShapeDtypeStruct((b, nk * n_st, GROUP, T_STICK), BF16)
    skt_spec = pl.BlockSpec((None, n_st, GROUP, T_STICK), lambda i, j: (i, j, 0, 0))
    out_shape = [q4, kt(2 * GROUP), bf(GROUP),
                 q4, kt(2 * GROUP), bf(GROUP),
                 f32(GROUP), f32(GROUP), bf(GROUP), f32(GROUP),
                 q4, skt_shape, bf(GROUP)]
    out_specs = [heads, keyt(2 * GROUP), row(GROUP),
                 heads, keyt(2 * GROUP), row(GROUP),
                 row(GROUP), row(GROUP), row(GROUP), row(GROUP),
                 heads, skt_spec, row(GROUP)]
    return pl.pallas_call(
        functools.partial(_proj_kernel, mla_scale=MLA_QK ** -0.5 * LOG2E),
        out_shape=out_shape,
        grid=(b, nk),
        in_specs=[row(d), row(4 * LANES)] + [_layer_spec(a, layer) for a in params],
        out_specs=out_specs,
        scratch_shapes=[pltpu.VMEM((1, LANES), F32)],
        compiler_params=pltpu.CompilerParams(
            dimension_semantics=("parallel", "arbitrary"),
            vmem_limit_bytes=VMEM_LIMIT_BYTES),
        name="in_projection",
    )(x, tab, *params)


def _softmax_attn_kernel(q_ref, kt_ref, v_ref, gain_ref, o_ref, acc_ref, m_ref, l_ref,
                         *, chunk_causal, tq, tk):
    i = pl.program_id(1)
    groups = kt_ref.shape[1] // GROUP
    hpg = N_HEADS // groups

    def block(j, c0, nk, r0, masked, first):
        nr = tq - r0
        kt = kt_ref[j, :, c0:c0 + nk]
        if masked:
            qpos = i * tq + r0 + lax.broadcasted_iota(jnp.int32, (nr, nk), 0)
            kpos = j * tk + c0 + lax.broadcasted_iota(jnp.int32, (nr, nk), 1)
            if chunk_causal:
                visible = (kpos >> CHUNK_SHIFT) <= (qpos >> CHUNK_SHIFT)
            else:
                visible = kpos <= qpos
        s_parts = [_dot(q_ref[g * hpg:(g + 1) * hpg, r0:tq, :].reshape(hpg * nr, GROUP),
                        kt[g * GROUP:(g + 1) * GROUP]) for g in range(groups)]
        p_rows, alphas = [], []
        for h in range(N_HEADS):
            hr = slice(h * tq + r0, (h + 1) * tq)
            s = s_parts[h // hpg][(h % hpg) * nr:(h % hpg + 1) * nr]
            if masked:
                s = jnp.where(visible, s, NEG_BIG)
            m_new = jnp.broadcast_to(jnp.max(s, axis=-1, keepdims=True), (nr, LANES))
            l_new = None
            if not first:
                m_prev = m_ref[hr]
                m_new = jnp.maximum(m_prev, m_new)
                alpha = jnp.exp2(m_prev - m_new)
                alphas.append(jnp.concatenate([alpha, alpha], axis=-1))
                l_new = alpha * l_ref[hr]
            ps = []
            for c in range(nk // LANES):
                pc = jnp.exp2(s[:, c * LANES:(c + 1) * LANES] - m_new)
                l_new = pc if l_new is None else l_new + pc
                ps.append(pc.astype(BF16))
            l_ref[hr] = l_new
            m_ref[hr] = m_new
            p_rows.append(jnp.concatenate(ps, axis=-1))
        start = pl.multiple_of(j * tk + c0, nk)
        pv = _dot(jnp.concatenate(p_rows, axis=0), v_ref[pl.ds(start, nk), :])
        if first:
            acc_ref[...] = pv
        elif r0 == 0:
            acc_ref[...] = acc_ref[...] * jnp.concatenate(alphas, axis=0) + pv
        else:
            for h in range(N_HEADS):
                hr = slice(h * tq + r0, (h + 1) * tq)
                acc_ref[hr] = acc_ref[hr] * alphas[h] + pv[h * nr:(h + 1) * nr]

    n_full = (i * tq) // tk
    half = tk // 2
    block(n_full, 0, half, 0, True, True)
    block(n_full, half, half, tq // 2, True, False)

    def body(j, carry):
        block(j, 0, tk, 0, False, False)
        return carry

    lax.fori_loop(0, n_full, body, 0)

    inv_l = 1.0 / jnp.sum(l_ref[...], axis=-1, keepdims=True)
    o_all = acc_ref[...] * inv_l
    head_v = lax.broadcasted_iota(jnp.int32, (1, GROUP), 1) >> 6
    o = _head_select(head_v, [o_all[h * tq:(h + 1) * tq] for h in range(N_HEADS)])
    o_ref[...] = _rms(o, gain_ref[...]).astype(o_ref.dtype)


def _softmax_attention(q, kt, v, gains, layer, group, *, chunk_causal, name):
    b, _, s, _ = q.shape
    nk, krows, tk = kt.shape[1:]
    tq = min(TQ_ATTN, s)
    assert tq == tk and tk % (2 * CHUNK) == 0, (tq, tk)
    rows = N_HEADS * tq
    return pl.pallas_call(
        functools.partial(_softmax_attn_kernel, chunk_causal=chunk_causal, tq=tq, tk=tk),
        out_shape=jax.ShapeDtypeStruct((b, s, GROUP), BF16),
        grid=(b, s // tq),
        in_specs=[pl.BlockSpec((None, N_HEADS, tq, GROUP), lambda i, j: (i, 0, j, 0)),
                  pl.BlockSpec((None, nk, krows, tk), lambda i, j: (i, 0, 0, 0)),
                  pl.BlockSpec((None, s, GROUP), lambda i, j: (i, 0, 0)),
                  _gain_spec(layer, group)],
        out_specs=pl.BlockSpec((None, tq, GROUP), lambda i, j: (i, j, 0)),
        scratch_shapes=[pltpu.VMEM((rows, GROUP), F32),
                        pltpu.VMEM((rows, LANES), F32),
                        pltpu.VMEM((rows, LANES), F32)],
        compiler_params=pltpu.CompilerParams(
            dimension_semantics=("parallel", "parallel"),
            vmem_limit_bytes=VMEM_LIMIT_BYTES),
        name=name,
    )(q, kt, v, gains)


def _stick_kernel(q_ref, kt_ref, v_ref, gain_ref, o_ref, acc_ref, r_ref, *, t):
    i = pl.program_id(1)
    rows = N_HEADS * t
    q = q_ref[...].reshape(rows, GROUP)
    rj = lax.broadcasted_iota(jnp.int32, (t, t), 0)
    cs = lax.broadcasted_iota(jnp.int32, (t, t), 1)
    upper = jnp.where(rj > cs, 1.0, 0.0).astype(BF16)
    upper2 = jnp.concatenate([upper, upper], axis=0)
    visible = cs < rj

    def scores(j, diagonal):
        z = _dot(q, kt_ref[j])
        leave = jnp.maximum(z, 0.0) + jnp.log(1.0 + jnp.exp(-jnp.abs(z)))
        if diagonal:
            leave = _per_head(lambda h, x: jnp.where(visible, x, 0.0), leave, t)
        hi = leave.astype(BF16)
        lo = (leave - hi.astype(F32)).astype(BF16)
        later = _dot(jnp.concatenate([hi, lo], axis=-1), upper2)
        row_sum = jnp.broadcast_to(jnp.sum(leave, axis=-1, keepdims=True), (rows, LANES))
        return z, leave, later, row_sum

    def block(j, diagonal, pre=None):
        z, leave, later, row_sum = scores(j, diagonal) if pre is None else pre
        if diagonal:
            r_new = row_sum
        else:
            r_prev = r_ref[...]
            later = later + jnp.concatenate([r_prev] * (t // LANES), axis=-1)
            r_new = r_prev + row_sum
        w = jnp.exp(z - leave - later)
        if diagonal:
            w = _per_head(lambda h, x: jnp.where(visible, x, 0.0), w, t)
        r_ref[...] = r_new
        start = pl.multiple_of(j * t, t)
        pv = _dot(w.astype(BF16), v_ref[pl.ds(start, t), :])
        if diagonal:
            acc_ref[...] = pv
        else:
            acc_ref[...] += pv
        return jnp.min(r_new)

    def cond(carry):
        j, r_min = carry
        return jnp.logical_and(j >= 0, r_min < STICK_EXIT)

    def body(carry):
        j, _ = carry
        return j - 1, block(j, False)

    def finish():
        acc = acc_ref[...]
        head_v = lax.broadcasted_iota(jnp.int32, (1, GROUP), 1) >> 6
        o = _head_select(head_v, [acc[h * t:(h + 1) * t] for h in range(N_HEADS)])
        o_ref[...] = _rms(o, gain_ref[...]).astype(o_ref.dtype)

    @pl.when(i == 0)
    def _():
        block(0, True)
        finish()

    @pl.when(i > 0)
    def _():
        pre_diag = scores(i, True)
        pre_left = scores(i - 1, False)
        block(i, True, pre_diag)
        r_min = block(i - 1, False, pre_left)
        lax.while_loop(cond, body, (i - 2, r_min))
        finish()


def _stick_attention(q, kt, v, gains, layer, group):
    b, _, s, _ = q.shape
    nk, krows, tk = kt.shape[1:]
    tq = tk
    rows = N_HEADS * tq
    return pl.pallas_call(
        functools.partial(_stick_kernel, t=tk),
        out_shape=jax.ShapeDtypeStruct((b, s, GROUP), BF16),
        grid=(b, s // tq),
        in_specs=[pl.BlockSpec((None, N_HEADS, tq, GROUP), lambda i, j: (i, 0, j, 0)),
                  pl.BlockSpec((None, nk, krows, tk), lambda i, j: (i, 0, 0, 0)),
                  pl.BlockSpec((None, s, GROUP), lambda i, j: (i, 0, 0)),
                  _gain_spec(layer, group)],
        out_specs=pl.BlockSpec((None, tq, GROUP), lambda i, j: (i, j, 0)),
        scratch_shapes=[pltpu.VMEM((rows, GROUP), F32),
                        pltpu.VMEM((rows, LANES), F32)],
        compiler_params=pltpu.CompilerParams(
            dimension_semantics=("parallel", "parallel"),
            vmem_limit_bytes=VMEM_LIMIT_BYTES),
        name="stick_attention",
    )(q, kt, v, gains)


def _retention_kernel(q_ref, k_ref, v_ref, g_ref, d_intra_ref, d_q_ref, d_kt_ref, d_state_ref,
                      gain_ref, o_ref, state_ref, *, t):
    @pl.when(pl.program_id(1) == 0)
    def _():
        state_ref[...] = jnp.zeros_like(state_ref)

    head = lax.broadcasted_iota(jnp.int32, (1, GROUP), 1) >> 6
    head_qk = (lax.broadcasted_iota(jnp.int32, (1, GROUP), 1) & (LANES - 1)) >> 5
    head_qk_r = (lax.broadcasted_iota(jnp.int32, (GROUP, 1), 0) & (LANES - 1)) >> 5
    same_head = head_qk_r == head

    state = state_ref[...]
    outs = []
    for sb in range(q_ref.shape[0] // t):
        rows = slice(sb * t, (sb + 1) * t)
        q = q_ref[rows]
        v = v_ref[rows]
        kt = k_ref[rows].T
        q_b = q.astype(BF16)
        zq = jnp.zeros_like(q_b)
        q4 = jnp.concatenate([jnp.where(head_qk == h, q_b, zq) for h in range(N_HEADS)], axis=0)
        w = (_dot(q4, kt.astype(BF16)) * d_intra_ref[...]).astype(BF16)
        wv = _dot(w, v)
        out = _head_select(head, [wv[h * t:(h + 1) * t] for h in range(N_HEADS)])
        outs.append(out + _dot((q * d_q_ref[...]).astype(BF16), state.astype(BF16)))
        kv = _dot((kt * d_kt_ref[...]).astype(BF16), v)
        state = state * d_state_ref[...] + jnp.where(same_head, kv, 0.0)
    state_ref[...] = state
    out = jnp.concatenate(outs, axis=0)

    head_r = lax.broadcasted_iota(jnp.int32, (GROUP, 1), 0) >> 6
    avg = jnp.where(head_r == head, 1.0 / HEAD_DIM, 0.0).astype(BF16)
    avg3 = jnp.concatenate([avg, avg, avg], axis=0)

    def head_mean(x):
        hi = x.astype(BF16)
        rest = x - hi.astype(F32)
        mid = rest.astype(BF16)
        lo = (rest - mid.astype(F32)).astype(BF16)
        return _dot(jnp.concatenate([hi, mid, lo], axis=-1), avg3)

    mu = head_mean(out)
    cen = out - mu
    var = head_mean(cen * cen)
    y = cen * lax.rsqrt(var + EPS) * gain_ref[...]
    g = g_ref[...]
    o_ref[...] = (y * (g * (1.0 / (1.0 + jnp.exp(-g))))).astype(o_ref.dtype)


def _retention_decay_tables(t):
    log_gamma = jnp.log1p(-jnp.power(2.0, -RET_DECAY_OFFSET - jnp.arange(N_HEADS, dtype=F32)))
    lg_v = jnp.repeat(log_gamma, HEAD_DIM)
    lg_qk = jnp.tile(jnp.repeat(log_gamma, HEAD_DIM // 2), 2)
    r = jnp.arange(t, dtype=F32)
    dist = jnp.abs(r[:, None] - r[None, :])
    reach = (jnp.arange(t)[None, :] // CHUNK) <= (jnp.arange(t)[:, None] // CHUNK)
    d_intra = jnp.where(reach[None], jnp.exp(log_gamma[:, None, None] * dist[None]), 0.0)
    d_q = jnp.exp(lg_qk[None, :] * (r[:, None] + 1.0))
    d_kt = jnp.exp(lg_qk[:, None] * (float(t - 1) - r)[None, :])
    d_state = jnp.exp(lg_v * float(t))[None, :]
    return d_intra.reshape(N_HEADS * t, t), d_q, d_kt, d_state


def _retention(q, k, v, g, tables, gains, layer, group):
    b, s, _ = q.shape
    t = tables[1].shape[0]
    tm = min(RET_BLOCKS_PER_STEP * t, s)
    row = pl.BlockSpec((None, tm, GROUP), lambda i, j: (i, j, 0))
    const = lambda a: pl.BlockSpec(a.shape, lambda i, j: (0, 0))
    return pl.pallas_call(
        functools.partial(_retention_kernel, t=t),
        out_shape=jax.ShapeDtypeStruct((b, s, GROUP), BF16),
        grid=(b, s // tm),
        in_specs=[row, row, row, row] + [const(a) for a in tables] + [_gain_spec(layer, group)],
        out_specs=row,
        scratch_shapes=[pltpu.VMEM((GROUP, GROUP), F32)],
        compiler_params=pltpu.CompilerParams(
            dimension_semantics=("parallel", "arbitrary"),
            vmem_limit_bytes=VMEM_LIMIT_BYTES),
        name="retention",
    )(q, k, v, g, *tables, gains)


def _post_kernel(x_ref, a_ref, b_ref, c_ref, d_ref, wo_ref, gpost_ref, gpre_ref,
                 wup_ref, wdn_ref, gfpost_ref, o_ref, *, f_chunk, chain_rows):
    n_chunks = wup_ref.shape[1] // f_chunk
    n_chain = x_ref.shape[0] // chain_rows

    def prologue(k):
        rows = slice(k * chain_rows, (k + 1) * chain_rows)
        mixed = jnp.concatenate([a_ref[rows], b_ref[rows], c_ref[rows], d_ref[rows]], axis=-1)
        x1 = x_ref[rows] + _rms(_dot(mixed, wo_ref[...]), gpost_ref[...])
        return x1, _rms(x1, gpre_ref[...]).astype(BF16)

    def chunk(h, c):
        u = jnp.maximum(_dot(h, wup_ref[:, c * f_chunk:(c + 1) * f_chunk]), 0.0)
        return _dot((u * u).astype(BF16), wdn_ref[c * f_chunk:(c + 1) * f_chunk, :])

    state = {}
    for step in range(n_chain + 1):
        if step < n_chain:
            state[step] = prologue(step)
        if step >= 1:
            k = step - 1
            x1, h = state.pop(k)
            y = chunk(h, 0)
            for c in range(1, n_chunks):
                y = y + chunk(h, c)
            rows = slice(k * chain_rows, (k + 1) * chain_rows)
            o_ref[rows] = x1 + _rms(y, gfpost_ref[...])


def _post(x, a, bb, c, dd, layer, params):
    b, s, d = x.shape
    tm = min(POST_CHAINS * TM_POST, s)
    row = lambda w: pl.BlockSpec((None, tm, w), lambda i, j: (i, j, 0))
    resident = lambda p: pl.BlockSpec((None,) + p.shape[1:],
                                      lambda i, j: (layer,) + (0,) * (p.ndim - 1),
                                      pipeline_mode=pl.Buffered(1))
    return pl.pallas_call(
        functools.partial(_post_kernel, f_chunk=1024, chain_rows=min(TM_POST, tm)),
        out_shape=jax.ShapeDtypeStruct((b, s, d), F32),
        grid=(b, s // tm),
        in_specs=([row(d), row(GROUP), row(GROUP), row(GROUP), row(GROUP)]
                  + [resident(p) for p in params]),
        out_specs=row(d),
        compiler_params=pltpu.CompilerParams(
            dimension_semantics=("parallel", "parallel"),
            vmem_limit_bytes=VMEM_LIMIT_BYTES),
        name="out_proj_mlp",
    )(x, a, bb, c, dd, *params)


def _swap_halves(w, period):
    half = period // 2
    starts = range(0, w.shape[-1], period)
    return jnp.concatenate(
        [w[..., s0 + o:s0 + o + half] for s0 in starts for o in (half, 0)], axis=-1)


def _halves_first(w, period):
    half = period // 2
    starts = range(0, w.shape[-1], period)
    return jnp.concatenate(
        [w[..., s0 + o:s0 + o + half] for o in (0, half) for s0 in starts], axis=-1)


def _projection_weights(w_in, b_forget, w_q_up, w_kv_up):
    depth, d, _ = w_in.shape
    sizes = [GROUP, GROUP, GROUP, N_HEADS, 256, 128, MLA_ROPE] + [GROUP] * 7
    offs = np.concatenate([[0], np.cumsum(sizes)])
    (fq, fk, fv, ff, cq, ckv, kr, rq, rk, rv, rg, sq, sk, sv) = [
        w_in[..., int(offs[n]):int(offs[n + 1])] for n in range(len(sizes))]
    scale = HEAD_DIM ** -0.5
    misc = jnp.concatenate(
        [kr, _swap_halves(kr, MLA_ROPE), ff,
         jnp.zeros((depth, d, LANES - 2 * MLA_ROPE - N_HEADS), F32)], axis=-1)
    w_cat = jnp.concatenate(
        [fq * scale, fk, fv, cq, ckv, misc, _halves_first(rq, HEAD_DIM),
         _halves_first(rk, HEAD_DIM) * scale, rv, rg, sq * scale, sk, sv], axis=-1).astype(BF16)
    bf_pad = jnp.concatenate(
        [b_forget, jnp.zeros((depth, LANES - N_HEADS), F32)], axis=-1)[:, None, :]
    wq4 = w_q_up.reshape(depth, -1, N_HEADS, MLA_QK)
    q_nope = wq4[..., :HEAD_DIM].reshape(depth, -1, GROUP)
    q_rope = wq4[..., HEAD_DIM:].reshape(depth, -1, N_HEADS * MLA_ROPE)
    w_q = jnp.concatenate(
        [q_nope, q_rope, _swap_halves(q_rope, MLA_ROPE)], axis=-1).astype(BF16)
    wkv4 = w_kv_up.reshape(depth, -1, N_HEADS, 2 * HEAD_DIM)
    w_kv = jnp.concatenate([wkv4[..., :HEAD_DIM].reshape(depth, -1, GROUP),
                            wkv4[..., HEAD_DIM:].reshape(depth, -1, GROUP)],
                           axis=-1).astype(BF16)
    return w_cat, bf_pad, w_q, w_kv


def kernel(x, positions, g_mix_pre, w_in, b_forget, g_q_lora, w_q_up, g_kv_lora, w_kv_up,
           g_mix_out, w_out, g_mix_post, g_ffn_pre, w_ffn_up, w_ffn_down, g_ffn_post):
    depth = w_in.shape[0]
    vec = lambda g: g[:, None, :]
    w_cat, bf_pad, w_q, w_kv = _projection_weights(w_in, b_forget, w_q_up, w_kv_up)
    proj_params = (vec(g_mix_pre), w_cat, bf_pad, vec(g_q_lora), w_q, vec(g_kv_lora), w_kv)
    post_params = (w_out.astype(BF16), vec(g_mix_post), vec(g_ffn_pre),
                   w_ffn_up.astype(BF16), w_ffn_down.astype(BF16), vec(g_ffn_post))
    gains = g_mix_out.reshape(depth, 4, 1, GROUP)
    tab = _rope_tables(positions)
    ret_tables = _retention_decay_tables(min(T_RET, x.shape[1]))
    for layer in range(depth):
        (fq, fkt, fv, mq, mkt, mv, rq, rk, rv, rg, sq, skt, sv) = _projection(
            x, tab, layer, proj_params)
        out_a = _softmax_attention(fq, fkt, fv, gains, layer, 0,
                                   chunk_causal=False, name="fox_attention")
        out_b = _softmax_attention(mq, mkt, mv, gains, layer, 1,
                                   chunk_causal=True, name="mla_attention")
        out_c = _retention(rq, rk, rv, rg, ret_tables, gains, layer, 2)
        out_d = _stick_attention(sq, skt, sv, gains, layer, 3)
        x = _post(x, out_a, out_b, out_c, out_d, layer, post_params)
    return x
```

```python
import functools

import numpy as np
import jax
import jax.numpy as jnp
from jax import lax
from jax.experimental import pallas as pl
from jax.experimental.pallas import tpu as pltpu

F32 = jnp.float32
BF16 = jnp.bfloat16

N_HEADS = 4
HEAD_DIM = 64
GROUP = N_HEADS * HEAD_DIM
CHUNK = 64
CHUNK_SHIFT = 6
MLA_ROPE = 32
MLA_QK = HEAD_DIM + MLA_ROPE
ROPE_BASE = 10000.0
EPS = 1e-6
RET_DECAY_OFFSET = 5.0
NEG_BIG = -1e30
LOG2E = 1.4426950408889634
STICK_EXIT = 128.0

LANES = 128
VMEM_LIMIT_BYTES = 56 * 1024 * 1024

T_KEY = 512
TM_POST = 512
POST_CHAINS = 2
TQ_ATTN = 512
T_STICK = 256
T_RET = 256
RET_BLOCKS_PER_STEP = 4
TM_TAB = 1024

C_FOX = 0
C_CQ = 768
C_CKV = 1024
C_MISC = 1152
C_RET = 1280
C_RVG = 1792
C_SB = 2304
C_END = 3072
MISC_ROT = 32
MISC_FF = 64


def _rms(x, g):
    return x * lax.rsqrt(jnp.mean(x * x, axis=-1, keepdims=True) + EPS) * g


def _log_sigmoid(x):
    return jnp.minimum(x, 0.0) - jnp.log1p(jnp.exp(-jnp.abs(x)))


def _dot(a, b):
    return jnp.dot(a, b, preferred_element_type=F32)


def _head_select(lane_head, per_head):
    out = per_head[N_HEADS - 1]
    for h in range(N_HEADS - 2, -1, -1):
        out = jnp.where(lane_head == h, per_head[h], out)
    return out


def _layer_spec(a, layer):
    zeros = (0,) * (a.ndim - 1)
    return pl.BlockSpec((None,) + a.shape[1:], lambda i, j: (layer,) + zeros)


def _gain_spec(layer, group):
    return pl.BlockSpec((None, None, 1, GROUP), lambda i, j: (layer, group, 0, 0))


def _per_head(fn, x, tq):
    return jnp.concatenate([fn(h, x[h * tq:(h + 1) * tq]) for h in range(N_HEADS)], axis=0)


def _table_kernel(pos_ref, invf_ref, sign_ref, tab_ref):
    half_r, half_m = HEAD_DIM // 2, MLA_ROPE // 2
    tm = pos_ref.shape[0]
    lane = lax.broadcasted_iota(jnp.int32, (1, LANES), 1)
    pos = pos_ref[...].astype(F32)
    ang = jnp.where(lane < LANES // 2, pos[:tm // 2], pos[tm // 2:]) * invf_ref[...]
    is_r = lane < half_r
    is_m = (lane >= half_r) & (lane < half_r + half_m)

    def spread_r(x):
        x = jnp.where(is_r, x, 0.0)
        x = x + pltpu.roll(x, half_r, 1)
        return x + pltpu.roll(x, 2 * half_r, 1)

    def spread_m(x):
        x = pltpu.roll(jnp.where(is_m, x, 0.0), LANES - half_r, 1)
        x = x + pltpu.roll(x, half_m, 1)
        x = x + pltpu.roll(x, 2 * half_m, 1)
        return x + pltpu.roll(x, 4 * half_m, 1)

    def tables(c, s):
        return jnp.concatenate(
            [spread_r(c), spread_r(s), spread_m(c), spread_m(s) * sign_ref[...]], axis=-1)

    c = jnp.cos(ang)
    s = jnp.sin(ang)
    tab_ref[:tm // 2, :] = tables(c, s)
    tab_ref[tm // 2:, :] = tables(pltpu.roll(c, LANES // 2, 1), pltpu.roll(s, LANES // 2, 1))


def _rope_tables(positions):
    b, s = positions.shape
    half_r, half_m = HEAD_DIM // 2, MLA_ROPE // 2
    invf_r = ROPE_BASE ** (-jnp.arange(half_r, dtype=F32) / half_r)
    invf_m = ROPE_BASE ** (-jnp.arange(half_m, dtype=F32) / half_m)
    invf = jnp.concatenate([invf_r, invf_m, jnp.zeros((LANES // 2 - half_r - half_m,), F32)])
    invf = jnp.tile(invf, 2)[None, :]
    lane = np.arange(LANES)
    sign = jnp.asarray(np.where(lane % MLA_ROPE < half_m, -1.0, 1.0), F32)[None, :]
    tm = min(TM_TAB, s)
    return pl.pallas_call(
        _table_kernel,
        out_shape=jax.ShapeDtypeStruct((b, s, 4 * LANES), F32),
        grid=(b, s // tm),
        in_specs=[pl.BlockSpec((None, tm, 1), lambda i, j: (i, j, 0)),
                  pl.BlockSpec((1, LANES), lambda i, j: (0, 0)),
                  pl.BlockSpec((1, LANES), lambda i, j: (0, 0))],
        out_specs=pl.BlockSpec((None, tm, 4 * LANES), lambda i, j: (i, j, 0)),
        compiler_params=pltpu.CompilerParams(
            dimension_semantics=("parallel", "parallel")),
        name="rope_tables",
    )(positions[:, :, None], invf, sign)


def _proj_kernel(x_ref, tab_ref, g_ref, w_ref, bf_ref, gq_ref, wq_ref, gkv_ref, wkv_ref,
                 fq_o, fkt_o, fv_o, mq_o, mkt_o, mv_o,
                 rq_o, rk_o, rv_o, rg_o, sq_o, skt_o, sv_o, carry_ref, *, mla_scale):
    tm = x_ref.shape[0]
    h = _rms(x_ref[...], g_ref[...]).astype(BF16)

    def proj(a, b):
        return _dot(h, w_ref[:, a:b])

    tab = tab_ref[...]
    cos_r, sin_r = tab[:, 0:LANES], tab[:, LANES:2 * LANES]
    cos_m, sin_m = tab[:, 2 * LANES:3 * LANES], tab[:, 3 * LANES:]
    head64 = lax.broadcasted_iota(jnp.int32, (1, GROUP), 1) >> 6
    lane = lax.broadcasted_iota(jnp.int32, (1, LANES), 1)
    misc = proj(C_MISC, C_RET)

    def store_heads(o_ref, q):
        qb = q.astype(BF16)
        zero = jnp.zeros_like(qb)
        for hh in range(N_HEADS):
            o_ref[hh] = jnp.where(head64 == hh, qb, zero)

    @pl.when(pl.program_id(1) == 0)
    def _():
        carry_ref[...] = jnp.zeros_like(carry_ref)

    ls = _log_sigmoid(pltpu.roll(misc, LANES - MISC_FF, 1) + bf_ref[...])
    row = lax.broadcasted_iota(jnp.int32, (tm, tm), 0)
    col = lax.broadcasted_iota(jnp.int32, (tm, tm), 1)
    tri = jnp.where(col <= row, 1.0, 0.0).astype(BF16)
    ls_hi = ls.astype(BF16)
    ls_rest = ls - ls_hi.astype(F32)
    ls_mid = ls_rest.astype(BF16)
    ls_lo = (ls_rest - ls_mid.astype(F32)).astype(BF16)
    parts = _dot(tri, jnp.concatenate([ls_hi, ls_mid, ls_lo], axis=-1))
    cum = (parts[:, :LANES] + parts[:, LANES:2 * LANES] + parts[:, 2 * LANES:]) + carry_ref[...]
    carry_ref[...] = cum[tm - 1:tm, :]
    c_hi = (cum * LOG2E).astype(BF16).astype(F32)
    c_rest = cum * LOG2E - c_hi
    c_mid = c_rest.astype(BF16).astype(F32)
    c_lo = (c_rest - c_mid).astype(BF16).astype(F32)
    term_of_lane = lane & 3
    spread = jnp.zeros((tm, LANES), F32)
    for hh in range(N_HEADS):
        for t, term in enumerate((c_hi, c_mid, c_lo)):
            val = jnp.broadcast_to(term[:, hh:hh + 1], (tm, LANES))
            spread = jnp.where(((lane >> 3) == hh) & (term_of_lane == t), val, spread)
    used = (lane < 8 * N_HEADS) & (term_of_lane < 3)
    first = used & ((lane & 4) == 0)
    second = used & ((lane & 4) != 0)
    gate_q = jnp.where(first, spread, jnp.where(second, 1.0, 0.0))
    gate_k = jnp.where(second, -spread, jnp.where(first, 1.0, 0.0))

    hq = _rms(proj(C_CQ, C_CKV), gq_ref[...]).astype(BF16)
    qm = _dot(hq, wq_ref[...]) * mla_scale
    q_nope = qm[:, :GROUP].astype(BF16)
    q_rope = (qm[:, GROUP:GROUP + LANES] * cos_m + qm[:, GROUP + LANES:] * sin_m).astype(BF16)
    zero = jnp.zeros_like(q_rope)
    for hh in range(N_HEADS):
        pair = hh // 2
        nope = jnp.where((lane >> 6) == hh % 2, q_nope[:, pair * LANES:(pair + 1) * LANES], zero)
        rope = jnp.where((lane >> 5) == hh, q_rope, zero)
        mq_o[hh] = jnp.concatenate([nope, rope], axis=-1)
    hkv = _rms(proj(C_CKV, C_MISC), gkv_ref[...]).astype(BF16)
    kv = _dot(hkv, wkv_ref[...])
    in_rope = lane < MLA_ROPE
    k_rope = (misc * jnp.where(in_rope, cos_m, 0.0)
              + pltpu.roll(misc, LANES - MISC_ROT, 1) * jnp.where(in_rope, sin_m, 0.0))
    k_rope = (k_rope + pltpu.roll(k_rope, MLA_ROPE, 1)
              + pltpu.roll(k_rope, 2 * MLA_ROPE, 1) + pltpu.roll(k_rope, 3 * MLA_ROPE, 1))
    for pair in range(2):
        kp = jnp.concatenate([kv[:, pair * LANES:(pair + 1) * LANES], k_rope], axis=-1)
        mkt_o[pair * GROUP:(pair + 1) * GROUP, :] = kp.T.astype(BF16)
    mv_o[...] = kv[:, GROUP:].astype(BF16)

    pr = proj(C_RET, C_RVG)

    def rotary(first, second):
        return jnp.concatenate([first * cos_r - second * sin_r, second * cos_r + first * sin_r],
                               axis=-1)

    rq_o[...] = rotary(pr[:, 0:LANES], pr[:, LANES:GROUP])
    rk_o[...] = rotary(pr[:, GROUP:GROUP + LANES], pr[:, GROUP + LANES:2 * GROUP])
    pv = proj(C_RVG, C_SB)
    rv_o[...] = pv[:, :GROUP].astype(BF16)
    rg_o[...] = pv[:, GROUP:]

    ps = proj(C_SB, C_END)
    store_heads(sq_o, ps[:, 0:GROUP])
    skt = ps[:, GROUP:2 * GROUP].T.astype(BF16)
    for c in range(tm // T_STICK):
        skt_o[c] = skt[:, c * T_STICK:(c + 1) * T_STICK]
    sv_o[...] = ps[:, 2 * GROUP:].astype(BF16)

    p = proj(C_FOX, C_CQ)
    fq = p[:, 0:GROUP] * LOG2E
    for hh in range(N_HEADS):
        pair = hh // 2
        nope = jnp.where((lane >> 6) == hh % 2, fq[:, pair * LANES:(pair + 1) * LANES], 0.0)
        gate = jnp.where((lane >> 3) == hh, gate_q, 0.0)
        fq_o[hh] = jnp.concatenate([nope, gate], axis=-1).astype(BF16)
    for pair in range(2):
        kp = jnp.concatenate([p[:, GROUP + pair * LANES:GROUP + (pair + 1) * LANES], gate_k],
                             axis=-1)
        fkt_o[pair * GROUP:(pair + 1) * GROUP, :] = kp.T.astype(BF16)
    fv_o[...] = p[:, 2 * GROUP:].astype(BF16)


def _projection(x, tab, layer, params):
    b, s, d = x.shape
    tm = min(T_KEY, s)
    nk = s // tm
    row = lambda w: pl.BlockSpec((None, tm, w), lambda i, j: (i, j, 0))
    heads = pl.BlockSpec((None, N_HEADS, tm, GROUP), lambda i, j: (i, 0, j, 0))
    keyt = lambda r: pl.BlockSpec((None, None, r, tm), lambda i, j: (i, j, 0, 0))
    bf = lambda w: jax.ShapeDtypeStruct((b, s, w), BF16)
    f32 = lambda w: jax.ShapeDtypeStruct((b, s, w), F32)
    q4 = jax.ShapeDtypeStruct((b, N_HEADS, s, GROUP), BF16)
    kt = lambda r: jax.ShapeDtypeStruct((b, nk, r, tm), BF16)
    n_st = tm // T_STICK
    skt_shape = jax.ShapeDtypeStruct((b, nk * n_st, GROUP, T_STICK), BF16)
    skt_spec = pl.BlockSpec((None, n_st, GROUP, T_STICK), lambda i, j: (i, j, 0, 0))
    out_shape = [q4, kt(2 * GROUP), bf(GROUP),
                 q4, kt(2 * GROUP), bf(GROUP),
                 f32(GROUP), f32(GROUP), bf(GROUP), f32(GROUP),
                 q4, skt_shape, bf(GROUP)]
    out_specs = [heads, keyt(2 * GROUP), row(GROUP),
                 heads, keyt(2 * GROUP), row(GROUP),
                 row(GROUP), row(GROUP), row(GROUP), row(GROUP),
                 heads, skt_spec, row(GROUP)]
    return pl.pallas_call(
        functools.partial(_proj_kernel, mla_scale=MLA_QK ** -0.5 * LOG2E),
        out_shape=out_shape,
        grid=(b, nk),
        in_specs=[row(d), row(4 * LANES)] + [_layer_spec(a, layer) for a in params],
        out_specs=out_specs,
        scratch_shapes=[pltpu.VMEM((1, LANES), F32)],
        compiler_params=pltpu.CompilerParams(
            dimension_semantics=("parallel", "arbitrary"),
            vmem_limit_bytes=VMEM_LIMIT_BYTES),
        name="in_projection",
    )(x, tab, *params)


def _softmax_attn_kernel(q_ref, kt_ref, v_ref, gain_ref, o_ref, acc_ref, m_ref, l_ref,
                         *, chunk_causal, tq, tk):
    i = pl.program_id(1)
    groups = kt_ref.shape[1] // GROUP
    hpg = N_HEADS // groups

    def scores(j, c0, nk, r0):
        nr = tq - r0
        kt = kt_ref[j, :, c0:c0 + nk]
        return [_dot(q_ref[g * hpg:(g + 1) * hpg, r0:tq, :].reshape(hpg * nr, GROUP),
                     kt[g * GROUP:(g + 1) * GROUP]) for g in range(groups)]

    def block(j, c0, nk, r0, masked, first, s_parts=None):
        nr = tq - r0
        if masked:
            qpos = i * tq + r0 + lax.broadcasted_iota(jnp.int32, (nr, nk), 0)
            kpos = j * tk + c0 + lax.broadcasted_iota(jnp.int32, (nr, nk), 1)
            if chunk_causal:
                visible = (kpos >> CHUNK_SHIFT) <= (qpos >> CHUNK_SHIFT)
            else:
                visible = kpos <= qpos
        if s_parts is None:
            s_parts = scores(j, c0, nk, r0)
        p_rows, alphas = [], []
        for h in range(N_HEADS):
            hr = slice(h * tq + r0, (h + 1) * tq)
            s = s_parts[h // hpg][(h % hpg) * nr:(h % hpg + 1) * nr]
            if masked:
                s = jnp.where(visible, s, NEG_BIG)
            m_new = jnp.broadcast_to(jnp.max(s, axis=-1, keepdims=True), (nr, LANES))
            l_new = None
            if not first:
                m_prev = m_ref[hr]
                m_new = jnp.maximum(m_prev, m_new)
                alpha = jnp.exp2(m_prev - m_new)
                alphas.append(jnp.concatenate([alpha, alpha], axis=-1))
                l_new = alpha * l_ref[hr]
            ps = []
            for c in range(nk // LANES):
                pc = jnp.exp2(s[:, c * LANES:(c + 1) * LANES] - m_new)
                l_new = pc if l_new is None else l_new + pc
                ps.append(pc.astype(BF16))
            l_ref[hr] = l_new
            m_ref[hr] = m_new
            p_rows.append(jnp.concatenate(ps, axis=-1))
        start = pl.multiple_of(j * tk + c0, nk)
        pv = _dot(jnp.concatenate(p_rows, axis=0), v_ref[pl.ds(start, nk), :])
        if first:
            acc_ref[...] = pv
        elif r0 == 0:
            acc_ref[...] = acc_ref[...] * jnp.concatenate(alphas, axis=0) + pv
        else:
            for h in range(N_HEADS):
                hr = slice(h * tq + r0, (h + 1) * tq)
                acc_ref[hr] = acc_ref[hr] * alphas[h] + pv[h * nr:(h + 1) * nr]

    n_full = (i * tq) // tk
    half = tk // 2
    block(n_full, 0, half, 0, True, True)
    block(n_full, half, half, tq // 2, True, False)

    def body(j, carry):
        block(j, 0, tk, 0, False, False)
        return carry

    lax.fori_loop(0, n_full, body, 0)

    inv_l = 1.0 / jnp.sum(l_ref[...], axis=-1, keepdims=True)
    o_all = acc_ref[...] * inv_l
    head_v = lax.broadcasted_iota(jnp.int32, (1, GROUP), 1) >> 6
    o = _head_select(head_v, [o_all[h * tq:(h + 1) * tq] for h in range(N_HEADS)])
    o_ref[...] = _rms(o, gain_ref[...]).astype(o_ref.dtype)


def _softmax_attention(q, kt, v, gains, layer, group, *, chunk_causal, name):
    b, _, s, _ = q.shape
    nk, krows, tk = kt.shape[1:]
    tq = min(TQ_ATTN, s)
    assert tq == tk and tk % (2 * CHUNK) == 0, (tq, tk)
    rows = N_HEADS * tq
    return pl.pallas_call(
        functools.partial(_softmax_attn_kernel, chunk_causal=chunk_causal, tq=tq, tk=tk),
        out_shape=jax.ShapeDtypeStruct((b, s, GROUP), BF16),
        grid=(b, s // tq),
        in_specs=[pl.BlockSpec((None, N_HEADS, tq, GROUP), lambda i, j: (i, 0, j, 0)),
                  pl.BlockSpec((None, nk, krows, tk), lambda i, j: (i, 0, 0, 0)),
                  pl.BlockSpec((None, s, GROUP), lambda i, j: (i, 0, 0)),
                  _gain_spec(layer, group)],
        out_specs=pl.BlockSpec((None, tq, GROUP), lambda i, j: (i, j, 0)),
        scratch_shapes=[pltpu.VMEM((rows, GROUP), F32),
                        pltpu.VMEM((rows, LANES), F32),
                        pltpu.VMEM((rows, LANES), F32)],
        compiler_params=pltpu.CompilerParams(
            dimension_semantics=("parallel", "parallel"),
            vmem_limit_bytes=VMEM_LIMIT_BYTES),
        name=name,
    )(q, kt, v, gains)


def _stick_kernel(q_ref, kt_ref, v_ref, gain_ref, o_ref, acc_ref, r_ref, *, t):
    i = pl.program_id(1)
    rows = N_HEADS * t
    q = q_ref[...].reshape(rows, GROUP)
    rj = lax.broadcasted_iota(jnp.int32, (t, t), 0)
    cs = lax.broadcasted_iota(jnp.int32, (t, t), 1)
    upper = jnp.where(rj > cs, 1.0, 0.0).astype(BF16)
    upper2 = jnp.concatenate([upper, upper], axis=0)
    visible = cs < rj

    def scores(j, diagonal):
        z = _dot(q, kt_ref[j])
        leave = jnp.maximum(z, 0.0) + jnp.log(1.0 + jnp.exp(-jnp.abs(z)))
        if diagonal:
            leave = _per_head(lambda h, x: jnp.where(visible, x, 0.0), leave, t)
        hi = leave.astype(BF16)
        lo = (leave - hi.astype(F32)).astype(BF16)
        later = _dot(jnp.concatenate([hi, lo], axis=-1), upper2)
        row_sum = jnp.broadcast_to(jnp.sum(leave, axis=-1, keepdims=True), (rows, LANES))
        return z, leave, later, row_sum

    def block(j, diagonal, pre=None):
        z, leave, later, row_sum = scores(j, diagonal) if pre is None else pre
        if diagonal:
            r_new = row_sum
        else:
            r_prev = r_ref[...]
            later = later + jnp.concatenate([r_prev] * (t // LANES), axis=-1)
            r_new = r_prev + row_sum
        w = jnp.exp(z - leave - later)
        if diagonal:
            w = _per_head(lambda h, x: jnp.where(visible, x, 0.0), w, t)
        r_ref[...] = r_new
        start = pl.multiple_of(j * t, t)
        pv = _dot(w.astype(BF16), v_ref[pl.ds(start, t), :])
        if diagonal:
            acc_ref[...] = pv
        else:
            acc_ref[...] += pv
        return jnp.min(r_new)

    def cond(carry):
        j, r_min = carry
        return jnp.logical_and(j >= 0, r_min < STICK_EXIT)

    def body(carry):
        j, _ = carry
        return j - 1, block(j, False)

    def finish():
        acc = acc_ref[...]
        head_v = lax.broadcasted_iota(jnp.int32, (1, GROUP), 1) >> 6
        o = _head_select(head_v, [acc[h * t:(h + 1) * t] for h in range(N_HEADS)])
        o_ref[...] = _rms(o, gain_ref[...]).astype(o_ref.dtype)

    @pl.when(i == 0)
    def _():
        block(0, True)
        finish()

    @pl.when(i > 0)
    def _():
        pre_diag = scores(i, True)
        pre_left = scores(i - 1, False)
        block(i, True, pre_diag)
        r_min = block(i - 1, False, pre_left)
        lax.while_loop(cond, body, (i - 2, r_min))
        finish()


def _stick_attention(q, kt, v, gains, layer, group):
    b, _, s, _ = q.shape
    nk, krows, tk = kt.shape[1:]
    tq = tk
    rows = N_HEADS * tq
    return pl.pallas_call(
        functools.partial(_stick_kernel, t=tk),
        out_shape=jax.ShapeDtypeStruct((b, s, GROUP), BF16),
        grid=(b, s // tq),
        in_specs=[pl.BlockSpec((None, N_HEADS, tq, GROUP), lambda i, j: (i, 0, j, 0)),
                  pl.BlockSpec((None, nk, krows, tk), lambda i, j: (i, 0, 0, 0)),
                  pl.BlockSpec((None, s, GROUP), lambda i, j: (i, 0, 0)),
                  _gain_spec(layer, group)],
        out_specs=pl.BlockSpec((None, tq, GROUP), lambda i, j: (i, j, 0)),
        scratch_shapes=[pltpu.VMEM((rows, GROUP), F32),
                        pltpu.VMEM((rows, LANES), F32)],
        compiler_params=pltpu.CompilerParams(
            dimension_semantics=("parallel", "parallel"),
            vmem_limit_bytes=VMEM_LIMIT_BYTES),
        name="stick_attention",
    )(q, kt, v, gains)


def _retention_kernel(q_ref, k_ref, v_ref, g_ref, d_intra_ref, d_q_ref, d_kt_ref, d_state_ref,
                      gain_ref, o_ref, state_ref, *, t):
    @pl.when(pl.program_id(1) == 0)
    def _():
        state_ref[...] = jnp.zeros_like(state_ref)

    head = lax.broadcasted_iota(jnp.int32, (1, GROUP), 1) >> 6
    head_qk = (lax.broadcasted_iota(jnp.int32, (1, GROUP), 1) & (LANES - 1)) >> 5
    head_qk_r = (lax.broadcasted_iota(jnp.int32, (GROUP, 1), 0) & (LANES - 1)) >> 5
    same_head = head_qk_r == head

    state = state_ref[...]
    outs = []
    for sb in range(q_ref.shape[0] // t):
        rows = slice(sb * t, (sb + 1) * t)
        q = q_ref[rows]
        v = v_ref[rows]
        kt = k_ref[rows].T
        q_b = q.astype(BF16)
        zq = jnp.zeros_like(q_b)
        q4 = jnp.concatenate([jnp.where(head_qk == h, q_b, zq) for h in range(N_HEADS)], axis=0)
        w = (_dot(q4, kt.astype(BF16)) * d_intra_ref[...]).astype(BF16)
        wv = _dot(w, v)
        out = _head_select(head, [wv[h * t:(h + 1) * t] for h in range(N_HEADS)])
        outs.append(out + _dot((q * d_q_ref[...]).astype(BF16), state.astype(BF16)))
        kv = _dot((kt * d_kt_ref[...]).astype(BF16), v)
        state = state * d_state_ref[...] + jnp.where(same_head, kv, 0.0)
    state_ref[...] = state
    out = jnp.concatenate(outs, axis=0)

    head_r = lax.broadcasted_iota(jnp.int32, (GROUP, 1), 0) >> 6
    avg = jnp.where(head_r == head, 1.0 / HEAD_DIM, 0.0).astype(BF16)
    avg3 = jnp.concatenate([avg, avg, avg], axis=0)

    def head_mean(x):
        hi = x.astype(BF16)
        rest = x - hi.astype(F32)
        mid = rest.astype(BF16)
        lo = (rest - mid.astype(F32)).astype(BF16)
        return _dot(jnp.concatenate([hi, mid, lo], axis=-1), avg3)

    mu = head_mean(out)
    cen = out - mu
    var = head_mean(cen * cen)
    y = cen * lax.rsqrt(var + EPS) * gain_ref[...]
    g = g_ref[...]
    o_ref[...] = (y * (g * (1.0 / (1.0 + jnp.exp(-g))))).astype(o_ref.dtype)


def _retention_decay_tables(t):
    log_gamma = jnp.log1p(-jnp.power(2.0, -RET_DECAY_OFFSET - jnp.arange(N_HEADS, dtype=F32)))
    lg_v = jnp.repeat(log_gamma, HEAD_DIM)
    lg_qk = jnp.tile(jnp.repeat(log_gamma, HEAD_DIM // 2), 2)
    r = jnp.arange(t, dtype=F32)
    dist = jnp.abs(r[:, None] - r[None, :])
    reach = (jnp.arange(t)[None, :] // CHUNK) <= (jnp.arange(t)[:, None] // CHUNK)
    d_intra = jnp.where(reach[None], jnp.exp(log_gamma[:, None, None] * dist[None]), 0.0)
    d_q = jnp.exp(lg_qk[None, :] * (r[:, None] + 1.0))
    d_kt = jnp.exp(lg_qk[:, None] * (float(t - 1) - r)[None, :])
    d_state = jnp.exp(lg_v * float(t))[None, :]
    return d_intra.reshape(N_HEADS * t, t), d_q, d_kt, d_state


def _retention(q, k, v, g, tables, gains, layer, group):
    b, s, _ = q.shape
    t = tables[1].shape[0]
    tm = min(RET_BLOCKS_PER_STEP * t, s)
    row = pl.BlockSpec((None, tm, GROUP), lambda i, j: (i, j, 0))
    const = lambda a: pl.BlockSpec(a.shape, lambda i, j: (0, 0))
    return pl.pallas_call(
        functools.partial(_retention_kernel, t=t),
        out_shape=jax.ShapeDtypeStruct((b, s, GROUP), BF16),
        grid=(b, s // tm),
        in_specs=[row, row, row, row] + [const(a) for a in tables] + [_gain_spec(layer, group)],
        out_specs=row,
        scratch_shapes=[pltpu.VMEM((GROUP, GROUP), F32)],
        compiler_params=pltpu.CompilerParams(
            dimension_semantics=("parallel", "arbitrary"),
            vmem_limit_bytes=VMEM_LIMIT_BYTES),
        name="retention",
    )(q, k, v, g, *tables, gains)


def _post_kernel(x_ref, a_ref, b_ref, c_ref, d_ref, wo_ref, gpost_ref, gpre_ref,
                 wup_ref, wdn_ref, gfpost_ref, o_ref, *, f_chunk, chain_rows):
    n_chunks = wup_ref.shape[1] // f_chunk
    n_chain = x_ref.shape[0] // chain_rows

    def prologue(k):
        rows = slice(k * chain_rows, (k + 1) * chain_rows)
        mixed = jnp.concatenate([a_ref[rows], b_ref[rows], c_ref[rows], d_ref[rows]], axis=-1)
        x1 = x_ref[rows] + _rms(_dot(mixed, wo_ref[...]), gpost_ref[...])
        return x1, _rms(x1, gpre_ref[...]).astype(BF16)

    def chunk(h, c):
        u = jnp.maximum(_dot(h, wup_ref[:, c * f_chunk:(c + 1) * f_chunk]), 0.0)
        return _dot((u * u).astype(BF16), wdn_ref[c * f_chunk:(c + 1) * f_chunk, :])

    state = {}
    for step in range(n_chain + 1):
        if step < n_chain:
            state[step] = prologue(step)
        if step >= 1:
            k = step - 1
            x1, h = state.pop(k)
            y = chunk(h, 0)
            for c in range(1, n_chunks):
                y = y + chunk(h, c)
            rows = slice(k * chain_rows, (k + 1) * chain_rows)
            o_ref[rows] = x1 + _rms(y, gfpost_ref[...])


def _post(x, a, bb, c, dd, layer, params):
    b, s, d = x.shape
    tm = min(POST_CHAINS * TM_POST, s)
    row = lambda w: pl.BlockSpec((None, tm, w), lambda i, j: (i, j, 0))
    resident = lambda p: pl.BlockSpec((None,) + p.shape[1:],
                                      lambda i, j: (layer,) + (0,) * (p.ndim - 1),
                                      pipeline_mode=pl.Buffered(1))
    return pl.pallas_call(
        functools.partial(_post_kernel, f_chunk=1024, chain_rows=min(TM_POST, tm)),
        out_shape=jax.ShapeDtypeStruct((b, s, d), F32),
        grid=(b, s // tm),
        in_specs=([row(d), row(GROUP), row(GROUP), row(GROUP), row(GROUP)]
                  + [resident(p) for p in params]),
        out_specs=row(d),
        compiler_params=pltpu.CompilerParams(
            dimension_semantics=("parallel", "parallel"),
            vmem_limit_bytes=VMEM_LIMIT_BYTES),
        name="out_proj_mlp",
    )(x, a, bb, c, dd, *params)


def _swap_halves(w, period):
    half = period // 2
    starts = range(0, w.shape[-1], period)
    return jnp.concatenate(
        [w[..., s0 + o:s0 + o + half] for s0 in starts for o in (half, 0)], axis=-1)


def _halves_first(w, period):
    half = period // 2
    starts = range(0, w.shape[-1], period)
    return jnp.concatenate(
        [w[..., s0 + o:s0 + o + half] for o in (0, half) for s0 in starts], axis=-1)


def _projection_weights(w_in, b_forget, w_q_up, w_kv_up):
    depth, d, _ = w_in.shape
    sizes = [GROUP, GROUP, GROUP, N_HEADS, 256, 128, MLA_ROPE] + [GROUP] * 7
    offs = np.concatenate([[0], np.cumsum(sizes)])
    (fq, fk, fv, ff, cq, ckv, kr, rq, rk, rv, rg, sq, sk, sv) = [
        w_in[..., int(offs[n]):int(offs[n + 1])] for n in range(len(sizes))]
    scale = HEAD_DIM ** -0.5
    misc = jnp.concatenate(
        [kr, _swap_halves(kr, MLA_ROPE), ff,
         jnp.zeros((depth, d, LANES - 2 * MLA_ROPE - N_HEADS), F32)], axis=-1)
    w_cat = jnp.concatenate(
        [fq * scale, fk, fv, cq, ckv, misc, _halves_first(rq, HEAD_DIM),
         _halves_first(rk, HEAD_DIM) * scale, rv, rg, sq * scale, sk, sv], axis=-1).astype(BF16)
    bf_pad = jnp.concatenate(
        [b_forget, jnp.zeros((depth, LANES - N_HEADS), F32)], axis=-1)[:, None, :]
    wq4 = w_q_up.reshape(depth, -1, N_HEADS, MLA_QK)
    q_nope = wq4[..., :HEAD_DIM].reshape(depth, -1, GROUP)
    q_rope = wq4[..., HEAD_DIM:].reshape(depth, -1, N_HEADS * MLA_ROPE)
    w_q = jnp.concatenate(
        [q_nope, q_rope, _swap_halves(q_rope, MLA_ROPE)], axis=-1).astype(BF16)
    wkv4 = w_kv_up.reshape(depth, -1, N_HEADS, 2 * HEAD_DIM)
    w_kv = jnp.concatenate([wkv4[..., :HEAD_DIM].reshape(depth, -1, GROUP),
                            wkv4[..., HEAD_DIM:].reshape(depth, -1, GROUP)],
                           axis=-1).astype(BF16)
    return w_cat, bf_pad, w_q, w_kv


def kernel(x, positions, g_mix_pre, w_in, b_forget, g_q_lora, w_q_up, g_kv_lora, w_kv_up,
           g_mix_out, w_out, g_mix_post, g_ffn_pre, w_ffn_up, w_ffn_down, g_ffn_post):
    depth = w_in.shape[0]
    vec = lambda g: g[:, None, :]
    w_cat, bf_pad, w_q, w_kv = _projection_weights(w_in, b_forget, w_q_up, w_kv_up)
    proj_params = (vec(g_mix_pre), w_cat, bf_pad, vec(g_q_lora), w_q, vec(g_kv_lora), w_kv)
    post_params = (w_out.astype(BF16), vec(g_mix_post), vec(g_ffn_pre),
                   w_ffn_up.astype(BF16), w_ffn_down.astype(BF16), vec(g_ffn_post))
    gains = g_mix_out.reshape(depth, 4, 1, GROUP)
    tab = _rope_tables(positions)
    ret_tables = _retention_decay_tables(min(T_RET, x.shape[1]))
    for layer in range(depth):
        (fq, fkt, fv, mq, mkt, mv, rq, rk, rv, rg, sq, skt, sv) = _projection(
            x, tab, layer, proj_params)
        out_a = _softmax_attention(fq, fkt, fv, gains, layer, 0,
                                   chunk_causal=False, name="fox_attention")
        out_b = _softmax_attention(mq, mkt, mv, gains, layer, 1,
                                   chunk_causal=True, name="mla_attention")
        out_c = _retention(rq, rk, rv, rg, ret_tables, gains, layer, 2)
        out_d = _stick_attention(sq, skt, sv, gains, layer, 3)
        x = _post(x, out_a, out_b, out_c, out_d, layer, post_params)
    return x
```
